```python
import jax
import jax.numpy as jnp
from jax import lax
import numpy as np

D_MODEL = 1024
BATCH = 32
SEQ = 256
DEPTH = 2
DEC_BATCH = 2
DEC_SEQ = 4096
PAST_LEN = 256

GRID_W = 64
R_HEADS = 8
R_HD = 64
R_W = R_HEADS * R_HD
DECAY_LORA = 64
AAA_LORA = 64
GATE_LORA = 128
RWKV_GN_EPS = 64e-5
M_HEADS = 4
M_HD = 128
M_W = M_HEADS * M_HD
M_CHUNK = 64
N_HEADS = 8
N_HD = 64
N_W = N_HEADS * N_HD
WIN_ROWS = 8
WIN_COLS = 16
N_BRANCH = 3
D_FF = 2816
N_EXPERTS = 8
TOP_K = 2
D_FF_EXPERT = 3584
N_DENSE = (DEPTH + 1) // 2
N_MOE = DEPTH // 2
ROPE_BASE = 10000.0
NORM_EPS = 1e-6
PROJ_NAMES = ('r', 'k', 'v', 'wl', 'al', 'gl', 'mq', 'mk', 'mv', 'mo', 'mif', 'nq', 'nk', 'nv', 'gate')
PROJ_WIDTHS = (R_W, R_W, R_W, 2 * DECAY_LORA, 2 * AAA_LORA, GATE_LORA, M_W, M_W, M_W, M_W, 4 * M_HEADS, N_W, N_W, N_W, N_BRANCH * D_MODEL)
P_TOTAL = sum(PROJ_WIDTHS)

kernel_name = 'hybrid_rwkv7_mlstm_natten_prefix_dit'


def _rmsnorm(x, g):
    xf = x.astype(jnp.float32)
    y = xf * lax.rsqrt(jnp.mean(xf * xf, axis=-1, keepdims=True) + NORM_EPS)
    return (y * g.astype(jnp.float32)).astype(x.dtype)


def _to_heads(x, n_heads):
    b, t, _ = x.shape
    return x.reshape(b, t, n_heads, -1).transpose(0, 2, 1, 3)


def _from_heads(x):
    b, h, t, d = x.shape
    return x.transpose(0, 2, 1, 3).reshape(b, t, h * d)


def _split_proj(z):
    offsets = [int(o) for o in np.cumsum(PROJ_WIDTHS)[:-1]]
    return dict(zip(PROJ_NAMES, jnp.split(z, offsets, axis=-1)))


def _axial_rope(x):
    t_len, hd = x.shape[2], x.shape[3]
    nf = hd // 4
    t = jnp.arange(t_len)
    pos = jnp.stack([t // GRID_W, t % GRID_W], axis=-1).astype(jnp.float32)
    inv = ROPE_BASE ** (-jnp.arange(nf, dtype=jnp.float32) / nf)
    ang = pos[:, :, None] * inv
    cos, sin = jnp.cos(ang), jnp.sin(ang)
    xa = x.reshape(x.shape[:3] + (2, 2, nf))
    x1, x2 = xa[..., 0, :], xa[..., 1, :]
    out = jnp.stack([x1 * cos - x2 * sin, x1 * sin + x2 * cos], axis=-2)
    return out.reshape(x.shape).astype(x.dtype)


def _rwkv_scan(r, w, k, v, kk, bb, s0, reverse):
    def step(s, inp):
        r_t, w_t, k_t, v_t, kk_t, b_t = inp
        sa = jnp.einsum('bhij,bhj->bhi', s, kk_t)
        s = s * w_t[:, :, None, :] - sa[..., None] * b_t[:, :, None, :] + v_t[..., None] * k_t[:, :, None, :]
        return s, jnp.einsum('bhij,bhj->bhi', s, r_t)
    xs = tuple(jnp.moveaxis(a, 1, 0) for a in (r, w, k, v, kk, bb))
    s_final, ys = lax.scan(step, s0, xs, reverse=reverse)
    return jnp.moveaxis(ys, 0, 1), s_final


def _rwkv_branch(z, l, P, s0):
    f32 = jnp.float32
    r, k, v = z['r'], z['k'], z['v']
    b, t, _ = r.shape
    wl = jnp.tanh(z['wl'].astype(f32)).reshape(b, t, 2, DECAY_LORA)
    w_pre = P['rw_w0'][l] + jnp.einsum('btdr,drc->btdc', wl, P['rw_w2'][l])
    decay = jnp.exp(-jnp.exp(-jax.nn.softplus(-w_pre) - 0.5))
    al = z['al'].astype(f32).reshape(b, t, 2, AAA_LORA)
    a = jax.nn.sigmoid(P['rw_a0'][l] + jnp.einsum('btdr,drc->btdc', al, P['rw_a2'][l]))
    g = jax.nn.sigmoid(z['gl'].astype(f32)) @ P['rw_g2'][l]
    kf = k.astype(f32)
    kk = (kf * P['rw_k_k'][l]).reshape(b, t, R_HEADS, R_HD)
    kk = kk * lax.rsqrt(jnp.sum(kk * kk, axis=-1, keepdims=True) + 1e-12)
    k_dir = kf[:, :, None] * (1.0 + (a - 1.0) * P['rw_k_a'][l])
    b_dir = kk.reshape(b, t, 1, R_W) * a
    hv = lambda u: u.reshape(b, t, R_HEADS, R_HD)
    rf, vf = hv(r.astype(f32)), hv(v.astype(f32))
    s0 = s0.astype(f32)
    y_f, s_f = _rwkv_scan(rf, hv(decay[:, :, 0]), hv(k_dir[:, :, 0]), vf, kk, hv(b_dir[:, :, 0]), s0[:, 0], False)
    y_b, s_b = _rwkv_scan(rf, hv(decay[:, :, 1]), hv(k_dir[:, :, 1]), vf, kk, hv(b_dir[:, :, 1]), s0[:, 1], True)
    y = y_f + y_b
    mu = jnp.mean(y, axis=-1, keepdims=True)
    var = jnp.mean(jnp.square(y - mu), axis=-1, keepdims=True)
    yn = ((y - mu) * lax.rsqrt(var + RWKV_GN_EPS)).reshape(b, t, R_W) * P['rw_ln_w'][l] + P['rw_ln_b'][l]
    kd5 = k_dir.reshape(b, t, 2, R_HEADS, R_HD)
    bonus = jnp.sum(rf[:, :, None] * kd5 * P['rw_r_k'][l], axis=(2, 4))[..., None] * vf
    out = (yn + bonus.reshape(b, t, R_W)) * g
    return out.astype(r.dtype), jnp.stack([s_f, s_b], axis=1)


def _mlstm_chunkwise(q, k, v, ig, fg, c0, n0, m0):
    b, h, t, _ = q.shape
    nc = t // M_CHUNK
    chunks = lambda u: jnp.moveaxis(u.reshape((b, h, nc, M_CHUNK) + u.shape[3:]), 2, 0)
    causal = jnp.tril(jnp.ones((M_CHUNK, M_CHUNK), dtype=bool))

    def step(carry, inp):
        c, n, m = carry
        qc, kc, vc, ic, fc = inp
        bsum = jnp.cumsum(fc, axis=-1)
        dmat = jnp.where(causal, bsum[..., :, None] - bsum[..., None, :] + ic[..., None, :], -jnp.inf)
        inter = bsum + m[..., None]
        mt = jnp.maximum(inter, jnp.max(dmat, axis=-1))
        s = jnp.einsum('bhtd,bhsd->bhts', qc, kc) * jnp.exp(dmat - mt[..., None])
        iw = jnp.exp(inter - mt)
        num = iw[..., None] * jnp.einsum('bhtd,bhde->bhte', qc, c) + jnp.einsum('bhts,bhse->bhte', s, vc)
        den = iw * jnp.einsum('bhtd,bhd->bht', qc, n) + jnp.sum(s, axis=-1)
        hout = num / jnp.maximum(jnp.abs(den), jnp.exp(-mt))[..., None]
        blast = bsum[..., -1]
        wlog = blast[..., None] - bsum + ic
        m_new = jnp.maximum(blast + m, jnp.max(wlog, axis=-1))
        dec = jnp.exp(blast + m - m_new)
        ws = jnp.exp(wlog - m_new[..., None])
        c_new = dec[..., None, None] * c + jnp.einsum('bhs,bhsd,bhse->bhde', ws, kc, vc)
        n_new = dec[..., None] * n + jnp.einsum('bhs,bhsd->bhd', ws, kc)
        return (c_new, n_new, m_new), hout

    (c, n, m), hs = lax.scan(step, (c0, n0, m0), tuple(chunks(u) for u in (q, k, v, ig, fg)))
    return jnp.moveaxis(hs, 0, 2).reshape(b, h, t, -1), c, n, m


def _mlstm_branch(z, l, P, c0, n0, m0, use_rope):
    f32 = jnp.float32
    b, t, _ = z['mq'].shape
    q = _to_heads(z['mq'].astype(f32), M_HEADS)
    k = _to_heads(z['mk'].astype(f32), M_HEADS)
    v = _to_heads(z['mv'].astype(f32), M_HEADS)
    if use_rope:
        q, k = _axial_rope(q), _axial_rope(k)
    k = k * (M_HD ** -0.5)
    gts = z['mif'].astype(f32).reshape(b, t, 2, 2, M_HEADS)
    ig = (gts[:, :, :, 0] + P['ml_b_i'][l]).transpose(0, 2, 3, 1)
    fg = jax.nn.log_sigmoid(gts[:, :, :, 1] + P['ml_b_f'][l]).transpose(0, 2, 3, 1)
    c0, n0, m0 = c0.astype(f32), n0.astype(f32), m0.astype(f32)
    h_f, c_f, n_f, m_f = _mlstm_chunkwise(q, k, v, ig[:, 0], fg[:, 0], c0[:, 0], n0[:, 0], m0[:, 0])
    flip = lambda u: jnp.flip(u, axis=2)
    h_b, c_b, n_b, m_b = _mlstm_chunkwise(flip(q), flip(k), flip(v), flip(ig[:, 1]), flip(fg[:, 1]), c0[:, 1], n0[:, 1], m0[:, 1])
    hsum = h_f + flip(h_b)
    mu = jnp.mean(hsum, axis=-1, keepdims=True)
    var = jnp.mean(jnp.square(hsum - mu), axis=-1, keepdims=True)
    hn = _from_heads((hsum - mu) * lax.rsqrt(var + NORM_EPS)) * P['ml_norm_w'][l]
    out = jax.nn.sigmoid(z['mo'].astype(f32)) * hn
    return (out.astype(z['mq'].dtype), jnp.stack([c_f, c_b], axis=1), jnp.stack([n_f, n_b], axis=1), jnp.stack([m_f, m_b], axis=1))


def _context_attention(q, k, v):
    s = jnp.einsum('bhtd,bhsd->bhts', q, k).astype(jnp.float32) * (N_HD ** -0.5)
    p = jax.nn.softmax(s, axis=-1)
    return jnp.einsum('bhts,bhsd->bhtd', p.astype(v.dtype), v)


def _natten_latent(q, k, v, k_ctx, v_ctx, rpb):
    b, h, t, hd = q.shape
    rows_n = t // GRID_W
    kr = min(WIN_ROWS, rows_n)
    qg = q.reshape(b, h, rows_n, GRID_W, hd) * (hd ** -0.5)
    kg = k.reshape(b, h, rows_n, GRID_W, hd)
    vg = v.reshape(b, h, rows_n, GRID_W, hd)
    rows = jnp.arange(rows_n)
    rstart = jnp.clip(rows - kr // 2, 0, rows_n - kr)
    ridx = rstart[:, None] + jnp.arange(kr)
    k_rows = kg[:, :, ridx]
    v_rows = vg[:, :, ridx]
    cols = jnp.arange(GRID_W)
    cstart = jnp.clip(cols - WIN_COLS // 2, 0, GRID_W - WIN_COLS)
    in_band = (cols[None, :] >= cstart[:, None]) & (cols[None, :] < cstart[:, None] + WIN_COLS)
    dr = ridx - rows[:, None] + (WIN_ROWS - 1)
    dc = jnp.clip(cols[None, :] - cols[:, None], -(WIN_COLS - 1), WIN_COLS - 1) + (WIN_COLS - 1)
    bias = rpb[:, dr[:, None, :, None], dc[None, :, None, :]].astype(jnp.float32)
    s_loc = jnp.einsum('bhrcd,bhrkwd->bhrckw', qg, k_rows).astype(jnp.float32) + bias[None]
    s_loc = jnp.where(in_band[:, None, :], s_loc, -jnp.inf)
    s_ctx = jnp.einsum('bhrcd,bhsd->bhrcs', qg, k_ctx).astype(jnp.float32)
    n_loc = kr * GRID_W
    p = jax.nn.softmax(jnp.concatenate([s_loc.reshape(b, h, rows_n, GRID_W, n_loc), s_ctx], axis=-1), axis=-1)
    p_loc = p[..., :n_loc].reshape(b, h, rows_n, GRID_W, kr, GRID_W).astype(v.dtype)
    p_ctx = p[..., n_loc:].astype(v.dtype)
    out = jnp.einsum('bhrckw,bhrkwd->bhrcd', p_loc, v_rows) + jnp.einsum('bhrcs,bhsd->bhrcd', p_ctx, v_ctx)
    return out.reshape(b, h, t, hd)


def _token_mixer(hin, l, P, cache):
    b, t, _ = hin.shape
    z = _split_proj(hin @ P['w_in'][l])
    if cache is None:
        s0 = jnp.zeros((b, 2, R_HEADS, R_HD, R_HD), jnp.float32)
        c0 = jnp.zeros((b, 2, M_HEADS, M_HD, M_HD), jnp.float32)
        n0 = jnp.zeros((b, 2, M_HEADS, M_HD), jnp.float32)
        m0 = jnp.zeros((b, 2, M_HEADS), jnp.float32)
    else:
        s0, c0, n0, m0, k_ctx, v_ctx = cache
    y_r, s_rw = _rwkv_branch(z, l, P, s0)
    y_m, c_m, n_m, m_m = _mlstm_branch(z, l, P, c0, n0, m0, cache is not None)
    q, k, v = (_to_heads(z[nm], N_HEADS) for nm in ('nq', 'nk', 'nv'))
    if cache is None:
        y_n = _context_attention(q, k, v)
    else:
        y_n = _natten_latent(q, k, v, k_ctx, v_ctx, P['nat_rpb'][l])
    y_n = _from_heads(y_n)
    gates = jax.nn.sigmoid(z['gate']).reshape(b, t, N_BRANCH, D_MODEL)
    merged = (gates[:, :, 0] * (y_r @ P['w_o_rwkv'][l]) + gates[:, :, 1] * (y_m @ P['w_o_mlstm'][l])
              + gates[:, :, 2] * (y_n @ P['w_o_nat'][l]))
    return merged @ P['w_out'][l], (s_rw, c_m, n_m, m_m, k, v)


def _swiglu(x, wg, wu, wd):
    return (jax.nn.silu(x @ wg) * (x @ wu)) @ wd


def _moe(x, wr, br, wg, wu, wd):
    logits = (x @ wr + br).astype(jnp.float32)
    top_v, top_i = lax.top_k(logits, TOP_K)
    wts = jax.nn.softmax(top_v, axis=-1)
    gate = jnp.sum(jax.nn.one_hot(top_i, N_EXPERTS, dtype=jnp.float32) * wts[..., None], axis=-2).astype(x.dtype)
    out = jnp.zeros_like(x)
    for e in range(N_EXPERTS):
        out = out + gate[..., e:e + 1] * _swiglu(x, wg[e], wu[e], wd[e])
    return out


def _layer(x, mod, l, P, cache):
    sh1, sc1, g1, sh2, sc2, g2 = jnp.split(mod, 6, axis=-1)
    hin = _rmsnorm(x, P['g_pre_mix'][l]) * (1.0 + sc1) + sh1
    o, st = _token_mixer(hin, l, P, cache)
    x = x + g1 * _rmsnorm(o, P['g_post_mix'][l])
    hin = _rmsnorm(x, P['g_pre_ffn'][l]) * (1.0 + sc2) + sh2
    j = l // 2
    if l % 2 == 0:
        f = _swiglu(hin, P['ff_w_gate'][j], P['ff_w_up'][j], P['ff_w_down'][j])
    else:
        f = _moe(hin, P['moe_w_router'][j], P['moe_b_router'][j], P['moe_w_gate'][j], P['moe_w_up'][j], P['moe_w_down'][j])
    x = x + g2 * _rmsnorm(f, P['g_post_ffn'][l])
    return x, st


def setup_inputs(seed: int = 0) -> dict:
    key = jax.random.key(seed)
    ks = iter(jax.random.split(key, 64))

    def nrm(shape, scale=1.0):
        return jax.random.normal(next(ks), shape, jnp.float32) * scale

    L, D = DEPTH, D_MODEL
    inp = {}
    inp['x_prompt'] = nrm((BATCH, SEQ, D))
    inp['x_sample'] = nrm((DEC_BATCH, DEC_SEQ, D))
    inp['state_rwkv'] = nrm((DEC_BATCH, L, 2, R_HEADS, R_HD, R_HD), 0.5)
    inp['state_mlstm_C'] = nrm((DEC_BATCH, L, 2, M_HEADS, M_HD, M_HD), 0.5)
    inp['state_mlstm_n'] = nrm((DEC_BATCH, L, 2, M_HEADS, M_HD), 0.5)
    inp['state_mlstm_m'] = nrm((DEC_BATCH, L, 2, M_HEADS), 1.0)
    inp['cache_nat_k'] = nrm((DEC_BATCH, L, N_HEADS, PAST_LEN, N_HD))
    inp['cache_nat_v'] = nrm((DEC_BATCH, L, N_HEADS, PAST_LEN, N_HD))
    inp['c'] = nrm((DEC_BATCH, D))
    inp['c_ctx'] = nrm((D,))
    inp['w_mod'] = nrm((L, D, 6 * D), 0.5 * D ** -0.5)
    inp['b_mod'] = nrm((L, 6 * D), 0.02)
    inp['g_pre_mix'] = 1.0 + nrm((L, D), 0.02)
    inp['g_post_mix'] = 1.0 + nrm((L, D), 0.02)
    inp['g_pre_ffn'] = 1.0 + nrm((L, D), 0.02)
    inp['g_post_ffn'] = 1.0 + nrm((L, D), 0.02)
    inp['w_in'] = nrm((L, D, P_TOTAL), D ** -0.5)
    inp['rw_w0'] = -1.5 + nrm((L, 2, R_W), 0.5)
    inp['rw_w2'] = nrm((L, 2, DECAY_LORA, R_W), 0.1)
    inp['rw_a0'] = nrm((L, 2, R_W), 0.1)
    inp['rw_a2'] = nrm((L, 2, AAA_LORA, R_W), 0.1)
    inp['rw_g2'] = nrm((L, GATE_LORA, R_W), GATE_LORA ** -0.5)
    inp['rw_k_k'] = 0.85 + nrm((L, R_W), 0.02)
    inp['rw_k_a'] = 1.0 + nrm((L, R_W), 0.02)
    inp['rw_r_k'] = nrm((L, R_HEADS, R_HD), 0.1)
    inp['rw_ln_w'] = 1.0 + nrm((L, R_W), 0.02)
    inp['rw_ln_b'] = nrm((L, R_W), 0.02)
    inp['ml_b_i'] = nrm((L, 2, M_HEADS), 0.1)
    inp['ml_b_f'] = 3.0 + nrm((L, 2, M_HEADS), 0.5)
    inp['ml_norm_w'] = 1.0 + nrm((L, M_W), 0.02)
    inp['nat_rpb'] = nrm((L, N_HEADS, 2 * WIN_ROWS - 1, 2 * WIN_COLS - 1), 0.1)
    inp['w_o_rwkv'] = nrm((L, R_W, D), R_W ** -0.5)
    inp['w_o_mlstm'] = nrm((L, M_W, D), M_W ** -0.5)
    inp['w_o_nat'] = nrm((L, N_W, D), N_W ** -0.5)
    inp['w_out'] = nrm((L, D, D), D ** -0.5)
    inp['ff_w_gate'] = nrm((N_DENSE, D, D_FF), D ** -0.5)
    inp['ff_w_up'] = nrm((N_DENSE, D, D_FF), D ** -0.5)
    inp['ff_w_down'] = nrm((N_DENSE, D_FF, D), D_FF ** -0.5)
    inp['moe_w_router'] = nrm((N_MOE, D, N_EXPERTS), D ** -0.5)
    inp['moe_b_router'] = nrm((N_MOE, N_EXPERTS), 0.01)
    inp['moe_w_gate'] = nrm((N_MOE, N_EXPERTS, D, D_FF_EXPERT), D ** -0.5)
    inp['moe_w_up'] = nrm((N_MOE, N_EXPERTS, D, D_FF_EXPERT), D ** -0.5)
    inp['moe_w_down'] = nrm((N_MOE, N_EXPERTS, D_FF_EXPERT, D), D_FF_EXPERT ** -0.5)
    return inp


def reference(x_prompt, x_sample, state_rwkv, state_mlstm_C, state_mlstm_n, state_mlstm_m, cache_nat_k, cache_nat_v,
              c, c_ctx, w_mod, b_mod, g_pre_mix, g_post_mix, g_pre_ffn, g_post_ffn, w_in,
              rw_w0, rw_w2, rw_a0, rw_a2, rw_g2, rw_k_k, rw_k_a, rw_r_k, rw_ln_w, rw_ln_b,
              ml_b_i, ml_b_f, ml_norm_w, nat_rpb, w_o_rwkv, w_o_mlstm, w_o_nat, w_out,
              ff_w_gate, ff_w_up, ff_w_down, moe_w_router, moe_b_router, moe_w_gate, moe_w_up, moe_w_down):
    P = dict(w_in=w_in, g_pre_mix=g_pre_mix, g_post_mix=g_post_mix, g_pre_ffn=g_pre_ffn, g_post_ffn=g_post_ffn,
             rw_w0=rw_w0, rw_w2=rw_w2, rw_a0=rw_a0, rw_a2=rw_a2, rw_g2=rw_g2, rw_k_k=rw_k_k, rw_k_a=rw_k_a,
             rw_r_k=rw_r_k, rw_ln_w=rw_ln_w, rw_ln_b=rw_ln_b, ml_b_i=ml_b_i, ml_b_f=ml_b_f, ml_norm_w=ml_norm_w,
             nat_rpb=nat_rpb, w_o_rwkv=w_o_rwkv, w_o_mlstm=w_o_mlstm, w_o_nat=w_o_nat, w_out=w_out,
             ff_w_gate=ff_w_gate, ff_w_up=ff_w_up, ff_w_down=ff_w_down, moe_w_router=moe_w_router,
             moe_b_router=moe_b_router, moe_w_gate=moe_w_gate, moe_w_up=moe_w_up, moe_w_down=moe_w_down)
    x = x_prompt
    per_layer = []
    for l in range(DEPTH):
        mod = (jax.nn.silu(c_ctx) @ w_mod[l] + b_mod[l])[None, None, :]
        x, st = _layer(x, mod, l, P, None)
        per_layer.append(st)
    y_prompt = x
    new_rwkv = jnp.stack([s[0] for s in per_layer], axis=1)
    new_C = jnp.stack([s[1] for s in per_layer], axis=1)
    new_n = jnp.stack([s[2] for s in per_layer], axis=1)
    new_m = jnp.stack([s[3] for s in per_layer], axis=1)
    new_k = jnp.stack([s[4] for s in per_layer], axis=1)
    new_v = jnp.stack([s[5] for s in per_layer], axis=1)
    x = x_sample
    for l in range(DEPTH):
        mod = (jax.nn.silu(c) @ w_mod[l] + b_mod[l])[:, None, :]
        cache = (state_rwkv[:, l], state_mlstm_C[:, l], state_mlstm_n[:, l], state_mlstm_m[:, l], cache_nat_k[:, l], cache_nat_v[:, l])
        x, _ = _layer(x, mod, l, P, cache)
    y_sample = x
    return (y_prompt, y_sample, new_rwkv, new_C, new_n, new_m, new_k, new_v)
```

```python
import functools
import math

import jax
import jax.numpy as jnp
from jax import lax
from jax.experimental import pallas as pl
from jax.experimental.pallas import tpu as pltpu

F32 = jnp.float32
BF16 = jnp.bfloat16

D = 1024
N_CTX_B, N_CTX_T = 32, 256
N_LAT_B, N_LAT_T = 2, 4096
DEPTH = 2
GRID_W = 64
R_HEADS, R_HD = 8, 64
M_HEADS, M_HD = 4, 128
N_HEADS, N_HD = 8, 64
HW = 512
WIN_ROWS, WIN_COLS = 8, 16
N_EXPERTS = 8
RWKV_GN_EPS = 64e-5
NORM_EPS = 1e-6
ROPE_BASE = 10000.0
CHUNK = 64
NEG = -1e30
LANE = 128
VMEM_LIMIT = 56 * 1024 * 1024

ZB_R, ZB_K, ZB_V = 0, 4, 8
ZB_MQ, ZB_MK, ZB_MV, ZB_MO = 12, 16, 20, 24
ZB_NQ, ZB_NK, ZB_NV = 28, 32, 36
ZB_GATE = 40
ZB_WL, ZB_AL, ZB_GL, ZB_MIF = 64, 65, 66, 67
P_PAD = 68 * LANE

NN = (((1,), (0,)), ((), ()))
NT = (((1,), (1,)), ((), ()))
TN = (((0,), (0,)), ((), ()))


def _dot(a, b, dims=NN):
    return lax.dot_general(a.astype(BF16), b.astype(BF16), dims, preferred_element_type=F32)


def _split(x):
    hi = x.astype(BF16)
    return hi, (x - hi.astype(F32)).astype(BF16)


def _dot3(a, b, dims=NN):
    ah, al = _split(a)
    bh, bl = _split(b)
    d = lambda u, w: lax.dot_general(u, w, dims, preferred_element_type=F32)
    return d(ah, bh) + (d(ah, bl) + d(al, bh))


def _dot_lhs2(a, b_exact, dims=NN):
    ah, al = _split(a)
    d = lambda u: lax.dot_general(u, b_exact, dims, preferred_element_type=F32)
    return d(ah) + d(al)


def _dot_rhs2(a_exact, b, dims=NN):
    bh, bl = _split(b)
    d = lambda w: lax.dot_general(a_exact, w, dims, preferred_element_type=F32)
    return d(bh) + d(bl)


def _log_sigmoid(x):
    return jnp.minimum(x, 0.0) - jnp.log1p(jnp.exp(-jnp.abs(x)))


def _rms(x, g):
    return x * lax.rsqrt(jnp.mean(x * x, axis=-1, keepdims=True) + NORM_EPS) * g


def _params(sem):
    return pltpu.CompilerParams(dimension_semantics=sem, vmem_limit_bytes=VMEM_LIMIT)


def _mod_body(c_ref, w_ref, b_ref, o_ref):
    c = c_ref[...]
    o_ref[...] = _dot(c * jax.nn.sigmoid(c), w_ref[...]) + b_ref[...]


def _mod_call(cvec, w_mod, b_mod):
    tn = 1536
    return pl.pallas_call(
        _mod_body,
        out_shape=jax.ShapeDtypeStruct((8, 6 * D), F32),
        grid=(6 * D // tn,),
        in_specs=[pl.BlockSpec((8, D), lambda j: (0, 0)),
                  pl.BlockSpec((D, tn), lambda j: (0, j)),
                  pl.BlockSpec((1, tn), lambda j: (0, j))],
        out_specs=pl.BlockSpec((8, tn), lambda j: (0, j)),
        compiler_params=_params(("arbitrary",)),
        name="adaln_mod",
    )(cvec, w_mod, b_mod.reshape(1, 6 * D))


def _mod_index(tile_rows, n_ctx, lat_t):
    def f(i):
        start = i * tile_rows
        return jnp.where(start < n_ctx, 0, 1 + (start - n_ctx) // lat_t)
    return f


def _proj_body(x_ref, mod_ref, g_ref, w_ref, z_ref, hin_s):
    @pl.when(pl.program_id(1) == 0)
    def _():
        m = mod_ref[0]
        hin_s[...] = (_rms(x_ref[...], g_ref[...]) * (1.0 + m[1:2]) + m[0:1]).astype(BF16)
    z_ref[...] = jnp.dot(hin_s[...], w_ref[...], preferred_element_type=F32)


def _proj_call(x, mod3, g, w_bf, n_ctx, lat_t, tm=512, tn=2176):
    ntok = x.shape[0]
    midx = _mod_index(tm, n_ctx, lat_t)
    return pl.pallas_call(
        _proj_body,
        out_shape=jax.ShapeDtypeStruct((ntok, P_PAD), F32),
        grid=(ntok // tm, P_PAD // tn),
        in_specs=[pl.BlockSpec((tm, D), lambda i, j: (i, 0)),
                  pl.BlockSpec((1, 8, D), lambda i, j: (midx(i), 0, 0)),
                  pl.BlockSpec((1, D), lambda i, j: (0, 0)),
                  pl.BlockSpec((D, tn), lambda i, j: (0, j))],
        out_specs=pl.BlockSpec((tm, tn), lambda i, j: (i, j)),
        scratch_shapes=[pltpu.VMEM((tm, D), BF16)],
        compiler_params=_params(("parallel", "arbitrary")),
        name="in_proj",
    )(x, mod3, g.reshape(1, D), w_bf)


def _rwkv_body(*refs, nchunk, has_init, has_out):
    (r_ref, k_ref, v_ref, wl_ref, al_ref, w0_ref, w2_ref, a0_ref, a2_ref,
     kkw_ref, kaw_ref, rkw_ref) = refs[:12]
    rest = list(refs[12:])
    s0_ref = rest.pop(0) if has_init else None
    y_ref, bv_ref = rest.pop(0), rest.pop(0)
    sout_ref = rest.pop(0) if has_out else None
    lw_s, kd_s, bd_s, kn_s, st_s = rest
    C = CHUNK
    d = pl.program_id(2)
    tb = pl.program_id(3)
    n_tb = pl.num_programs(3)

    @pl.when(tb == 0)
    def _():
        if has_init:
            st_s[...] = s0_ref[0, 0]
        else:
            st_s[...] = jnp.zeros_like(st_s)

    r = r_ref[...]
    kf = k_ref[...]
    v = v_ref[...]
    w_pre = w0_ref[0] + _dot3(jnp.tanh(wl_ref[...]), w2_ref[0])
    lw_s[...] = -math.exp(-0.5) * jax.nn.sigmoid(w_pre)
    a = jax.nn.sigmoid(a0_ref[0] + _dot3(al_ref[...], a2_ref[0]))
    kd = kf * (1.0 + (a - 1.0) * kaw_ref[...])
    kd_s[...] = kd
    kkf = kf * kkw_ref[...]
    rk = r * kd * rkw_ref[...]
    for h in range(2):
        sl = slice(R_HD * h, R_HD * (h + 1))
        kkh = kkf[:, sl]
        kkh = kkh * lax.rsqrt(jnp.sum(kkh * kkh, axis=-1, keepdims=True) + 1e-12)
        kn_s[:, sl] = kkh
        bd_s[:, sl] = kkh * a[:, sl]
        bv_ref[0, :, sl] = jnp.sum(rk[:, sl], axis=-1, keepdims=True) * v[:, sl]

    ti = lax.broadcasted_iota(jnp.int32, (C, C), 0)
    si = lax.broadcasted_iota(jnp.int32, (C, C), 1)
    dlt = (ti - si) * (1 - 2 * d)
    strict = dlt > 0
    incl = dlt >= 0
    incl_bf = jnp.where(incl, 1.0, 0.0).astype(BF16)
    eye = jnp.where(ti == si, 1.0, 0.0)
    pair = (ti >> 1) == (si >> 1)
    sibling = [((ti >> lg) ^ (si >> lg)) == 1 for lg in range(1, 6)]

    def chunk(i, carry):
        ci = jnp.where(d == 0, i, nchunk - 1 - i)
        rows = pl.ds(pl.multiple_of(ci * C, C), C)
        for h in range(2):
            sl = slice(R_HD * h, R_HD * (h + 1))
            lw = lw_s[rows, sl]
            rc = r_ref[rows, sl]
            vc = v_ref[rows, sl]
            kdc = kd_s[rows, sl]
            bdc = bd_s[rows, sl]
            knc = kn_s[rows, sl]
            cum = _dot_rhs2(incl_bf, lw)
            tot = jnp.sum(lw, axis=0, keepdims=True)
            e_neg = jnp.exp(-cum)
            e_hat = jnp.exp(tot - cum)
            rt = rc * jnp.exp(cum)
            at = -knc * jnp.exp(cum - lw)
            lhs = jnp.concatenate([at, rt], axis=0)
            ab = _dot3(lhs, bdc * e_neg, NT)
            ak = _dot3(lhs, kdc * e_neg, NT)
            n_mat = jnp.where(strict, ab[:C], 0.0)
            a_ak = jnp.where(strict, ak[:C], 0.0)
            a_rb = jnp.where(incl, ab[C:], 0.0)
            a_rk = jnp.where(incl, ak[C:], 0.0)
            t_inv = eye + jnp.where(pair, n_mat, 0.0)
            for sib in sibling:
                x_mat = _dot3(jnp.where(sib, n_mat, 0.0), t_inv)
                t_inv = t_inv + _dot3(t_inv, x_mat)
            p_mat = _dot3(t_inv, at)
            q_mat = _dot3(t_inv, _dot3(a_ak, vc))
            s0 = st_s[h]
            uy = _dot3(jnp.concatenate([p_mat, rt], axis=0), s0, NT)
            u = uy[:C] + q_mat
            y_ref[0, rows, sl] = uy[C:] + _dot(a_rb, u) + _dot(a_rk, vc)
            st_s[h] = (s0 * jnp.exp(tot) + _dot3(u, bdc * e_hat, TN)
                       + _dot3(vc, kdc * e_hat, TN))
        return carry

    lax.fori_loop(0, nchunk, chunk, 0)

    if has_out:
        @pl.when(tb == n_tb - 1)
        def _():
            sout_ref[0, 0] = st_s[...]


def _rwkv_call(z, P, s0, *, row0, nb, t, tb, has_out):
    n_tb = t // tb
    blk0 = row0 // tb
    has_init = s0 is not None

    def rowblk(b, d, i):
        return blk0 + b * n_tb + jnp.where(d == 0, i, n_tb - 1 - i)

    def zspec(cb):
        return pl.BlockSpec((tb, LANE), lambda b, hp, d, i: (rowblk(b, d, i), cb(hp)))

    vec = lambda: pl.BlockSpec((1, LANE), lambda b, hp, d, i: (0, hp))
    dvec = lambda: pl.BlockSpec((1, 1, LANE), lambda b, hp, d, i: (d, 0, hp))
    dmat = lambda: pl.BlockSpec((1, LANE, LANE), lambda b, hp, d, i: (d, 0, hp))
    st_spec = pl.BlockSpec((1, 1, 2, R_HD, R_HD), lambda b, hp, d, i: (b, d, hp, 0, 0))
    in_specs = [zspec(lambda hp: ZB_R + hp), zspec(lambda hp: ZB_K + hp), zspec(lambda hp: ZB_V + hp),
                zspec(lambda hp: ZB_WL), zspec(lambda hp: ZB_AL),
                dvec(), dmat(), dvec(), dmat(), vec(), vec(), vec()]
    args = [z, z, z, z, z, P['w0'], P['w2'], P['a0'], P['a2'], P['kk'], P['ka'], P['rk']]
    if has_init:
        in_specs.append(st_spec)
        args.append(s0)
    yspec = pl.BlockSpec((1, tb, LANE), lambda b, hp, d, i: (d, rowblk(b, d, i) - blk0, hp))
    out_shape = [jax.ShapeDtypeStruct((2, nb * t, HW), F32)] * 2
    out_specs = [yspec, yspec]
    if has_out:
        out_shape.append(jax.ShapeDtypeStruct((nb, 2, R_HEADS, R_HD, R_HD), F32))
        out_specs.append(st_spec)
    return pl.pallas_call(
        functools.partial(_rwkv_body, nchunk=tb // CHUNK, has_init=has_init, has_out=has_out),
        out_shape=out_shape,
        grid=(nb, R_HEADS // 2, 2, n_tb),
        in_specs=in_specs,
        out_specs=out_specs,
        scratch_shapes=[pltpu.VMEM((tb, LANE), F32)] * 4 + [pltpu.VMEM((2, R_HD, R_HD), F32)],
        compiler_params=_params(("parallel", "parallel", "arbitrary", "arbitrary")),
        name="rwkv7_chunked",
    )(*args)


def _mlstm_body(*refs, nchunk, use_rope, has_init, has_out):
    refs = list(refs)
    q_ref, k_ref, v_ref, g_ref, gt_ref = [refs.pop(0) for _ in range(5)]
    cos_ref, sa_ref, sb_ref = [refs.pop(0) for _ in range(3)] if use_rope else (None,) * 3
    bi_ref, bf_ref = refs.pop(0), refs.pop(0)
    c0_ref, n0_ref, m0_ref = [refs.pop(0) for _ in range(3)] if has_init else (None,) * 3
    h_ref = refs.pop(0)
    cout_ref, nout_ref, mout_ref = [refs.pop(0) for _ in range(3)] if has_out else (None,) * 3
    ic_s, fc_s, c_s, n_s, m_s = refs
    L = CHUNK
    b = pl.program_id(0)
    h = pl.program_id(1)
    d = pl.program_id(2)
    tb = pl.program_id(3)
    n_tb = pl.num_programs(3)
    bi = bi_ref[d, h]
    bf = bf_ref[d, h]
    ji = d * (2 * M_HEADS) + h
    jf = ji + M_HEADS

    @pl.when(tb == 0)
    def _():
        if has_init:
            c_s[...] = c0_ref[0, 0, 0]
            n_s[...] = n0_ref[0, 0, 0]
            m_s[...] = jnp.full(m_s.shape, m0_ref[b, d, h], F32)
        else:
            c_s[...] = jnp.zeros_like(c_s)
            n_s[...] = jnp.zeros_like(n_s)
            m_s[...] = jnp.zeros_like(m_s)

    g = g_ref[...]
    lane = lax.broadcasted_iota(jnp.int32, g.shape, 1)
    ic_s[...] = jnp.sum(jnp.where(lane == ji, g, 0.0), axis=1, keepdims=True) + bi
    fc_s[...] = _log_sigmoid(jnp.sum(jnp.where(lane == jf, g, 0.0), axis=1, keepdims=True) + bf)

    ti = lax.broadcasted_iota(jnp.int32, (L, L), 0)
    si = lax.broadcasted_iota(jnp.int32, (L, L), 1)
    sgn = 1 - 2 * d
    incl = (ti - si) * sgn >= 0
    incl_t = (si - ti) * sgn >= 0

    def chunk(i, carry):
        ci = jnp.where(d == 0, i, nchunk - 1 - i)
        rows = pl.ds(pl.multiple_of(ci * L, L), L)
        qc = q_ref[rows, :]
        kc = k_ref[rows, :]
        vc = v_ref[rows, :]
        if use_rope:
            cos, sa, sb = cos_ref[rows, :], sa_ref[rows, :], sb_ref[rows, :]
            qc = qc * cos + pltpu.roll(qc, 96, 1) * sa + pltpu.roll(qc, 32, 1) * sb
            kc = kc * cos + pltpu.roll(kc, 96, 1) * sa + pltpu.roll(kc, 32, 1) * sb
        kc = kc * (M_HD ** -0.5)
        icol = ic_s[rows, :]
        fcol = fc_s[rows, :]
        irow = gt_ref[ci, pl.ds(ji, 1), :] + bi
        frow = _log_sigmoid(gt_ref[ci, pl.ds(jf, 1), :] + bf)
        bsum_col = jnp.sum(jnp.where(incl, frow, 0.0), axis=1, keepdims=True)
        bsum_row = jnp.sum(jnp.where(incl_t, fcol, 0.0), axis=0, keepdims=True)
        m = m_s[:, 0:1]
        dmat = jnp.where(incl, bsum_col - bsum_row + irow, NEG)
        inter = bsum_col + m
        mt = jnp.maximum(inter, jnp.max(dmat, axis=1, keepdims=True))
        sm = _dot(qc, kc, NT) * jnp.exp(dmat - mt)
        iw = jnp.exp(inter - mt)
        c = c_s[...]
        n = n_s[...]
        num = iw * _dot(qc, c) + _dot(sm, vc)
        den = iw * jnp.sum(qc * n, axis=1, keepdims=True) + jnp.sum(sm, axis=1, keepdims=True)
        h_ref[0, rows, :] = num / jnp.maximum(jnp.abs(den), jnp.exp(-mt))
        blast = jnp.sum(frow, axis=1, keepdims=True)
        wlog = blast - bsum_col + icol
        m_new = jnp.maximum(blast + m, jnp.max(wlog, axis=0, keepdims=True))
        dec = jnp.exp(blast + m - m_new)
        kw = kc * jnp.exp(wlog - m_new)
        c_s[...] = dec * c + _dot3(kw, vc, TN)
        n_s[...] = dec * n + jnp.sum(kw, axis=0, keepdims=True)
        m_s[...] = jnp.broadcast_to(m_new, m_s.shape)
        return carry

    lax.fori_loop(0, nchunk, chunk, 0)

    if has_out:
        @pl.when(tb == n_tb - 1)
        def _():
            cout_ref[0, 0, 0] = c_s[...]
            nout_ref[0, 0, 0] = n_s[...]
            mout_ref[0, 0, 0] = m_s[...]


def _mlstm_call(z, gt, rope, b_i, b_f, init, *, row0, nb, t, tb, has_out):
    n_tb = t // tb
    blk0 = row0 // tb
    cpb = tb // CHUNK
    use_rope = rope is not None
    has_init = init is not None

    def rowblk(b, d, i):
        return blk0 + b * n_tb + jnp.where(d == 0, i, n_tb - 1 - i)

    def zspec(cb):
        return pl.BlockSpec((tb, LANE), lambda b, h, d, i: (rowblk(b, d, i), cb(h)))

    smem = pl.BlockSpec(memory_space=pltpu.SMEM)
    in_specs = [zspec(lambda h: ZB_MQ + h), zspec(lambda h: ZB_MK + h), zspec(lambda h: ZB_MV + h),
                zspec(lambda h: ZB_MIF),
                pl.BlockSpec((cpb, 16, CHUNK), lambda b, h, d, i: (rowblk(b, d, i), 0, 0))]
    args = [z, z, z, z, gt]
    if use_rope:
        tspec = pl.BlockSpec((tb, LANE), lambda b, h, d, i: (jnp.where(d == 0, i, n_tb - 1 - i), 0))
        in_specs += [tspec] * 3
        args += list(rope)
    in_specs += [smem, smem]
    args += [b_i, b_f]
    cspec = pl.BlockSpec((1, 1, 1, M_HD, M_HD), lambda b, h, d, i: (b, d, h, 0, 0))
    nspec = pl.BlockSpec((1, 1, 1, 1, M_HD), lambda b, h, d, i: (b, d, h, 0, 0))
    if has_init:
        in_specs += [cspec, nspec, smem]
        args += list(init)
    out_shape = [jax.ShapeDtypeStruct((2, nb * t, HW), F32)]
    out_specs = [pl.BlockSpec((1, tb, LANE), lambda b, h, d, i: (d, rowblk(b, d, i) - blk0, h))]
    if has_out:
        out_shape += [jax.ShapeDtypeStruct((nb, 2, M_HEADS, M_HD, M_HD), F32),
                      jax.ShapeDtypeStruct((nb, 2, M_HEADS, 1, M_HD), F32),
                      jax.ShapeDtypeStruct((nb, 2, M_HEADS, 1, M_HD), F32)]
        out_specs += [cspec, nspec, nspec]
    return pl.pallas_call(
        functools.partial(_mlstm_body, nchunk=cpb, use_rope=use_rope, has_init=has_init, has_out=has_out),
        out_shape=out_shape,
        grid=(nb, M_HEADS, 2, n_tb),
        in_specs=in_specs,
        out_specs=out_specs,
        scratch_shapes=[pltpu.VMEM((tb, 1), F32), pltpu.VMEM((tb, 1), F32),
                        pltpu.VMEM((M_HD, M_HD), F32), pltpu.VMEM((1, M_HD), F32), pltpu.VMEM((1, M_HD), F32)],
        compiler_params=_params(("parallel", "parallel", "arbitrary", "arbitrary")),
        name="mlstm_chunkwise",
    )(*args)


def _ctx_attn_body(q_ref, k_ref, v_ref, y_ref, ko_ref, vo_ref):
    q = q_ref[...] * (N_HD ** -0.5)
    k = k_ref[...]
    v = v_ref[...]
    for h in range(2):
        sl = slice(N_HD * h, N_HD * (h + 1))
        kh, vh = k[:, sl], v[:, sl]
        ko_ref[0, h] = kh
        vo_ref[0, h] = vh
        s = _dot(q[:, sl], kh, NT)
        p = jnp.exp(s - jnp.max(s, axis=-1, keepdims=True))
        y_ref[:, sl] = _dot(p, vh) / jnp.sum(p, axis=-1, keepdims=True)


def _ctx_attn_call(z, *, nb, t):
    zspec = lambda cb: pl.BlockSpec((t, LANE), lambda b, hp: (b, cb + hp))
    kv_spec = pl.BlockSpec((1, 2, t, N_HD), lambda b, hp: (b, hp, 0, 0))
    kv_shape = jax.ShapeDtypeStruct((nb, N_HEADS, t, N_HD), F32)
    return pl.pallas_call(
        _ctx_attn_body,
        out_shape=[jax.ShapeDtypeStruct((nb * t, HW), F32), kv_shape, kv_shape],
        grid=(nb, N_HEADS // 2),
        in_specs=[zspec(ZB_NQ), zspec(ZB_NK), zspec(ZB_NV)],
        out_specs=[pl.BlockSpec((t, LANE), lambda b, hp: (b, hp)), kv_spec, kv_spec],
        compiler_params=_params(("parallel", "parallel")),
        name="context_attention",
    )(z, z, z)


def _natten_body(q_ref, k_ref, v_ref, kc_ref, vc_ref, tab_ref, y_ref, *, rows_per_step, rows_n):
    rb = pl.program_id(2)
    n_loc = WIN_ROWS * GRID_W

    def row(i, carry):
        r = rb * rows_per_step + i
        rstart = jnp.clip(r - WIN_ROWS // 2, 0, rows_n - WIN_ROWS)
        var = r - rstart
        qrows = pl.ds(pl.multiple_of(i * GRID_W, GRID_W), GRID_W)
        krows = pl.ds(pl.multiple_of(rstart * GRID_W, GRID_W), n_loc)
        for h in range(2):
            sl = slice(N_HD * h, N_HD * (h + 1))
            q = q_ref[qrows, sl] * (N_HD ** -0.5)
            s_loc = _dot(q, k_ref[krows, sl], NT) + tab_ref[h, var]
            s_ctx = _dot(q, kc_ref[0, h], NT)
            m = jnp.maximum(jnp.max(s_loc, axis=-1, keepdims=True), jnp.max(s_ctx, axis=-1, keepdims=True))
            p_loc = jnp.exp(s_loc - m)
            p_ctx = jnp.exp(s_ctx - m)
            den = jnp.sum(p_loc, axis=-1, keepdims=True) + jnp.sum(p_ctx, axis=-1, keepdims=True)
            y_ref[qrows, sl] = (_dot(p_loc, v_ref[krows, sl]) + _dot(p_ctx, vc_ref[0, h])) / den
        return carry

    lax.fori_loop(0, rows_per_step, row, 0)


def _natten_call(z, k_ctx, v_ctx, tab, *, row0, nb, t, rows_per_step=8):
    rows_n = t // GRID_W
    tq = rows_per_step * GRID_W
    qblk0 = row0 // tq
    sblk0 = row0 // t
    n_rb = rows_n // rows_per_step
    past = k_ctx.shape[2]
    kv_spec = lambda cb: pl.BlockSpec((t, LANE), lambda b, hp, rb: (sblk0 + b, cb + hp))
    cache_spec = pl.BlockSpec((1, 2, past, N_HD), lambda b, hp, rb: (b, hp, 0, 0))
    return pl.pallas_call(
        functools.partial(_natten_body, rows_per_step=rows_per_step, rows_n=rows_n),
        out_shape=jax.ShapeDtypeStruct((nb * t, HW), F32),
        grid=(nb, N_HEADS // 2, n_rb),
        in_specs=[pl.BlockSpec((tq, LANE), lambda b, hp, rb: (qblk0 + b * n_rb + rb, ZB_NQ + hp)),
                  kv_spec(ZB_NK), kv_spec(ZB_NV), cache_spec, cache_spec,
                  pl.BlockSpec((2, WIN_ROWS, GRID_W, WIN_ROWS * GRID_W), lambda b, hp, rb: (hp, 0, 0, 0))],
        out_specs=pl.BlockSpec((tq, LANE), lambda b, hp, rb: (b * n_rb + rb, hp)),
        compiler_params=_params(("parallel", "parallel", "arbitrary")),
        name="neighbourhood_attention",
    )(z, z, z, k_ctx, v_ctx, tab)


def _natten_table(rpb):
    var = jnp.arange(WIN_ROWS)[:, None, None, None]
    kr = jnp.arange(WIN_ROWS)[None, None, :, None]
    cq = jnp.arange(GRID_W)[None, :, None, None]
    ck = jnp.arange(GRID_W)[None, None, None, :]
    dr = kr - var + (WIN_ROWS - 1)
    dc = jnp.clip(ck - cq, -(WIN_COLS - 1), WIN_COLS - 1) + (WIN_COLS - 1)
    cstart = jnp.clip(cq - WIN_COLS // 2, 0, GRID_W - WIN_COLS)
    in_band = (ck >= cstart) & (ck < cstart + WIN_COLS)
    tab = jnp.where(in_band[None], rpb[:, dr, dc], NEG)
    return tab.reshape(rpb.shape[0], WIN_ROWS, GRID_W, WIN_ROWS * GRID_W)


def _group_norm(y, avg_bf, eps):
    mu = _dot_lhs2(y, avg_bf)
    yc = y - mu
    var = _dot_lhs2(yc * yc, avg_bf)
    return yc * lax.rsqrt(var + eps)


def _merge_body(*refs, with_router):
    refs = list(refs)
    (x_ref, mod_ref, yr_ref, bv_ref, hm_ref, yn_ref, gl_ref, mo_ref, g0_ref, g1_ref, g2_ref,
     avg64_ref, avg128_ref, rg2_ref, lnw_ref, lnb_ref, mlw_ref, wor_ref, wom_ref, won_ref, wout_ref,
     gpost_ref, gpre_ref) = [refs.pop(0) for _ in range(23)]
    wr_ref, br_ref = (refs.pop(0), refs.pop(0)) if with_router else (None, None)
    x1_ref, hin_ref = refs.pop(0), refs.pop(0)
    lg_ref = refs.pop(0) if with_router else None
    m = mod_ref[0]
    yr = yr_ref[0] + yr_ref[1]
    yn_r = _group_norm(yr, avg64_ref[...], RWKV_GN_EPS) * lnw_ref[...] + lnb_ref[...]
    g = _dot(jax.nn.sigmoid(gl_ref[...]), rg2_ref[...])
    out_r = (yn_r + bv_ref[0] + bv_ref[1]) * g
    hn = _group_norm(hm_ref[0] + hm_ref[1], avg128_ref[...], NORM_EPS) * mlw_ref[...]
    out_m = jax.nn.sigmoid(mo_ref[...]) * hn
    merged = (jax.nn.sigmoid(g0_ref[...]) * _dot(out_r, wor_ref[...])
              + jax.nn.sigmoid(g1_ref[...]) * _dot(out_m, wom_ref[...])
              + jax.nn.sigmoid(g2_ref[...]) * _dot(yn_ref[...], won_ref[...]))
    o = _dot(merged, wout_ref[...])
    x1 = x_ref[...] + m[2:3] * _rms(o, gpost_ref[...])
    x1_ref[...] = x1
    hin = _rms(x1, gpre_ref[...]) * (1.0 + m[4:5]) + m[3:4]
    hin_ref[...] = hin.astype(BF16)
    if with_router:
        lg_ref[...] = _dot3(hin, wr_ref[...]) + br_ref[...]


def _merge_call(x, mod3, yr, bv, hm, yn, z, W, n_ctx, lat_t, router, tm=256):
    ntok = x.shape[0]
    midx = _mod_index(tm, n_ctx, lat_t)
    with_router = router is not None
    row = lambda w: pl.BlockSpec((tm, w), lambda i: (i, 0))
    dirs = pl.BlockSpec((2, tm, HW), lambda i: (0, i, 0))
    zspec = lambda w, cb: pl.BlockSpec((tm, w), lambda i: (i, cb))
    full = lambda a: pl.BlockSpec(a.shape, lambda i: (0,) * a.ndim)
    consts = [W['avg64'], W['avg128'], W['rg2'], W['lnw'], W['lnb'], W['mlw'], W['wor'], W['wom'], W['won'],
              W['wout'], W['gpost'], W['gpre']]
    in_specs = [row(D), pl.BlockSpec((1, 8, D), lambda i: (midx(i), 0, 0)), dirs, dirs, dirs, row(HW),
                zspec(LANE, ZB_GL), zspec(HW, ZB_MO // 4),
                zspec(D, ZB_GATE // 8), zspec(D, ZB_GATE // 8 + 1), zspec(D, ZB_GATE // 8 + 2)]
    in_specs += [full(a) for a in consts]
    args = [x, mod3, yr, bv, hm, yn, z, z, z, z, z] + consts
    out_shape = [jax.ShapeDtypeStruct((ntok, D), F32), jax.ShapeDtypeStruct((ntok, D), BF16)]
    out_specs = [row(D), row(D)]
    if with_router:
        in_specs += [full(router[0]), full(router[1])]
        args += list(router)
        out_shape.append(jax.ShapeDtypeStruct((ntok, LANE), F32))
        out_specs.append(row(LANE))
    return pl.pallas_call(
        functools.partial(_merge_body, with_router=with_router),
        out_shape=out_shape,
        grid=(ntok // tm,),
        in_specs=in_specs,
        out_specs=out_specs,
        compiler_params=_params(("parallel",)),
        name="branch_merge",
    )(*args)


def _top2_gate(lg):
    lane = lax.broadcasted_iota(jnp.int32, lg.shape, 1)
    m1 = jnp.max(lg, axis=1, keepdims=True)
    i1 = jnp.min(jnp.where(lg == m1, lane, LANE), axis=1, keepdims=True)
    first = lane == i1
    lg2 = jnp.where(first, -jnp.inf, lg)
    m2 = jnp.max(lg2, axis=1, keepdims=True)
    i2 = jnp.min(jnp.where(lg2 == m2, lane, LANE), axis=1, keepdims=True)
    e2 = jnp.exp(m2 - m1)
    den = 1.0 + e2
    return jnp.where(first, 1.0 / den, 0.0) + jnp.where(lane == i2, e2 / den, 0.0)


def _ffn_body(*refs, moe):
    refs = list(refs)
    h_ref, x1_ref, mod_ref, gpost_ref = [refs.pop(0) for _ in range(4)]
    lg_ref = refs.pop(0) if moe else None
    wg_ref, wu_ref, wd_ref, o_ref, acc_s = [refs.pop(0) for _ in range(5)]
    gate_s = refs.pop(0) if moe else None
    e = pl.program_id(1)
    j = pl.program_id(2)
    n_e = pl.num_programs(1)
    n_j = pl.num_programs(2)

    @pl.when((e == 0) & (j == 0))
    def _():
        acc_s[...] = jnp.zeros_like(acc_s)
        if moe:
            gate_s[...] = _top2_gate(lg_ref[...])

    hin = h_ref[...]
    hg = jnp.dot(hin, wg_ref[0], preferred_element_type=F32)
    hu = jnp.dot(hin, wu_ref[0], preferred_element_type=F32)
    part = _dot(hg * jax.nn.sigmoid(hg) * hu, wd_ref[0])
    if moe:
        lane = lax.broadcasted_iota(jnp.int32, gate_s.shape, 1)
        part = part * jnp.sum(jnp.where(lane == e, gate_s[...], 0.0), axis=1, keepdims=True)
    acc_s[...] += part

    @pl.when((e == n_e - 1) & (j == n_j - 1))
    def _():
        m = mod_ref[0]
        o_ref[...] = x1_ref[...] + m[5:6] * _rms(acc_s[...], gpost_ref[...])


def _ffn_call(hin, x1, mod3, gpost, logits, wg, wu, wd, n_ctx, lat_t, tm=512, tf=896):
    ntok = hin.shape[0]
    n_e, _, dff = wg.shape
    moe = logits is not None
    midx = _mod_index(tm, n_ctx, lat_t)
    in_specs = [pl.BlockSpec((tm, D), lambda i, e, j: (i, 0)),
                pl.BlockSpec((tm, D), lambda i, e, j: (i, 0)),
                pl.BlockSpec((1, 8, D), lambda i, e, j: (midx(i), 0, 0)),
                pl.BlockSpec((1, D), lambda i, e, j: (0, 0))]
    args = [hin, x1, mod3, gpost]
    scratch = [pltpu.VMEM((tm, D), F32)]
    if moe:
        in_specs.append(pl.BlockSpec((tm, LANE), lambda i, e, j: (i, 0)))
        args.append(logits)
        scratch.append(pltpu.VMEM((tm, LANE), F32))
    in_specs += [pl.BlockSpec((1, D, tf), lambda i, e, j: (e, 0, j)),
                 pl.BlockSpec((1, D, tf), lambda i, e, j: (e, 0, j)),
                 pl.BlockSpec((1, tf, D), lambda i, e, j: (e, j, 0))]
    args += [wg, wu, wd]
    return pl.pallas_call(
        functools.partial(_ffn_body, moe=moe),
        out_shape=jax.ShapeDtypeStruct((ntok, D), F32),
        grid=(ntok // tm, n_e, dff // tf),
        in_specs=in_specs,
        out_specs=pl.BlockSpec((tm, D), lambda i, e, j: (i, 0)),
        scratch_shapes=scratch,
        compiler_params=_params(("parallel", "arbitrary", "arbitrary")),
        name="moe_swiglu" if moe else "dense_swiglu",
    )(*args)


def _rope_tables(t):
    i = jnp.arange(M_HD)
    half, pair, f = i // 64, (i % 64) // 32, i % 32
    tt = jnp.arange(t)
    pos = jnp.stack([tt // GRID_W, tt % GRID_W], axis=-1).astype(F32)
    inv = ROPE_BASE ** (-f.astype(F32) / 32)
    ang = pos[:, half] * inv[None, :]
    sin = jnp.sin(ang)
    return (jnp.cos(ang), jnp.where(pair == 0, -sin, 0.0)[...], jnp.where(pair == 1, sin, 0.0))


def _pad_in_proj(w):
    off = {}
    o = 0
    for name, width in (('r', 512), ('k', 512), ('v', 512), ('wl', 128), ('al', 128), ('gl', 128),
                        ('mq', 512), ('mk', 512), ('mv', 512), ('mo', 512), ('mif', 16),
                        ('nq', 512), ('nk', 512), ('nv', 512), ('gate', 3072)):
        off[name] = (o, o + width)
        o += width
    col = lambda n: w[:, off[n][0]:off[n][1]]
    parts = [col(n) for n in ('r', 'k', 'v', 'mq', 'mk', 'mv', 'mo', 'nq', 'nk', 'nv', 'gate', 'wl', 'al', 'gl', 'mif')]
    parts.append(jnp.zeros((w.shape[0], LANE - 16), w.dtype))
    return jnp.concatenate(parts, axis=1).astype(BF16)


def _block_avg(width, group):
    i = jnp.arange(width) // group
    return jnp.where(i[:, None] == i[None, :], 1.0 / group, 0.0).astype(BF16)


def _zero_pad_rows(w2):
    z = jnp.zeros_like(w2[0])
    return jnp.stack([jnp.concatenate([w2[0], z], axis=0), jnp.concatenate([z, w2[1]], axis=0)])


def kernel(x_prompt, x_sample, state_rwkv, state_mlstm_C, state_mlstm_n, state_mlstm_m, cache_nat_k, cache_nat_v,
           c, c_ctx, w_mod, b_mod, g_pre_mix, g_post_mix, g_pre_ffn, g_post_ffn, w_in,
           rw_w0, rw_w2, rw_a0, rw_a2, rw_g2, rw_k_k, rw_k_a, rw_r_k, rw_ln_w, rw_ln_b,
           ml_b_i, ml_b_f, ml_norm_w, nat_rpb, w_o_rwkv, w_o_mlstm, w_o_nat, w_out,
           ff_w_gate, ff_w_up, ff_w_down, moe_w_router, moe_b_router, moe_w_gate, moe_w_up, moe_w_down):
    cb, ct = x_prompt.shape[:2]
    lb, lt = x_sample.shape[:2]
    n_ctx, n_lat = cb * ct, lb * lt
    x = jnp.concatenate([x_prompt.reshape(n_ctx, D), x_sample.reshape(n_lat, D)], axis=0)
    cvec = jnp.zeros((8, D), F32).at[0].set(c_ctx).at[1:1 + lb].set(c)
    rope = _rope_tables(lt)
    avg64, avg128 = _block_avg(HW, R_HD), _block_avg(HW, M_HD)
    lat_tb = min(lt, 512)
    states = []
    for l in range(DEPTH):
        mod = _mod_call(cvec, w_mod[l], b_mod[l])
        mod3 = jnp.pad(mod[:1 + lb].reshape(1 + lb, 6, D), ((0, 0), (0, 2), (0, 0)))
        z = _proj_call(x, mod3, g_pre_mix[l], _pad_in_proj(w_in[l]), n_ctx, lt)

        rp = dict(w0=rw_w0[l].reshape(2, 1, HW), w2=_zero_pad_rows(rw_w2[l]),
                  a0=rw_a0[l].reshape(2, 1, HW), a2=_zero_pad_rows(rw_a2[l]),
                  kk=rw_k_k[l].reshape(1, HW), ka=rw_k_a[l].reshape(1, HW), rk=rw_r_k[l].reshape(1, HW))
        yr_c, bv_c, s_rw = _rwkv_call(z, rp, None, row0=0, nb=cb, t=ct, tb=ct, has_out=True)
        yr_l, bv_l = _rwkv_call(z, rp, state_rwkv[:, l], row0=n_ctx, nb=lb, t=lt, tb=lat_tb, has_out=False)

        gt = z[:, ZB_MIF * LANE:ZB_MIF * LANE + 16].reshape(-1, CHUNK, 16).transpose(0, 2, 1)
        hm_c, c_m, n_m, m_m = _mlstm_call(z, gt, None, ml_b_i[l], ml_b_f[l], None,
                                          row0=0, nb=cb, t=ct, tb=ct, has_out=True)
        init = (state_mlstm_C[:, l], state_mlstm_n[:, l][:, :, :, None, :], state_mlstm_m[:, l])
        hm_l, = _mlstm_call(z, gt, rope, ml_b_i[l], ml_b_f[l], init,
                            row0=n_ctx, nb=lb, t=lt, tb=lat_tb, has_out=False)

        yn_c, k_c, v_c = _ctx_attn_call(z, nb=cb, t=ct)
        yn_l = _natten_call(z, cache_nat_k[:, l], cache_nat_v[:, l], _natten_table(nat_rpb[l]),
                            row0=n_ctx, nb=lb, t=lt)
        states.append((s_rw, c_m, n_m[:, :, :, 0, :], m_m[:, :, :, 0, 0], k_c, v_c))

        cat = lambda a, b_, ax: jnp.concatenate([a, b_], axis=ax)
        mw = dict(avg64=avg64, avg128=avg128, rg2=rw_g2[l].astype(BF16), lnw=rw_ln_w[l].reshape(1, HW),
                  lnb=rw_ln_b[l].reshape(1, HW), mlw=ml_norm_w[l].reshape(1, HW),
                  wor=w_o_rwkv[l].astype(BF16), wom=w_o_mlstm[l].astype(BF16), won=w_o_nat[l].astype(BF16),
                  wout=w_out[l].astype(BF16), gpost=g_post_mix[l].reshape(1, D), gpre=g_pre_ffn[l].reshape(1, D))
        j = l // 2
        if l % 2 == 0:
            router = None
        else:
            wr = jnp.pad(moe_w_router[j], ((0, 0), (0, LANE - N_EXPERTS)))
            br = jnp.pad(moe_b_router[j], (0, LANE - N_EXPERTS), constant_values=NEG).reshape(1, LANE)
            router = (wr, br)
        merged = _merge_call(x, mod3, cat(yr_c, yr_l, 1), cat(bv_c, bv_l, 1), cat(hm_c, hm_l, 1),
                             cat(yn_c, yn_l, 0), z, mw, n_ctx, lt, router)
        gpost = g_post_ffn[l].reshape(1, D)
        if l % 2 == 0:
            x = _ffn_call(merged[1], merged[0], mod3, gpost, None, ff_w_gate[j:j + 1].astype(BF16),
                          ff_w_up[j:j + 1].astype(BF16), ff_w_down[j:j + 1].astype(BF16), n_ctx, lt, tf=1408)
        else:
            x = _ffn_call(merged[1], merged[0], mod3, gpost, merged[2], moe_w_gate[j].astype(BF16),
                          moe_w_up[j].astype(BF16), moe_w_down[j].astype(BF16), n_ctx, lt, tf=896)

    stack = lambda k: jnp.stack([s[k] for s in states], axis=1)
    return (x[:n_ctx].reshape(cb, ct, D), x[n_ctx:].reshape(lb, lt, D),
            stack(0), stack(1), stack(2), stack(3), stack(4), stack(5))
```

```python
import functools
import math

import jax
import jax.numpy as jnp
from jax import lax
from jax.experimental import pallas as pl
from jax.experimental.pallas import tpu as pltpu

F32 = jnp.float32
BF16 = jnp.bfloat16

D = 1024
N_CTX_B, N_CTX_T = 32, 256
N_LAT_B, N_LAT_T = 2, 4096
DEPTH = 2
GRID_W = 64
R_HEADS, R_HD = 8, 64
M_HEADS, M_HD = 4, 128
N_HEADS, N_HD = 8, 64
HW = 512
WIN_ROWS, WIN_COLS = 8, 16
N_EXPERTS = 8
RWKV_GN_EPS = 64e-5
NORM_EPS = 1e-6
ROPE_BASE = 10000.0
CHUNK = 64
NEG = -1e30
LANE = 128
VMEM_LIMIT = 56 * 1024 * 1024

ZB_R, ZB_K, ZB_V = 0, 4, 8
ZB_MQ, ZB_MK, ZB_MV, ZB_MO = 12, 16, 20, 24
ZB_NQ, ZB_NK, ZB_NV = 28, 32, 36
ZB_GATE = 40
ZB_WL, ZB_AL, ZB_GL, ZB_MIF = 64, 65, 66, 67
P_PAD = 68 * LANE

NN = (((1,), (0,)), ((), ()))
NT = (((1,), (1,)), ((), ()))
TN = (((0,), (0,)), ((), ()))


def _dot(a, b, dims=NN):
    return lax.dot_general(a.astype(BF16), b.astype(BF16), dims, preferred_element_type=F32)


def _split(x):
    hi = x.astype(BF16)
    return hi, (x - hi.astype(F32)).astype(BF16)


def _dot3(a, b, dims=NN):
    ah, al = _split(a)
    bh, bl = _split(b)
    d = lambda u, w: lax.dot_general(u, w, dims, preferred_element_type=F32)
    return d(ah, bh) + (d(ah, bl) + d(al, bh))


def _dot_lhs2(a, b_exact, dims=NN):
    ah, al = _split(a)
    d = lambda u: lax.dot_general(u, b_exact, dims, preferred_element_type=F32)
    return d(ah) + d(al)


def _dot_rhs2(a_exact, b, dims=NN):
    bh, bl = _split(b)
    d = lambda w: lax.dot_general(a_exact, w, dims, preferred_element_type=F32)
    return d(bh) + d(bl)


def _log_sigmoid(x):
    return jnp.minimum(x, 0.0) - jnp.log1p(jnp.exp(-jnp.abs(x)))


def _rms(x, g):
    return x * lax.rsqrt(jnp.mean(x * x, axis=-1, keepdims=True) + NORM_EPS) * g


def _params(sem):
    return pltpu.CompilerParams(dimension_semantics=sem, vmem_limit_bytes=VMEM_LIMIT)


def _mod_body(c_ref, w_ref, b_ref, o_ref):
    c = c_ref[...]
    o_ref[...] = _dot(c * jax.nn.sigmoid(c), w_ref[...]) + b_ref[...]


def _mod_call(cvec, w_mod, b_mod):
    tn = 1536
    return pl.pallas_call(
        _mod_body,
        out_shape=jax.ShapeDtypeStruct((8, 6 * D), F32),
        grid=(6 * D // tn,),
        in_specs=[pl.BlockSpec((8, D), lambda j: (0, 0)),
                  pl.BlockSpec((D, tn), lambda j: (0, j)),
                  pl.BlockSpec((1, tn), lambda j: (0, j))],
        out_specs=pl.BlockSpec((8, tn), lambda j: (0, j)),
        compiler_params=_params(("arbitrary",)),
        name="adaln_mod",
    )(cvec, w_mod, b_mod.reshape(1, 6 * D))


def _mod_index(tile_rows, n_ctx, lat_t):
    def f(i):
        start = i * tile_rows
        return jnp.where(start < n_ctx, 0, 1 + (start - n_ctx) // lat_t)
    return f


def _proj_body(x_ref, mod_ref, g_ref, w_ref, z_ref, hin_s):
    @pl.when(pl.program_id(1) == 0)
    def _():
        m = mod_ref[0]
        hin_s[...] = (_rms(x_ref[...], g_ref[...]) * (1.0 + m[1:2]) + m[0:1]).astype(BF16)
    z_ref[...] = jnp.dot(hin_s[...], w_ref[...], preferred_element_type=F32)


def _proj_call(x, mod3, g, w_bf, n_ctx, lat_t, tm=512, tn=2176):
    ntok = x.shape[0]
    midx = _mod_index(tm, n_ctx, lat_t)
    return pl.pallas_call(
        _proj_body,
        out_shape=jax.ShapeDtypeStruct((ntok, P_PAD), F32),
        grid=(ntok // tm, P_PAD // tn),
        in_specs=[pl.BlockSpec((tm, D), lambda i, j: (i, 0)),
                  pl.BlockSpec((1, 8, D), lambda i, j: (midx(i), 0, 0)),
                  pl.BlockSpec((1, D), lambda i, j: (0, 0)),
                  pl.BlockSpec((D, tn), lambda i, j: (0, j))],
        out_specs=pl.BlockSpec((tm, tn), lambda i, j: (i, j)),
        scratch_shapes=[pltpu.VMEM((tm, D), BF16)],
        compiler_params=_params(("parallel", "arbitrary")),
        name="in_proj",
    )(x, mod3, g.reshape(1, D), w_bf)


RW_TILE = 2 * CHUNK
RW_BLOCK = 512


def _rwkv_body(*refs, tb_rows, nseq, has_init, has_out):
    (r_ref, k_ref, v_ref, wl_ref, al_ref, w0_ref, w2_ref, a0_ref, a2_ref,
     kkw_ref, kaw_ref, rkw_ref, hsum_ref) = refs[:13]
    rest = list(refs[13:])
    s0_ref = rest.pop(0) if has_init else None
    y_ref, bv_ref = rest.pop(0), rest.pop(0)
    sout_ref = rest.pop(0) if has_out else None
    r2_s, m_s, g_s, st_s = rest
    C = CHUNK
    TL = RW_TILE
    nchunk = tb_rows // nseq // C
    d = pl.program_id(2)
    tb = pl.program_id(3)
    n_tb = pl.num_programs(3)

    ti = lax.broadcasted_iota(jnp.int32, (TL, TL), 0)
    si = lax.broadcasted_iota(jnp.int32, (TL, TL), 1)
    head_blk = (ti >> 6) == (si >> 6)

    @pl.when(tb == 0)
    def _():
        for q in range(nseq):
            if has_init:
                s0 = s0_ref[q, 0]
                two = jnp.concatenate([jnp.concatenate([s0[0], s0[0]], axis=1),
                                       jnp.concatenate([s0[1], s0[1]], axis=1)], axis=0)
                st_s[q] = jnp.where(head_blk, two, 0.0)
            else:
                st_s[q] = jnp.zeros((LANE, LANE), F32)

    hsum = hsum_ref[...]
    r = r_ref[...]
    kf = k_ref[...]
    v = v_ref[...]
    w_pre = w0_ref[0] + _dot(jnp.tanh(wl_ref[...]), w2_ref[0])
    lw = -math.exp(-0.5) * jax.nn.sigmoid(w_pre)
    a = jax.nn.sigmoid(a0_ref[0] + _dot(al_ref[...], a2_ref[0]))
    kd = kf * (1.0 + (a - 1.0) * kaw_ref[...])
    kkf = kf * kkw_ref[...]
    kn = kkf * lax.rsqrt(_dot_lhs2(kkf * kkf, hsum) + 1e-12)
    bd = kn * a
    bv_ref[0] = _dot_lhs2(r * kd * rkw_ref[...], hsum) * v

    dlt = (ti - si) * (1 - 2 * d)
    strict = head_blk & (dlt > 0)
    incl = head_blk & (dlt >= 0)
    cum_ones = jnp.concatenate([jnp.where(incl, 1.0, 0.0), jnp.where(head_blk, 1.0, 0.0)], axis=0).astype(BF16)
    is_diag = ti == si
    eye = jnp.where(is_diag, 1.0, 0.0)
    pair = (ti >> 1) == (si >> 1)
    sibling = [((ti >> lg) ^ (si >> lg)) == 1 for lg in range(1, 6)]
    head0 = si < R_HD

    n_tiles = tb_rows // TL
    tiles = []
    for tile in range(n_tiles):
        rows = slice(TL * tile, TL * (tile + 1))
        cs = _dot_rhs2(cum_ones, lw[rows])
        cum, tot = cs[:TL], cs[TL:]
        e_neg = jnp.exp(-cum)
        e_hat = jnp.exp(tot - cum)
        rt = r[rows] * jnp.exp(cum)
        at = -kn[rows] * jnp.exp(cum - lw[rows])
        tiles.append(dict(rows=rows, tot=tot, rt=rt, at=at, bh=bd[rows] * e_hat, kh=kd[rows] * e_hat,
                          rhs=jnp.concatenate([bd[rows] * e_neg, kd[rows] * e_neg], axis=0).astype(BF16),
                          at_bf=at.astype(BF16), v_bf=v[rows].astype(BF16)))
    chains = [(tl, h) for tl in tiles for h in range(2)]
    a_all = []
    for tl, h in chains:
        hm = head0 if h == 0 else ~head0
        lhs = jnp.concatenate([jnp.where(hm, tl['at'], 0.0), jnp.where(hm, tl['rt'], 0.0)], axis=0)
        a_all.append(_dot(lhs, tl['rhs'], NT))
    n_mat = [jnp.where(strict, a[:TL, :TL], 0.0) for a in a_all]
    a_ak = [jnp.where(strict, a[:TL, TL:], 0.0).astype(BF16) for a in a_all]
    a_rb = [jnp.where(incl, a[TL:, :TL], 0.0).astype(BF16) for a in a_all]
    a_rk = [jnp.where(incl, a[TL:, TL:], 0.0).astype(BF16) for a in a_all]
    t_inv = [eye + jnp.where(pair, n, 0.0) for n in n_mat]
    for sib in sibling:
        x_mat = [_dot(jnp.where(sib, n, 0.0), t) for n, t in zip(n_mat, t_inv)]
        t_inv = [t + _dot(t, x) for t, x in zip(t_inv, x_mat)]
    p_h = [_dot(t, tl['at_bf']) for t, (tl, _) in zip(t_inv, chains)]
    av = [_dot(a, tl['v_bf']) for a, (tl, _) in zip(a_ak, chains)]
    q_h = [_dot(t, x) for t, x in zip(t_inv, av)]
    r2_h = [_dot(a, p) for a, p in zip(a_rb, p_h)]
    y0_h = [_dot(a, q) + _dot(k_, tl['v_bf']) for a, q, k_, (tl, _) in zip(a_rb, q_h, a_rk, chains)]
    for i, tl in enumerate(tiles):
        rows = tl['rows']
        pick = lambda u: jnp.where(head0, u[2 * i], u[2 * i + 1])
        p_m, q_m = pick(p_h), pick(q_h)
        r2_s[rows, :] = tl['rt'] + pick(r2_h)
        y_ref[0, rows, :] = pick(y0_h)
        for c in range(TL // C):
            cr = slice(C * c, C * (c + 1))
            decay = jnp.where(is_diag, jnp.exp(tl['tot'][C * c:C * c + 1]), 0.0)
            m_s[i * (TL // C) + c] = decay + jnp.where(head_blk, _dot(p_m[cr], tl['bh'][cr], TN), 0.0)
            g_s[i * (TL // C) + c] = jnp.where(
                head_blk, _dot(q_m[cr], tl['bh'][cr], TN) + _dot(v[rows][cr], tl['kh'][cr], TN), 0.0)

    def step(i, states):
        ci = jnp.where(d == 0, i, nchunk - 1 - i)
        new = []
        for q, s in enumerate(states):
            cq = q * nchunk + ci
            rows = pl.ds(pl.multiple_of(cq * C, C), C)
            y_ref[0, rows, :] = y_ref[0, rows, :] + _dot(r2_s[rows, :], s, NT)
            new.append(_dot3(s, m_s[cq]) + g_s[cq])
        return tuple(new)

    s_fin = lax.fori_loop(0, nchunk, step, tuple(st_s[q] for q in range(nseq)))
    for q in range(nseq):
        st_s[q] = s_fin[q]

    if has_out:
        @pl.when(tb == n_tb - 1)
        def _():
            for q in range(nseq):
                sout_ref[q, 0, 0] = s_fin[q][:R_HD, :R_HD]
                sout_ref[q, 0, 1] = s_fin[q][R_HD:, R_HD:]


def _rwkv_call(z, P, s0, *, row0, nb, t, tb, has_out):
    nseq = max(tb // t, 1)
    n_tb = max(t // tb, 1)
    blk0 = row0 // tb
    has_init = s0 is not None

    def rowblk(b, d, i):
        return blk0 + b * n_tb + jnp.where(d == 0, i, n_tb - 1 - i)

    def zspec(cb):
        return pl.BlockSpec((tb, LANE), lambda b, hp, d, i: (rowblk(b, d, i), cb(hp)))

    vec = lambda: pl.BlockSpec((1, LANE), lambda b, hp, d, i: (0, hp))
    dvec = lambda: pl.BlockSpec((1, 1, LANE), lambda b, hp, d, i: (d, 0, hp))
    dmat = lambda: pl.BlockSpec((1, LANE, LANE), lambda b, hp, d, i: (d, 0, hp))
    st_spec = pl.BlockSpec((nseq, 1, 2, R_HD, R_HD), lambda b, hp, d, i: (b, d, hp, 0, 0))
    in_specs = [zspec(lambda hp: ZB_R + hp), zspec(lambda hp: ZB_K + hp), zspec(lambda hp: ZB_V + hp),
                zspec(lambda hp: ZB_WL), zspec(lambda hp: ZB_AL),
                dvec(), dmat(), dvec(), dmat(), vec(), vec(), vec(),
                pl.BlockSpec((LANE, LANE), lambda b, hp, d, i: (0, 0))]
    args = [z, z, z, z, z, P['w0'], P['w2'], P['a0'], P['a2'], P['kk'], P['ka'], P['rk'],
            _block_avg(LANE, R_HD) * R_HD]
    if has_init:
        in_specs.append(st_spec)
        args.append(s0)
    yspec = pl.BlockSpec((1, tb, LANE), lambda b, hp, d, i: (d, rowblk(b, d, i) - blk0, hp))
    out_shape = [jax.ShapeDtypeStruct((2, nb * t, HW), F32)] * 2
    out_specs = [yspec, yspec]
    if has_out:
        out_shape.append(jax.ShapeDtypeStruct((nb, 2, R_HEADS, R_HD, R_HD), F32))
        out_specs.append(st_spec)
    return pl.pallas_call(
        functools.partial(_rwkv_body, tb_rows=tb, nseq=nseq, has_init=has_init, has_out=has_out),
        out_shape=out_shape,
        grid=(nb // nseq, R_HEADS // 2, 2, n_tb),
        in_specs=in_specs,
        out_specs=out_specs,
        scratch_shapes=[pltpu.VMEM((tb, LANE), F32), pltpu.VMEM((tb // CHUNK, LANE, LANE), F32),
                        pltpu.VMEM((tb // CHUNK, LANE, LANE), F32), pltpu.VMEM((nseq, LANE, LANE), F32)],
        compiler_params=_params(("parallel", "parallel", "arbitrary", "arbitrary")),
        name="rwkv7_chunked",
    )(*args)


def _mlstm_body(*refs, nchunk, use_rope, has_init, has_out):
    refs = list(refs)
    q_ref, k_ref, v_ref, g_ref, gt_ref = [refs.pop(0) for _ in range(5)]
    cos_ref, sa_ref, sb_ref = [refs.pop(0) for _ in range(3)] if use_rope else (None,) * 3
    bi_ref, bf_ref = refs.pop(0), refs.pop(0)
    c0_ref, n0_ref, m0_ref = [refs.pop(0) for _ in range(3)] if has_init else (None,) * 3
    h_ref = refs.pop(0)
    cout_ref, nout_ref, mout_ref = [refs.pop(0) for _ in range(3)] if has_out else (None,) * 3
    ic_s, fc_s, c_s, n_s, m_s = refs
    L = CHUNK
    b = pl.program_id(0)
    h = pl.program_id(1)
    d = pl.program_id(2)
    tb = pl.program_id(3)
    n_tb = pl.num_programs(3)
    bi = bi_ref[d, h]
    bf = bf_ref[d, h]
    ji = d * (2 * M_HEADS) + h
    jf = ji + M_HEADS

    @pl.when(tb == 0)
    def _():
        if has_init:
            c_s[...] = c0_ref[0, 0, 0]
            n_s[...] = n0_ref[0, 0, 0]
            m_s[...] = jnp.full(m_s.shape, m0_ref[b, d, h], F32)
        else:
            c_s[...] = jnp.zeros_like(c_s)
            n_s[...] = jnp.zeros_like(n_s)
            m_s[...] = jnp.zeros_like(m_s)

    g = g_ref[...]
    lane = lax.broadcasted_iota(jnp.int32, g.shape, 1)
    ic_s[...] = jnp.sum(jnp.where(lane == ji, g, 0.0), axis=1, keepdims=True) + bi
    fc_s[...] = _log_sigmoid(jnp.sum(jnp.where(lane == jf, g, 0.0), axis=1, keepdims=True) + bf)

    ti = lax.broadcasted_iota(jnp.int32, (L, L), 0)
    si = lax.broadcasted_iota(jnp.int32, (L, L), 1)
    sgn = 1 - 2 * d
    incl = (ti - si) * sgn >= 0
    incl_t = (si - ti) * sgn >= 0

    def chunk(i, carry):
        ci = jnp.where(d == 0, i, nchunk - 1 - i)
        rows = pl.ds(pl.multiple_of(ci * L, L), L)
        qc = q_ref[rows, :]
        kc = k_ref[rows, :]
        vc = v_ref[rows, :]
        if use_rope:
            cos, sa, sb = cos_ref[rows, :], sa_ref[rows, :], sb_ref[rows, :]
            qc = qc * cos + pltpu.roll(qc, 96, 1) * sa + pltpu.roll(qc, 32, 1) * sb
            kc = kc * cos + pltpu.roll(kc, 96, 1) * sa + pltpu.roll(kc, 32, 1) * sb
        kc = kc * (M_HD ** -0.5)
        icol = ic_s[rows, :]
        fcol = fc_s[rows, :]
        irow = gt_ref[ci, pl.ds(ji, 1), :] + bi
        frow = _log_sigmoid(gt_ref[ci, pl.ds(jf, 1), :] + bf)
        bsum_col = jnp.sum(jnp.where(incl, frow, 0.0), axis=1, keepdims=True)
        bsum_row = jnp.sum(jnp.where(incl_t, fcol, 0.0), axis=0, keepdims=True)
        m = m_s[:, 0:1]
        dmat = jnp.where(incl, bsum_col - bsum_row + irow, NEG)
        inter = bsum_col + m
        mt = jnp.maximum(inter, jnp.max(dmat, axis=1, keepdims=True))
        sm = _dot(qc, kc, NT) * jnp.exp(dmat - mt)
        iw = jnp.exp(inter - mt)
        c = c_s[...]
        n = n_s[...]
        num = iw * _dot(qc, c) + _dot(sm, vc)
        den = iw * jnp.sum(qc * n, axis=1, keepdims=True) + jnp.sum(sm, axis=1, keepdims=True)
        h_ref[0, rows, :] = num / jnp.maximum(jnp.abs(den), jnp.exp(-mt))
        blast = jnp.sum(frow, axis=1, keepdims=True)
        wlog = blast - bsum_col + icol
        m_new = jnp.maximum(blast + m, jnp.max(wlog, axis=0, keepdims=True))
        dec = jnp.exp(blast + m - m_new)
        kw = kc * jnp.exp(wlog - m_new)
        c_s[...] = dec * c + _dot3(kw, vc, TN)
        n_s[...] = dec * n + jnp.sum(kw, axis=0, keepdims=True)
        m_s[...] = jnp.broadcast_to(m_new, m_s.shape)
        return carry

    lax.fori_loop(0, nchunk, chunk, 0)

    if has_out:
        @pl.when(tb == n_tb - 1)
        def _():
            cout_ref[0, 0, 0] = c_s[...]
            nout_ref[0, 0, 0] = n_s[...]
            mout_ref[0, 0, 0] = m_s[...]


def _mlstm_call(z, gt, rope, b_i, b_f, init, *, row0, nb, t, tb, has_out):
    n_tb = t // tb
    blk0 = row0 // tb
    cpb = tb // CHUNK
    use_rope = rope is not None
    has_init = init is not None

    def rowblk(b, d, i):
        return blk0 + b * n_tb + jnp.where(d == 0, i, n_tb - 1 - i)

    def zspec(cb):
        return pl.BlockSpec((tb, LANE), lambda b, h, d, i: (rowblk(b, d, i), cb(h)))

    smem = pl.BlockSpec(memory_space=pltpu.SMEM)
    in_specs = [zspec(lambda h: ZB_MQ + h), zspec(lambda h: ZB_MK + h), zspec(lambda h: ZB_MV + h),
                zspec(lambda h: ZB_MIF),
                pl.BlockSpec((cpb, 16, CHUNK), lambda b, h, d, i: (rowblk(b, d, i), 0, 0))]
    args = [z, z, z, z, gt]
    if use_rope:
        tspec = pl.BlockSpec((tb, LANE), lambda b, h, d, i: (jnp.where(d == 0, i, n_tb - 1 - i), 0))
        in_specs += [tspec] * 3
        args += list(rope)
    in_specs += [smem, smem]
    args += [b_i, b_f]
    cspec = pl.BlockSpec((1, 1, 1, M_HD, M_HD), lambda b, h, d, i: (b, d, h, 0, 0))
    nspec = pl.BlockSpec((1, 1, 1, 1, M_HD), lambda b, h, d, i: (b, d, h, 0, 0))
    if has_init:
        in_specs += [cspec, nspec, smem]
        args += list(init)
    out_shape = [jax.ShapeDtypeStruct((2, nb * t, HW), F32)]
    out_specs = [pl.BlockSpec((1, tb, LANE), lambda b, h, d, i: (d, rowblk(b, d, i) - blk0, h))]
    if has_out:
        out_shape += [jax.ShapeDtypeStruct((nb, 2, M_HEADS, M_HD, M_HD), F32),
                      jax.ShapeDtypeStruct((nb, 2, M_HEADS, 1, M_HD), F32),
                      jax.ShapeDtypeStruct((nb, 2, M_HEADS, 1, M_HD), F32)]
        out_specs += [cspec, nspec, nspec]
    return pl.pallas_call(
        functools.partial(_mlstm_body, nchunk=cpb, use_rope=use_rope, has_init=has_init, has_out=has_out),
        out_shape=out_shape,
        grid=(nb, M_HEADS, 2, n_tb),
        in_specs=in_specs,
        out_specs=out_specs,
        scratch_shapes=[pltpu.VMEM((tb, 1), F32), pltpu.VMEM((tb, 1), F32),
                        pltpu.VMEM((M_HD, M_HD), F32), pltpu.VMEM((1, M_HD), F32), pltpu.VMEM((1, M_HD), F32)],
        compiler_params=_params(("parallel", "parallel", "arbitrary", "arbitrary")),
        name="mlstm_chunkwise",
    )(*args)


def _ctx_attn_body(q_ref, k_ref, v_ref, y_ref, ko_ref, vo_ref):
    q = q_ref[...] * (N_HD ** -0.5)
    k = k_ref[...]
    v = v_ref[...]
    for h in range(2):
        sl = slice(N_HD * h, N_HD * (h + 1))
        kh, vh = k[:, sl], v[:, sl]
        ko_ref[0, h] = kh
        vo_ref[0, h] = vh
        s = _dot(q[:, sl], kh, NT)
        p = jnp.exp(s - jnp.max(s, axis=-1, keepdims=True))
        y_ref[:, sl] = _dot(p, vh) / jnp.sum(p, axis=-1, keepdims=True)


def _ctx_attn_call(z, *, nb, t):
    zspec = lambda cb: pl.BlockSpec((t, LANE), lambda b, hp: (b, cb + hp))
    kv_spec = pl.BlockSpec((1, 2, t, N_HD), lambda b, hp: (b, hp, 0, 0))
    kv_shape = jax.ShapeDtypeStruct((nb, N_HEADS, t, N_HD), F32)
    return pl.pallas_call(
        _ctx_attn_body,
        out_shape=[jax.ShapeDtypeStruct((nb * t, HW), F32), kv_shape, kv_shape],
        grid=(nb, N_HEADS // 2),
        in_specs=[zspec(ZB_NQ), zspec(ZB_NK), zspec(ZB_NV)],
        out_specs=[pl.BlockSpec((t, LANE), lambda b, hp: (b, hp)), kv_spec, kv_spec],
        compiler_params=_params(("parallel", "parallel")),
        name="context_attention",
    )(z, z, z)


def _natten_body(q_ref, k_ref, v_ref, kc_ref, vc_ref, tab_ref, y_ref, *, rows_per_step, rows_n):
    rb = pl.program_id(2)
    n_loc = WIN_ROWS * GRID_W

    def row(i, carry):
        r = rb * rows_per_step + i
        rstart = jnp.clip(r - WIN_ROWS // 2, 0, rows_n - WIN_ROWS)
        var = r - rstart
        qrows = pl.ds(pl.multiple_of(i * GRID_W, GRID_W), GRID_W)
        krows = pl.ds(pl.multiple_of(rstart * GRID_W, GRID_W), n_loc)
        for h in range(2):
            sl = slice(N_HD * h, N_HD * (h + 1))
            q = q_ref[qrows, sl] * (N_HD ** -0.5)
            s_loc = _dot(q, k_ref[krows, sl], NT) + tab_ref[h, var]
            s_ctx = _dot(q, kc_ref[0, h], NT)
            m = jnp.maximum(jnp.max(s_loc, axis=-1, keepdims=True), jnp.max(s_ctx, axis=-1, keepdims=True))
            p_loc = jnp.exp(s_loc - m)
            p_ctx = jnp.exp(s_ctx - m)
            den = jnp.sum(p_loc, axis=-1, keepdims=True) + jnp.sum(p_ctx, axis=-1, keepdims=True)
            y_ref[qrows, sl] = (_dot(p_loc, v_ref[krows, sl]) + _dot(p_ctx, vc_ref[0, h])) / den
        return carry

    lax.fori_loop(0, rows_per_step, row, 0)


def _natten_call(z, k_ctx, v_ctx, tab, *, row0, nb, t, rows_per_step=8):
    rows_n = t // GRID_W
    tq = rows_per_step * GRID_W
    qblk0 = row0 // tq
    sblk0 = row0 // t
    n_rb = rows_n // rows_per_step
    past = k_ctx.shape[2]
    kv_spec = lambda cb: pl.BlockSpec((t, LANE), lambda b, hp, rb: (sblk0 + b, cb + hp))
    cache_spec = pl.BlockSpec((1, 2, past, N_HD), lambda b, hp, rb: (b, hp, 0, 0))
    return pl.pallas_call(
        functools.partial(_natten_body, rows_per_step=rows_per_step, rows_n=rows_n),
        out_shape=jax.ShapeDtypeStruct((nb * t, HW), F32),
        grid=(nb, N_HEADS // 2, n_rb),
        in_specs=[pl.BlockSpec((tq, LANE), lambda b, hp, rb: (qblk0 + b * n_rb + rb, ZB_NQ + hp)),
                  kv_spec(ZB_NK), kv_spec(ZB_NV), cache_spec, cache_spec,
                  pl.BlockSpec((2, WIN_ROWS, GRID_W, WIN_ROWS * GRID_W), lambda b, hp, rb: (hp, 0, 0, 0))],
        out_specs=pl.BlockSpec((tq, LANE), lambda b, hp, rb: (b * n_rb + rb, hp)),
        compiler_params=_params(("parallel", "parallel", "arbitrary")),
        name="neighbourhood_attention",
    )(z, z, z, k_ctx, v_ctx, tab)


def _natten_table(rpb):
    cq = jnp.arange(GRID_W)[:, None]
    ck = jnp.arange(GRID_W)[None, :]
    dc = jnp.clip(ck - cq, -(WIN_COLS - 1), WIN_COLS - 1) + (WIN_COLS - 1)
    cstart = jnp.clip(cq - WIN_COLS // 2, 0, GRID_W - WIN_COLS)
    in_band = (ck >= cstart) & (ck < cstart + WIN_COLS)
    onehot = ((dc[None] == jnp.arange(2 * WIN_COLS - 1)[:, None, None]) & in_band[None]).astype(F32)
    cols = jnp.einsum('hrd,dqk->hrqk', rpb, onehot, precision=lax.Precision.HIGHEST)
    cols = cols + jnp.where(in_band, 0.0, NEG)
    tab = jnp.stack([cols[:, WIN_ROWS - 1 - var:2 * WIN_ROWS - 1 - var] for var in range(WIN_ROWS)], axis=1)
    return tab.transpose(0, 1, 3, 2, 4).reshape(rpb.shape[0], WIN_ROWS, GRID_W, WIN_ROWS * GRID_W)


def _group_norm(y, avg_bf, eps):
    mu = _dot_lhs2(y, avg_bf)
    yc = y - mu
    var = _dot_lhs2(yc * yc, avg_bf)
    return yc * lax.rsqrt(var + eps)


def _merge_body(*refs, with_router):
    refs = list(refs)
    (x_ref, mod_ref, yr_ref, bv_ref, hm_ref, yn_ref, gl_ref, mo_ref, g0_ref, g1_ref, g2_ref,
     avg64_ref, avg128_ref, rg2_ref, lnw_ref, lnb_ref, mlw_ref, wor_ref, wom_ref, won_ref, wout_ref,
     gpost_ref, gpre_ref) = [refs.pop(0) for _ in range(23)]
    wr_ref, br_ref = (refs.pop(0), refs.pop(0)) if with_router else (None, None)
    x1_ref, hin_ref = refs.pop(0), refs.pop(0)
    lg_ref = refs.pop(0) if with_router else None
    m = mod_ref[0]
    yr = yr_ref[0] + yr_ref[1]
    yn_r = _group_norm(yr, avg64_ref[...], RWKV_GN_EPS) * lnw_ref[...] + lnb_ref[...]
    g = _dot(jax.nn.sigmoid(gl_ref[...]), rg2_ref[...])
    out_r = (yn_r + bv_ref[0] + bv_ref[1]) * g
    hn = _group_norm(hm_ref[0] + hm_ref[1], avg128_ref[...], NORM_EPS) * mlw_ref[...]
    out_m = jax.nn.sigmoid(mo_ref[...]) * hn
    merged = (jax.nn.sigmoid(g0_ref[...]) * _dot(out_r, wor_ref[...])
              + jax.nn.sigmoid(g1_ref[...]) * _dot(out_m, wom_ref[...])
              + jax.nn.sigmoid(g2_ref[...]) * _dot(yn_ref[...], won_ref[...]))
    o = _dot(merged, wout_ref[...])
    x1 = x_ref[...] + m[2:3] * _rms(o, gpost_ref[...])
    x1_ref[...] = x1
    hin = _rms(x1, gpre_ref[...]) * (1.0 + m[4:5]) + m[3:4]
    hin_ref[...] = hin.astype(BF16)
    if with_router:
        lg_ref[...] = _dot3(hin, wr_ref[...]) + br_ref[...]


def _merge_call(x, mod3, yr, bv, hm, yn, z, W, n_ctx, lat_t, router, tm=256):
    ntok = x.shape[0]
    midx = _mod_index(tm, n_ctx, lat_t)
    with_router = router is not None
    row = lambda w: pl.BlockSpec((tm, w), lambda i: (i, 0))
    dirs = pl.BlockSpec((2, tm, HW), lambda i: (0, i, 0))
    zspec = lambda w, cb: pl.BlockSpec((tm, w), lambda i: (i, cb))
    full = lambda a: pl.BlockSpec(a.shape, lambda i: (0,) * a.ndim)
    consts = [W['avg64'], W['avg128'], W['rg2'], W['lnw'], W['lnb'], W['mlw'], W['wor'], W['wom'], W['won'],
              W['wout'], W['gpost'], W['gpre']]
    in_specs = [row(D), pl.BlockSpec((1, 8, D), lambda i: (midx(i), 0, 0)), dirs, dirs, dirs, row(HW),
                zspec(LANE, ZB_GL), zspec(HW, ZB_MO // 4),
                zspec(D, ZB_GATE // 8), zspec(D, ZB_GATE // 8 + 1), zspec(D, ZB_GATE // 8 + 2)]
    in_specs += [full(a) for a in consts]
    args = [x, mod3, yr, bv, hm, yn, z, z, z, z, z] + consts
    out_shape = [jax.ShapeDtypeStruct((ntok, D), F32), jax.ShapeDtypeStruct((ntok, D), BF16)]
    out_specs = [row(D), row(D)]
    if with_router:
        in_specs += [full(router[0]), full(router[1])]
        args += list(router)
        out_shape.append(jax.ShapeDtypeStruct((ntok, LANE), F32))
        out_specs.append(row(LANE))
    return pl.pallas_call(
        functools.partial(_merge_body, with_router=with_router),
        out_shape=out_shape,
        grid=(ntok // tm,),
        in_specs=in_specs,
        out_specs=out_specs,
        compiler_params=_params(("parallel",)),
        name="branch_merge",
    )(*args)


def _top2_gate(lg):
    lane = lax.broadcasted_iota(jnp.int32, lg.shape, 1)
    m1 = jnp.max(lg, axis=1, keepdims=True)
    i1 = jnp.min(jnp.where(lg == m1, lane, LANE), axis=1, keepdims=True)
    first = lane == i1
    lg2 = jnp.where(first, -jnp.inf, lg)
    m2 = jnp.max(lg2, axis=1, keepdims=True)
    i2 = jnp.min(jnp.where(lg2 == m2, lane, LANE), axis=1, keepdims=True)
    e2 = jnp.exp(m2 - m1)
    den = 1.0 + e2
    return jnp.where(first, 1.0 / den, 0.0) + jnp.where(lane == i2, e2 / den, 0.0)


def _ffn_body(*refs, moe):
    refs = list(refs)
    h_ref, x1_ref, mod_ref, gpost_ref = [refs.pop(0) for _ in range(4)]
    lg_ref = refs.pop(0) if moe else None
    wg_ref, wu_ref, wd_ref, o_ref, acc_s = [refs.pop(0) for _ in range(5)]
    gate_s = refs.pop(0) if moe else None
    e = pl.program_id(1)
    j = pl.program_id(2)
    n_e = pl.num_programs(1)
    n_j = pl.num_programs(2)

    @pl.when((e == 0) & (j == 0))
    def _():
        acc_s[...] = jnp.zeros_like(acc_s)
        if moe:
            gate_s[...] = _top2_gate(lg_ref[...])

    hin = h_ref[...]
    hg = jnp.dot(hin, wg_ref[0], preferred_element_type=F32)
    hu = jnp.dot(hin, wu_ref[0], preferred_element_type=F32)
    part = _dot(hg * jax.nn.sigmoid(hg) * hu, wd_ref[0])
    if moe:
        lane = lax.broadcasted_iota(jnp.int32, gate_s.shape, 1)
        part = part * jnp.sum(jnp.where(lane == e, gate_s[...], 0.0), axis=1, keepdims=True)
    acc_s[...] += part

    @pl.when((e == n_e - 1) & (j == n_j - 1))
    def _():
        m = mod_ref[0]
        o_ref[...] = x1_ref[...] + m[5:6] * _rms(acc_s[...], gpost_ref[...])


def _ffn_call(hin, x1, mod3, gpost, logits, wg, wu, wd, n_ctx, lat_t, tm=512, tf=896):
    ntok = hin.shape[0]
    n_e, _, dff = wg.shape
    moe = logits is not None
    midx = _mod_index(tm, n_ctx, lat_t)
    in_specs = [pl.BlockSpec((tm, D), lambda i, e, j: (i, 0)),
                pl.BlockSpec((tm, D), lambda i, e, j: (i, 0)),
                pl.BlockSpec((1, 8, D), lambda i, e, j: (midx(i), 0, 0)),
                pl.BlockSpec((1, D), lambda i, e, j: (0, 0))]
    args = [hin, x1, mod3, gpost]
    scratch = [pltpu.VMEM((tm, D), F32)]
    if moe:
        in_specs.append(pl.BlockSpec((tm, LANE), lambda i, e, j: (i, 0)))
        args.append(logits)
        scratch.append(pltpu.VMEM((tm, LANE), F32))
    in_specs += [pl.BlockSpec((1, D, tf), lambda i, e, j: (e, 0, j)),
                 pl.BlockSpec((1, D, tf), lambda i, e, j: (e, 0, j)),
                 pl.BlockSpec((1, tf, D), lambda i, e, j: (e, j, 0))]
    args += [wg, wu, wd]
    return pl.pallas_call(
        functools.partial(_ffn_body, moe=moe),
        out_shape=jax.ShapeDtypeStruct((ntok, D), F32),
        grid=(ntok // tm, n_e, dff // tf),
        in_specs=in_specs,
        out_specs=pl.BlockSpec((tm, D), lambda i, e, j: (i, 0)),
        scratch_shapes=scratch,
        compiler_params=_params(("parallel", "arbitrary", "arbitrary")),
        name="moe_swiglu" if moe else "dense_swiglu",
    )(*args)


def _rope_tables(t):
    i = jnp.arange(M_HD)
    half, pair, f = i // 64, (i % 64) // 32, i % 32
    tt = jnp.arange(t)
    pos = jnp.stack([tt // GRID_W, tt % GRID_W], axis=-1).astype(F32)
    inv = ROPE_BASE ** (-f.astype(F32) / 32)
    ang = pos[:, half] * inv[None, :]
    sin = jnp.sin(ang)
    return (jnp.cos(ang), jnp.where(pair == 0, -sin, 0.0)[...], jnp.where(pair == 1, sin, 0.0))


def _pad_in_proj(w):
    off = {}
    o = 0
    for name, width in (('r', 512), ('k', 512), ('v', 512), ('wl', 128), ('al', 128), ('gl', 128),
                        ('mq', 512), ('mk', 512), ('mv', 512), ('mo', 512), ('mif', 16),
                        ('nq', 512), ('nk', 512), ('nv', 512), ('gate', 3072)):
        off[name] = (o, o + width)
        o += width
    col = lambda n: w[:, off[n][0]:off[n][1]]
    parts = [col(n) for n in ('r', 'k', 'v', 'mq', 'mk', 'mv', 'mo', 'nq', 'nk', 'nv', 'gate', 'wl', 'al', 'gl', 'mif')]
    parts.append(jnp.zeros((w.shape[0], LANE - 16), w.dtype))
    return jnp.concatenate(parts, axis=1).astype(BF16)


def _block_avg(width, group):
    i = jnp.arange(width) // group
    return jnp.where(i[:, None] == i[None, :], 1.0 / group, 0.0).astype(BF16)


def _zero_pad_rows(w2):
    z = jnp.zeros_like(w2[0])
    return jnp.stack([jnp.concatenate([w2[0], z], axis=0), jnp.concatenate([z, w2[1]], axis=0)])


def kernel(x_prompt, x_sample, state_rwkv, state_mlstm_C, state_mlstm_n, state_mlstm_m, cache_nat_k, cache_nat_v,
           c, c_ctx, w_mod, b_mod, g_pre_mix, g_post_mix, g_pre_ffn, g_post_ffn, w_in,
           rw_w0, rw_w2, rw_a0, rw_a2, rw_g2, rw_k_k, rw_k_a, rw_r_k, rw_ln_w, rw_ln_b,
           ml_b_i, ml_b_f, ml_norm_w, nat_rpb, w_o_rwkv, w_o_mlstm, w_o_nat, w_out,
           ff_w_gate, ff_w_up, ff_w_down, moe_w_router, moe_b_router, moe_w_gate, moe_w_up, moe_w_down):
    cb, ct = x_prompt.shape[:2]
    lb, lt = x_sample.shape[:2]
    n_ctx, n_lat = cb * ct, lb * lt
    x = jnp.concatenate([x_prompt.reshape(n_ctx, D), x_sample.reshape(n_lat, D)], axis=0)
    cvec = jnp.zeros((8, D), F32).at[0].set(c_ctx).at[1:1 + lb].set(c)
    rope = _rope_tables(lt)
    avg64, avg128 = _block_avg(HW, R_HD), _block_avg(HW, M_HD)
    lat_tb = min(lt, 512)
    states = []
    for l in range(DEPTH):
        mod = _mod_call(cvec, w_mod[l], b_mod[l])
        mod3 = jnp.pad(mod[:1 + lb].reshape(1 + lb, 6, D), ((0, 0), (0, 2), (0, 0)))
        z = _proj_call(x, mod3, g_pre_mix[l], _pad_in_proj(w_in[l]), n_ctx, lt)

        rp = dict(w0=rw_w0[l].reshape(2, 1, HW), w2=_zero_pad_rows(rw_w2[l]),
                  a0=rw_a0[l].reshape(2, 1, HW), a2=_zero_pad_rows(rw_a2[l]),
                  kk=rw_k_k[l].reshape(1, HW), ka=rw_k_a[l].reshape(1, HW), rk=rw_r_k[l].reshape(1, HW))
        yr_c, bv_c, s_rw = _rwkv_call(z, rp, None, row0=0, nb=cb, t=ct, tb=RW_BLOCK, has_out=True)
        yr_l, bv_l = _rwkv_call(z, rp, state_rwkv[:, l], row0=n_ctx, nb=lb, t=lt, tb=RW_BLOCK, has_out=False)

        gt = z[:, ZB_MIF * LANE:ZB_MIF * LANE + 16].reshape(-1, CHUNK, 16).transpose(0, 2, 1)
        hm_c, c_m, n_m, m_m = _mlstm_call(z, gt, None, ml_b_i[l], ml_b_f[l], None,
                                          row0=0, nb=cb, t=ct, tb=ct, has_out=True)
        init = (state_mlstm_C[:, l], state_mlstm_n[:, l][:, :, :, None, :], state_mlstm_m[:, l])
        hm_l, = _mlstm_call(z, gt, rope, ml_b_i[l], ml_b_f[l], init,
                            row0=n_ctx, nb=lb, t=lt, tb=lat_tb, has_out=False)

        yn_c, k_c, v_c = _ctx_attn_call(z, nb=cb, t=ct)
        yn_l = _natten_call(z, cache_nat_k[:, l], cache_nat_v[:, l], _natten_table(nat_rpb[l]),
                            row0=n_ctx, nb=lb, t=lt)
        states.append((s_rw, c_m, n_m[:, :, :, 0, :], m_m[:, :, :, 0, 0], k_c, v_c))

        cat = lambda a, b_, ax: jnp.concatenate([a, b_], axis=ax)
        mw = dict(avg64=avg64, avg128=avg128, rg2=rw_g2[l].astype(BF16), lnw=rw_ln_w[l].reshape(1, HW),
                  lnb=rw_ln_b[l].reshape(1, HW), mlw=ml_norm_w[l].reshape(1, HW),
                  wor=w_o_rwkv[l].astype(BF16), wom=w_o_mlstm[l].astype(BF16), won=w_o_nat[l].astype(BF16),
                  wout=w_out[l].astype(BF16), gpost=g_post_mix[l].reshape(1, D), gpre=g_pre_ffn[l].reshape(1, D))
        j = l // 2
        if l % 2 == 0:
            router = None
        else:
            wr = jnp.pad(moe_w_router[j], ((0, 0), (0, LANE - N_EXPERTS)))
            br = jnp.pad(moe_b_router[j], (0, LANE - N_EXPERTS), constant_values=NEG).reshape(1, LANE)
            router = (wr, br)
        merged = _merge_call(x, mod3, cat(yr_c, yr_l, 1), cat(bv_c, bv_l, 1), cat(hm_c, hm_l, 1),
                             cat(yn_c, yn_l, 0), z, mw, n_ctx, lt, router)
        gpost = g_post_ffn[l].reshape(1, D)
        if l % 2 == 0:
            x = _ffn_call(merged[1], merged[0], mod3, gpost, None, ff_w_gate[j:j + 1].astype(BF16),
                          ff_w_up[j:j + 1].astype(BF16), ff_w_down[j:j + 1].astype(BF16), n_ctx, lt, tf=1408)
        else:
            x = _ffn_call(merged[1], merged[0], mod3, gpost, merged[2], moe_w_gate[j].astype(BF16),
                          moe_w_up[j].astype(BF16), moe_w_down[j].astype(BF16), n_ctx, lt, tf=896)

    stack = lambda k: jnp.stack([s[k] for s in states], axis=1)
    return (x[:n_ctx].reshape(cb, ct, D), x[n_ctx:].reshape(lb, lt, D),
            stack(0), stack(1), stack(2), stack(3), stack(4), stack(5))
```

```python
import functools
import math

import jax
import jax.numpy as jnp
from jax import lax
from jax.experimental import pallas as pl
from jax.experimental.pallas import tpu as pltpu

F32 = jnp.float32
BF16 = jnp.bfloat16

D = 1024
N_CTX_B, N_CTX_T = 32, 256
N_LAT_B, N_LAT_T = 2, 4096
DEPTH = 2
GRID_W = 64
R_HEADS, R_HD = 8, 64
M_HEADS, M_HD = 4, 128
N_HEADS, N_HD = 8, 64
HW = 512
WIN_ROWS, WIN_COLS = 8, 16
N_EXPERTS = 8
RWKV_GN_EPS = 64e-5
NORM_EPS = 1e-6
ROPE_BASE = 10000.0
CHUNK = 64
NEG = -1e30
LANE = 128
VMEM_LIMIT = 56 * 1024 * 1024

ZB_R, ZB_K, ZB_V = 0, 4, 8
ZB_MQ, ZB_MK, ZB_MV, ZB_MO = 12, 16, 20, 24
ZB_NQ, ZB_NK, ZB_NV = 28, 32, 36
ZB_GATE = 40
ZB_WL, ZB_AL, ZB_GL, ZB_MIF = 64, 65, 66, 67
P_PAD = 68 * LANE

NN = (((1,), (0,)), ((), ()))
NT = (((1,), (1,)), ((), ()))
TN = (((0,), (0,)), ((), ()))


def _dot(a, b, dims=NN):
    return lax.dot_general(a.astype(BF16), b.astype(BF16), dims, preferred_element_type=F32)


def _split(x):
    hi = x.astype(BF16)
    return hi, (x - hi.astype(F32)).astype(BF16)


def _dot3(a, b, dims=NN):
    ah, al = _split(a)
    bh, bl = _split(b)
    d = lambda u, w: lax.dot_general(u, w, dims, preferred_element_type=F32)
    return d(ah, bh) + (d(ah, bl) + d(al, bh))


def _dot_lhs2(a, b_exact, dims=NN):
    ah, al = _split(a)
    d = lambda u: lax.dot_general(u, b_exact, dims, preferred_element_type=F32)
    return d(ah) + d(al)


def _dot_rhs2(a_exact, b, dims=NN):
    bh, bl = _split(b)
    d = lambda w: lax.dot_general(a_exact, w, dims, preferred_element_type=F32)
    return d(bh) + d(bl)


def _log_sigmoid(x):
    return jnp.minimum(x, 0.0) - jnp.log1p(jnp.exp(-jnp.abs(x)))


def _rms(x, g):
    return x * lax.rsqrt(jnp.mean(x * x, axis=-1, keepdims=True) + NORM_EPS) * g


def _params(sem):
    return pltpu.CompilerParams(dimension_semantics=sem, vmem_limit_bytes=VMEM_LIMIT)


def _alias_prev(in_specs, args, prev):
    aliases = {}
    for k, a in enumerate(prev or ()):
        aliases[len(args)] = k
        in_specs.append(pl.BlockSpec(memory_space=pl.ANY))
        args.append(a)
    return aliases


def _mod_body(c_ref, w_ref, b_ref, o_ref):
    c = c_ref[...]
    o_ref[...] = _dot(c * jax.nn.sigmoid(c), w_ref[...]) + b_ref[...]


def _mod_call(cvec, w_mod, b_mod):
    tn = 1536
    return pl.pallas_call(
        _mod_body,
        out_shape=jax.ShapeDtypeStruct((8, 6 * D), F32),
        grid=(6 * D // tn,),
        in_specs=[pl.BlockSpec((8, D), lambda j: (0, 0)),
                  pl.BlockSpec((D, tn), lambda j: (0, j)),
                  pl.BlockSpec((1, tn), lambda j: (0, j))],
        out_specs=pl.BlockSpec((8, tn), lambda j: (0, j)),
        compiler_params=_params(("arbitrary",)),
        name="adaln_mod",
    )(cvec, w_mod, b_mod.reshape(1, 6 * D))


def _mod_index(tile_rows, n_ctx, lat_t):
    def f(i):
        start = i * tile_rows
        return jnp.where(start < n_ctx, 0, 1 + (start - n_ctx) // lat_t)
    return f


def _proj_body(x_ref, mod_ref, g_ref, w_ref, z_ref, hin_s):
    @pl.when(pl.program_id(1) == 0)
    def _():
        m = mod_ref[0]
        hin_s[...] = (_rms(x_ref[...], g_ref[...]) * (1.0 + m[1:2]) + m[0:1]).astype(BF16)
    z_ref[...] = jnp.dot(hin_s[...], w_ref[...], preferred_element_type=F32)


def _proj_call(x, mod3, g, w_bf, n_ctx, lat_t, tm=512, tn=2176):
    ntok = x.shape[0]
    midx = _mod_index(tm, n_ctx, lat_t)
    return pl.pallas_call(
        _proj_body,
        out_shape=jax.ShapeDtypeStruct((ntok, P_PAD), F32),
        grid=(ntok // tm, P_PAD // tn),
        in_specs=[pl.BlockSpec((tm, D), lambda i, j: (i, 0)),
                  pl.BlockSpec((1, 8, D), lambda i, j: (midx(i), 0, 0)),
                  pl.BlockSpec((1, D), lambda i, j: (0, 0)),
                  pl.BlockSpec((D, tn), lambda i, j: (0, j))],
        out_specs=pl.BlockSpec((tm, tn), lambda i, j: (i, j)),
        scratch_shapes=[pltpu.VMEM((tm, D), BF16)],
        compiler_params=_params(("parallel", "arbitrary")),
        name="in_proj",
    )(x, mod3, g.reshape(1, D), w_bf)


RW_TILE = 2 * CHUNK
RW_BLOCK = 512


def _rwkv_body(*refs, tb_rows, nseq, has_init, has_out, n_alias):
    (r_ref, k_ref, v_ref, wl_ref, al_ref, w0_ref, w2_ref, a0_ref, a2_ref,
     kkw_ref, kaw_ref, rkw_ref, hsum_ref) = refs[:13]
    rest = list(refs[13:])
    s0_ref = rest.pop(0) if has_init else None
    del rest[:n_alias]
    y_ref, bv_ref = rest.pop(0), rest.pop(0)
    sout_ref = rest.pop(0) if has_out else None
    r2_s, m_s, g_s, st_s = rest
    C = CHUNK
    TL = RW_TILE
    nchunk = tb_rows // nseq // C
    d = pl.program_id(2)
    tb = pl.program_id(3)
    n_tb = pl.num_programs(3)

    ti = lax.broadcasted_iota(jnp.int32, (TL, TL), 0)
    si = lax.broadcasted_iota(jnp.int32, (TL, TL), 1)
    head_blk = (ti >> 6) == (si >> 6)

    @pl.when(tb == 0)
    def _():
        for q in range(nseq):
            if has_init:
                s0 = s0_ref[q, 0]
                two = jnp.concatenate([jnp.concatenate([s0[0], s0[0]], axis=1),
                                       jnp.concatenate([s0[1], s0[1]], axis=1)], axis=0)
                st_s[q] = jnp.where(head_blk, two, 0.0)
            else:
                st_s[q] = jnp.zeros((LANE, LANE), F32)

    hsum = hsum_ref[...]
    r = r_ref[...]
    kf = k_ref[...]
    v = v_ref[...]
    w_pre = w0_ref[0] + _dot(jnp.tanh(wl_ref[...]), w2_ref[0])
    lw = -math.exp(-0.5) * jax.nn.sigmoid(w_pre)
    a = jax.nn.sigmoid(a0_ref[0] + _dot(al_ref[...], a2_ref[0]))
    kd = kf * (1.0 + (a - 1.0) * kaw_ref[...])
    kkf = kf * kkw_ref[...]
    kn = kkf * lax.rsqrt(_dot_lhs2(kkf * kkf, hsum) + 1e-12)
    bd = kn * a
    bv_ref[0] = _dot_lhs2(r * kd * rkw_ref[...], hsum) * v

    dlt = (ti - si) * (1 - 2 * d)
    strict = head_blk & (dlt > 0)
    incl = head_blk & (dlt >= 0)
    cum_ones = jnp.concatenate([jnp.where(incl, 1.0, 0.0), jnp.where(head_blk, 1.0, 0.0)], axis=0).astype(BF16)
    is_diag = ti == si
    eye = jnp.where(is_diag, 1.0, 0.0)
    pair = (ti >> 1) == (si >> 1)
    sibling = [((ti >> lg) ^ (si >> lg)) == 1 for lg in range(1, 6)]
    head0 = si < R_HD

    n_tiles = tb_rows // TL
    tiles = []
    for tile in range(n_tiles):
        rows = slice(TL * tile, TL * (tile + 1))
        cs = _dot_rhs2(cum_ones, lw[rows])
        cum, tot = cs[:TL], cs[TL:]
        e_neg = jnp.exp(-cum)
        e_hat = jnp.exp(tot - cum)
        rt = r[rows] * jnp.exp(cum)
        at = -kn[rows] * jnp.exp(cum - lw[rows])
        tiles.append(dict(rows=rows, tot=tot, rt=rt, at=at, bh=bd[rows] * e_hat, kh=kd[rows] * e_hat,
                          rhs=jnp.concatenate([bd[rows] * e_neg, kd[rows] * e_neg], axis=0).astype(BF16),
                          at_bf=at.astype(BF16), v_bf=v[rows].astype(BF16)))
    chains = [(tl, h) for tl in tiles for h in range(2)]
    a_all = []
    for tl, h in chains:
        hm = head0 if h == 0 else ~head0
        lhs = jnp.concatenate([jnp.where(hm, tl['at'], 0.0), jnp.where(hm, tl['rt'], 0.0)], axis=0)
        a_all.append(_dot(lhs, tl['rhs'], NT))
    n_mat = [jnp.where(strict, a[:TL, :TL], 0.0) for a in a_all]
    a_ak = [jnp.where(strict, a[:TL, TL:], 0.0).astype(BF16) for a in a_all]
    a_rb = [jnp.where(incl, a[TL:, :TL], 0.0).astype(BF16) for a in a_all]
    a_rk = [jnp.where(incl, a[TL:, TL:], 0.0).astype(BF16) for a in a_all]
    t_inv = [eye + jnp.where(pair, n, 0.0) for n in n_mat]
    for sib in sibling:
        x_mat = [_dot(jnp.where(sib, n, 0.0), t) for n, t in zip(n_mat, t_inv)]
        t_inv = [t + _dot(t, x) for t, x in zip(t_inv, x_mat)]
    p_h = [_dot(t, tl['at_bf']) for t, (tl, _) in zip(t_inv, chains)]
    av = [_dot(a, tl['v_bf']) for a, (tl, _) in zip(a_ak, chains)]
    q_h = [_dot(t, x) for t, x in zip(t_inv, av)]
    r2_h = [_dot(a, p) for a, p in zip(a_rb, p_h)]
    y0_h = [_dot(a, q) + _dot(k_, tl['v_bf']) for a, q, k_, (tl, _) in zip(a_rb, q_h, a_rk, chains)]
    for i, tl in enumerate(tiles):
        rows = tl['rows']
        pick = lambda u: jnp.where(head0, u[2 * i], u[2 * i + 1])
        p_m, q_m = pick(p_h), pick(q_h)
        r2_s[rows, :] = tl['rt'] + pick(r2_h)
        y_ref[0, rows, :] = pick(y0_h)
        for c in range(TL // C):
            cr = slice(C * c, C * (c + 1))
            decay = jnp.where(is_diag, jnp.exp(tl['tot'][C * c:C * c + 1]), 0.0)
            m_s[i * (TL // C) + c] = decay + jnp.where(head_blk, _dot(p_m[cr], tl['bh'][cr], TN), 0.0)
            g_s[i * (TL // C) + c] = jnp.where(
                head_blk, _dot(q_m[cr], tl['bh'][cr], TN) + _dot(v[rows][cr], tl['kh'][cr], TN), 0.0)

    def step(i, states):
        ci = jnp.where(d == 0, i, nchunk - 1 - i)
        new = []
        for q, s in enumerate(states):
            cq = q * nchunk + ci
            rows = pl.ds(pl.multiple_of(cq * C, C), C)
            y_ref[0, rows, :] = y_ref[0, rows, :] + _dot(r2_s[rows, :], s, NT)
            new.append(_dot3(s, m_s[cq]) + g_s[cq])
        return tuple(new)

    s_fin = lax.fori_loop(0, nchunk, step, tuple(st_s[q] for q in range(nseq)))
    for q in range(nseq):
        st_s[q] = s_fin[q]

    if has_out:
        @pl.when(tb == n_tb - 1)
        def _():
            for q in range(nseq):
                sout_ref[q, 0, 0] = s_fin[q][:R_HD, :R_HD]
                sout_ref[q, 0, 1] = s_fin[q][R_HD:, R_HD:]


def _rwkv_call(z, P, s0, prev, *, row0, nb, t, tb, has_out):
    nseq = max(tb // t, 1)
    n_tb = max(t // tb, 1)
    blk0 = row0 // tb
    has_init = s0 is not None

    def rowblk(b, d, i):
        return blk0 + b * n_tb + jnp.where(d == 0, i, n_tb - 1 - i)

    def zspec(cb):
        return pl.BlockSpec((tb, LANE), lambda b, hp, d, i: (rowblk(b, d, i), cb(hp)))

    vec = lambda: pl.BlockSpec((1, LANE), lambda b, hp, d, i: (0, hp))
    dvec = lambda: pl.BlockSpec((1, 1, LANE), lambda b, hp, d, i: (d, 0, hp))
    dmat = lambda: pl.BlockSpec((1, LANE, LANE), lambda b, hp, d, i: (d, 0, hp))
    st_spec = pl.BlockSpec((nseq, 1, 2, R_HD, R_HD), lambda b, hp, d, i: (b, d, hp, 0, 0))
    in_specs = [zspec(lambda hp: ZB_R + hp), zspec(lambda hp: ZB_K + hp), zspec(lambda hp: ZB_V + hp),
                zspec(lambda hp: ZB_WL), zspec(lambda hp: ZB_AL),
                dvec(), dmat(), dvec(), dmat(), vec(), vec(), vec(),
                pl.BlockSpec((LANE, LANE), lambda b, hp, d, i: (0, 0))]
    args = [z, z, z, z, z, P['w0'], P['w2'], P['a0'], P['a2'], P['kk'], P['ka'], P['rk'],
            _block_avg(LANE, R_HD) * R_HD]
    if has_init:
        in_specs.append(st_spec)
        args.append(s0)
    aliases = _alias_prev(in_specs, args, prev)
    yspec = pl.BlockSpec((1, tb, LANE), lambda b, hp, d, i: (d, rowblk(b, d, i), hp))
    out_shape = [jax.ShapeDtypeStruct((2, z.shape[0], HW), F32)] * 2
    out_specs = [yspec, yspec]
    if has_out:
        out_shape.append(jax.ShapeDtypeStruct((nb, 2, R_HEADS, R_HD, R_HD), F32))
        out_specs.append(st_spec)
    return pl.pallas_call(
        functools.partial(_rwkv_body, tb_rows=tb, nseq=nseq, has_init=has_init, has_out=has_out,
                          n_alias=len(aliases)),
        out_shape=out_shape,
        grid=(nb // nseq, R_HEADS // 2, 2, n_tb),
        in_specs=in_specs,
        out_specs=out_specs,
        input_output_aliases=aliases,
        scratch_shapes=[pltpu.VMEM((tb, LANE), F32), pltpu.VMEM((tb // CHUNK, LANE, LANE), F32),
                        pltpu.VMEM((tb // CHUNK, LANE, LANE), F32), pltpu.VMEM((nseq, LANE, LANE), F32)],
        compiler_params=_params(("parallel", "parallel", "arbitrary", "arbitrary")),
        name="rwkv7_chunked",
    )(*args)


def _mlstm_body(*refs, tb_rows, nseq, use_rope, has_init, has_out, n_alias):
    refs = list(refs)
    q_ref, k_ref, v_ref, g_ref, gt_ref = [refs.pop(0) for _ in range(5)]
    cos_ref, sa_ref, sb_ref = [refs.pop(0) for _ in range(3)] if use_rope else (None,) * 3
    bi_ref, bf_ref = refs.pop(0), refs.pop(0)
    c0_ref, n0_ref, m0_ref = [refs.pop(0) for _ in range(3)] if has_init else (None,) * 3
    del refs[:n_alias]
    h_ref = refs.pop(0)
    cout_ref, nout_ref, mout_ref = [refs.pop(0) for _ in range(3)] if has_out else (None,) * 3
    sc_s, kv_s, nk_s, cprev_s, nprev_s, c_s, n_s, m_s = refs
    L = CHUNK
    nck = tb_rows // L
    ncs = nck // nseq
    b = pl.program_id(0)
    h = pl.program_id(1)
    d = pl.program_id(2)
    tb = pl.program_id(3)
    n_tb = pl.num_programs(3)
    bi = bi_ref[d, h]
    bf = bf_ref[d, h]
    ji = d * (2 * M_HEADS) + h
    jf = ji + M_HEADS

    @pl.when(tb == 0)
    def _():
        for q in range(nseq):
            if has_init:
                c_s[q] = c0_ref[q, 0, 0]
                n_s[q] = n0_ref[q, 0, 0]
                m_s[q] = jnp.full((1, M_HD), m0_ref[b * nseq + q, d, h], F32)
            else:
                c_s[q] = jnp.zeros((M_HD, M_HD), F32)
                n_s[q] = jnp.zeros((1, M_HD), F32)
                m_s[q] = jnp.zeros((1, M_HD), F32)

    g = g_ref[...]
    lane = lax.broadcasted_iota(jnp.int32, g.shape, 1)
    icol_all = jnp.sum(jnp.where(lane == ji, g, 0.0), axis=1, keepdims=True) + bi
    fcol_all = _log_sigmoid(jnp.sum(jnp.where(lane == jf, g, 0.0), axis=1, keepdims=True) + bf)

    ti = lax.broadcasted_iota(jnp.int32, (L, L), 0)
    si = lax.broadcasted_iota(jnp.int32, (L, L), 1)
    sgn = 1 - 2 * d
    incl = (ti - si) * sgn >= 0
    incl_t = (si - ti) * sgn >= 0
    row128 = lambda x: jnp.broadcast_to(x, (1, M_HD))

    ch = []
    for c in range(nck):
        rows = slice(L * c, L * (c + 1))
        irow = gt_ref[c, pl.ds(ji, 1), :] + bi
        frow = _log_sigmoid(gt_ref[c, pl.ds(jf, 1), :] + bf)
        bsum_col = jnp.sum(jnp.where(incl, frow, 0.0), axis=1, keepdims=True)
        bsum_row = jnp.sum(jnp.where(incl_t, fcol_all[rows], 0.0), axis=0, keepdims=True)
        blast = jnp.sum(frow, axis=1, keepdims=True)
        wlog = blast - bsum_col + icol_all[rows]
        dmat = jnp.where(incl, bsum_col - bsum_row + irow, NEG)
        sc_s[c, 0:1, :] = row128(blast)
        sc_s[c, 1:2, :] = row128(jnp.max(wlog, axis=0, keepdims=True))
        ch.append(dict(rows=rows, bsum_col=bsum_col, blast=blast, wlog=wlog, dmat=dmat,
                       dmax=jnp.max(dmat, axis=1, keepdims=True)))

    def m_step(i, ms):
        ci = jnp.where(d == 0, i, ncs - 1 - i)
        new = []
        for q, m in enumerate(ms):
            cq = q * ncs + ci
            m_new = jnp.maximum(sc_s[cq, 0:1, :] + m, sc_s[cq, 1:2, :])
            sc_s[cq, 2:3, :] = m
            sc_s[cq, 3:4, :] = m_new
            new.append(m_new)
        return tuple(new)

    m_fin = lax.fori_loop(0, ncs, m_step, tuple(m_s[q] for q in range(nseq)))

    for c, cd in enumerate(ch):
        rows = cd['rows']
        qc, kc = q_ref[rows, :], k_ref[rows, :]
        if use_rope:
            cos, sa, sb = cos_ref[rows, :], sa_ref[rows, :], sb_ref[rows, :]
            qc = qc * cos + pltpu.roll(qc, 96, 1) * sa + pltpu.roll(qc, 32, 1) * sb
            kc = kc * cos + pltpu.roll(kc, 96, 1) * sa + pltpu.roll(kc, 32, 1) * sb
        cd['q'] = qc
        cd['k'] = kc * (M_HD ** -0.5)
        cd['v'] = v_ref[rows, :]
        cd['m_prev'] = sc_s[c, 2:3, 0:1]
        cd['m_new'] = sc_s[c, 3:4, 0:1]
    for cd in ch:
        cd['qk'] = _dot(cd['q'], cd['k'], NT)
    for c, cd in enumerate(ch):
        kw = cd['k'] * jnp.exp(cd['wlog'] - cd['m_new'])
        kv_s[c] = _dot3(kw, cd['v'], TN)
        nk_s[c] = jnp.sum(kw, axis=0, keepdims=True)
        sc_s[c, 4:5, :] = row128(jnp.exp(cd['blast'] + cd['m_prev'] - cd['m_new']))
    for cd in ch:
        inter = cd['bsum_col'] + cd['m_prev']
        cd['mt'] = jnp.maximum(inter, cd['dmax'])
        cd['iw'] = jnp.exp(inter - cd['mt'])
        sm = cd['qk'] * jnp.exp(cd['dmat'] - cd['mt'])
        cd['ssum'] = jnp.sum(sm, axis=1, keepdims=True)
        cd['sv'] = _dot(sm, cd['v'])

    def c_step(i, carry):
        ci = jnp.where(d == 0, i, ncs - 1 - i)
        new = []
        for q, (c_st, n_st) in enumerate(carry):
            cq = q * ncs + ci
            dec = sc_s[cq, 4:5, 0:1]
            cprev_s[cq] = c_st
            nprev_s[cq] = n_st
            new.append((dec * c_st + kv_s[cq], dec * n_st + nk_s[cq]))
        return tuple(new)

    st_fin = lax.fori_loop(0, ncs, c_step, tuple((c_s[q], n_s[q]) for q in range(nseq)))
    for q in range(nseq):
        c_s[q], n_s[q] = st_fin[q]
        m_s[q] = m_fin[q]

    qc_prev = [_dot(cd['q'], cprev_s[c]) for c, cd in enumerate(ch)]
    for c, cd in enumerate(ch):
        num = cd['iw'] * qc_prev[c] + cd['sv']
        den = cd['iw'] * jnp.sum(cd['q'] * nprev_s[c], axis=1, keepdims=True) + cd['ssum']
        h_ref[0, cd['rows'], :] = num / jnp.maximum(jnp.abs(den), jnp.exp(-cd['mt']))

    if has_out:
        @pl.when(tb == n_tb - 1)
        def _():
            for q in range(nseq):
                cout_ref[q, 0, 0] = st_fin[q][0]
                nout_ref[q, 0, 0] = st_fin[q][1]
                mout_ref[q, 0, 0] = m_fin[q]


def _mlstm_call(z, gt, rope, b_i, b_f, init, prev, *, row0, nb, t, tb, has_out):
    nseq = max(tb // t, 1)
    n_tb = max(t // tb, 1)
    blk0 = row0 // tb
    cpb = tb // CHUNK
    use_rope = rope is not None
    has_init = init is not None

    def rowblk(b, d, i):
        return blk0 + b * n_tb + jnp.where(d == 0, i, n_tb - 1 - i)

    def zspec(cb):
        return pl.BlockSpec((tb, LANE), lambda b, h, d, i: (rowblk(b, d, i), cb(h)))

    smem = pl.BlockSpec(memory_space=pltpu.SMEM)
    in_specs = [zspec(lambda h: ZB_MQ + h), zspec(lambda h: ZB_MK + h), zspec(lambda h: ZB_MV + h),
                zspec(lambda h: ZB_MIF),
                pl.BlockSpec((cpb, 16, CHUNK), lambda b, h, d, i: (rowblk(b, d, i), 0, 0))]
    args = [z, z, z, z, gt]
    if use_rope:
        tspec = pl.BlockSpec((tb, LANE), lambda b, h, d, i: (jnp.where(d == 0, i, n_tb - 1 - i), 0))
        in_specs += [tspec] * 3
        args += list(rope)
    in_specs += [smem, smem]
    args += [b_i, b_f]
    cspec = pl.BlockSpec((nseq, 1, 1, M_HD, M_HD), lambda b, h, d, i: (b, d, h, 0, 0))
    nspec = pl.BlockSpec((nseq, 1, 1, 1, M_HD), lambda b, h, d, i: (b, d, h, 0, 0))
    if has_init:
        in_specs += [cspec, nspec, smem]
        args += list(init)
    aliases = _alias_prev(in_specs, args, prev)
    out_shape = [jax.ShapeDtypeStruct((2, z.shape[0], HW), F32)]
    out_specs = [pl.BlockSpec((1, tb, LANE), lambda b, h, d, i: (d, rowblk(b, d, i), h))]
    if has_out:
        out_shape += [jax.ShapeDtypeStruct((nb, 2, M_HEADS, M_HD, M_HD), F32),
                      jax.ShapeDtypeStruct((nb, 2, M_HEADS, 1, M_HD), F32),
                      jax.ShapeDtypeStruct((nb, 2, M_HEADS, 1, M_HD), F32)]
        out_specs += [cspec, nspec, nspec]
    return pl.pallas_call(
        functools.partial(_mlstm_body, tb_rows=tb, nseq=nseq, use_rope=use_rope, has_init=has_init,
                          has_out=has_out, n_alias=len(aliases)),
        out_shape=out_shape,
        grid=(nb // nseq, M_HEADS, 2, n_tb),
        in_specs=in_specs,
        out_specs=out_specs,
        input_output_aliases=aliases,
        scratch_shapes=[pltpu.VMEM((cpb, 8, M_HD), F32),
                        pltpu.VMEM((cpb, M_HD, M_HD), F32), pltpu.VMEM((cpb, 1, M_HD), F32),
                        pltpu.VMEM((cpb, M_HD, M_HD), F32), pltpu.VMEM((cpb, 1, M_HD), F32),
                        pltpu.VMEM((nseq, M_HD, M_HD), F32), pltpu.VMEM((nseq, 1, M_HD), F32),
                        pltpu.VMEM((nseq, 1, M_HD), F32)],
        compiler_params=_params(("parallel", "parallel", "arbitrary", "arbitrary")),
        name="mlstm_chunkwise",
    )(*args)


def _ctx_attn_body(q_ref, k_ref, v_ref, y_ref, ko_ref, vo_ref):
    q = q_ref[...] * (N_HD ** -0.5)
    k = k_ref[...]
    v = v_ref[...]
    for h in range(2):
        sl = slice(N_HD * h, N_HD * (h + 1))
        kh, vh = k[:, sl], v[:, sl]
        ko_ref[0, h] = kh
        vo_ref[0, h] = vh
        s = _dot(q[:, sl], kh, NT)
        p = jnp.exp(s - jnp.max(s, axis=-1, keepdims=True))
        y_ref[:, sl] = _dot(p, vh) / jnp.sum(p, axis=-1, keepdims=True)


def _ctx_attn_call(z, *, nb, t):
    zspec = lambda cb: pl.BlockSpec((t, LANE), lambda b, hp: (b, cb + hp))
    kv_spec = pl.BlockSpec((1, 2, t, N_HD), lambda b, hp: (b, hp, 0, 0))
    kv_shape = jax.ShapeDtypeStruct((nb, N_HEADS, t, N_HD), F32)
    return pl.pallas_call(
        _ctx_attn_body,
        out_shape=[jax.ShapeDtypeStruct((z.shape[0], HW), F32), kv_shape, kv_shape],
        grid=(nb, N_HEADS // 2),
        in_specs=[zspec(ZB_NQ), zspec(ZB_NK), zspec(ZB_NV)],
        out_specs=[pl.BlockSpec((t, LANE), lambda b, hp: (b, hp)), kv_spec, kv_spec],
        compiler_params=_params(("parallel", "parallel")),
        name="context_attention",
    )(z, z, z)


def _natten_body(q_ref, k_ref, v_ref, kc_ref, vc_ref, tab_ref, prev_ref, y_ref, *, rows_per_step, rows_n):
    del prev_ref
    rb = pl.program_id(2)
    n_loc = WIN_ROWS * GRID_W

    def row(i, carry):
        r = rb * rows_per_step + i
        rstart = jnp.clip(r - WIN_ROWS // 2, 0, rows_n - WIN_ROWS)
        var = r - rstart
        qrows = pl.ds(pl.multiple_of(i * GRID_W, GRID_W), GRID_W)
        krows = pl.ds(pl.multiple_of(rstart * GRID_W, GRID_W), n_loc)
        for h in range(2):
            sl = slice(N_HD * h, N_HD * (h + 1))
            q = q_ref[qrows, sl] * (N_HD ** -0.5)
            s_loc = _dot(q, k_ref[krows, sl], NT) + tab_ref[h, var]
            s_ctx = _dot(q, kc_ref[0, h], NT)
            m = jnp.maximum(jnp.max(s_loc, axis=-1, keepdims=True), jnp.max(s_ctx, axis=-1, keepdims=True))
            p_loc = jnp.exp(s_loc - m)
            p_ctx = jnp.exp(s_ctx - m)
            den = jnp.sum(p_loc, axis=-1, keepdims=True) + jnp.sum(p_ctx, axis=-1, keepdims=True)
            y_ref[qrows, sl] = (_dot(p_loc, v_ref[krows, sl]) + _dot(p_ctx, vc_ref[0, h])) / den
        return carry

    lax.fori_loop(0, rows_per_step, row, 0)


def _natten_call(z, k_ctx, v_ctx, tab, prev, *, row0, nb, t, rows_per_step=8):
    rows_n = t // GRID_W
    tq = rows_per_step * GRID_W
    qblk0 = row0 // tq
    sblk0 = row0 // t
    n_rb = rows_n // rows_per_step
    past = k_ctx.shape[2]
    kv_spec = lambda cb: pl.BlockSpec((t, LANE), lambda b, hp, rb: (sblk0 + b, cb + hp))
    cache_spec = pl.BlockSpec((1, 2, past, N_HD), lambda b, hp, rb: (b, hp, 0, 0))
    return pl.pallas_call(
        functools.partial(_natten_body, rows_per_step=rows_per_step, rows_n=rows_n),
        out_shape=jax.ShapeDtypeStruct((z.shape[0], HW), F32),
        grid=(nb, N_HEADS // 2, n_rb),
        in_specs=[pl.BlockSpec((tq, LANE), lambda b, hp, rb: (qblk0 + b * n_rb + rb, ZB_NQ + hp)),
                  kv_spec(ZB_NK), kv_spec(ZB_NV), cache_spec, cache_spec,
                  pl.BlockSpec((2, WIN_ROWS, GRID_W, WIN_ROWS * GRID_W), lambda b, hp, rb: (hp, 0, 0, 0)),
                  pl.BlockSpec(memory_space=pl.ANY)],
        out_specs=pl.BlockSpec((tq, LANE), lambda b, hp, rb: (qblk0 + b * n_rb + rb, hp)),
        input_output_aliases={6: 0},
        compiler_params=_params(("parallel", "parallel", "arbitrary")),
        name="neighbourhood_attention",
    )(z, z, z, k_ctx, v_ctx, tab, prev)


def _natten_table(rpb):
    cq = jnp.arange(GRID_W)[:, None]
    ck = jnp.arange(GRID_W)[None, :]
    dc = jnp.clip(ck - cq, -(WIN_COLS - 1), WIN_COLS - 1) + (WIN_COLS - 1)
    cstart = jnp.clip(cq - WIN_COLS // 2, 0, GRID_W - WIN_COLS)
    in_band = (ck >= cstart) & (ck < cstart + WIN_COLS)
    onehot = ((dc[None] == jnp.arange(2 * WIN_COLS - 1)[:, None, None]) & in_band[None]).astype(F32)
    cols = jnp.einsum('hrd,dqk->hrqk', rpb, onehot, precision=lax.Precision.HIGHEST)
    cols = cols + jnp.where(in_band, 0.0, NEG)
    tab = jnp.stack([cols[:, WIN_ROWS - 1 - var:2 * WIN_ROWS - 1 - var] for var in range(WIN_ROWS)], axis=1)
    return tab.transpose(0, 1, 3, 2, 4).reshape(rpb.shape[0], WIN_ROWS, GRID_W, WIN_ROWS * GRID_W)


def _group_norm(y, avg_bf, eps):
    mu = _dot_lhs2(y, avg_bf)
    yc = y - mu
    var = _dot_lhs2(yc * yc, avg_bf)
    return yc * lax.rsqrt(var + eps)


def _merge_body(*refs, with_router):
    refs = list(refs)
    (x_ref, mod_ref, yr_ref, bv_ref, hm_ref, yn_ref, gl_ref, mo_ref, g0_ref, g1_ref, g2_ref,
     avg64_ref, avg128_ref, rg2_ref, lnw_ref, lnb_ref, mlw_ref, wor_ref, wom_ref, won_ref, wout_ref,
     gpost_ref, gpre_ref) = [refs.pop(0) for _ in range(23)]
    wr_ref, br_ref = (refs.pop(0), refs.pop(0)) if with_router else (None, None)
    x1_ref, hin_ref = refs.pop(0), refs.pop(0)
    sel_ref = refs.pop(0) if with_router else None
    m = mod_ref[0]
    yr = yr_ref[0] + yr_ref[1]
    yn_r = _group_norm(yr, avg64_ref[...], RWKV_GN_EPS) * lnw_ref[...] + lnb_ref[...]
    g = _dot(jax.nn.sigmoid(gl_ref[...]), rg2_ref[...])
    out_r = (yn_r + bv_ref[0] + bv_ref[1]) * g
    hn = _group_norm(hm_ref[0] + hm_ref[1], avg128_ref[...], NORM_EPS) * mlw_ref[...]
    out_m = jax.nn.sigmoid(mo_ref[...]) * hn
    merged = (jax.nn.sigmoid(g0_ref[...]) * _dot(out_r, wor_ref[...])
              + jax.nn.sigmoid(g1_ref[...]) * _dot(out_m, wom_ref[...])
              + jax.nn.sigmoid(g2_ref[...]) * _dot(yn_ref[...], won_ref[...]))
    o = _dot(merged, wout_ref[...])
    x1 = x_ref[...] + m[2:3] * _rms(o, gpost_ref[...])
    x1_ref[...] = x1
    hin = _rms(x1, gpre_ref[...]) * (1.0 + m[4:5]) + m[3:4]
    if with_router:
        sel_ref[...] = _top2_select(_dot3(hin, wr_ref[...]) + br_ref[...])
        lo = lax.bitcast_convert_type(hin[:, :D // 2].astype(BF16).astype(F32), jnp.uint32)
        hi = lax.bitcast_convert_type(hin[:, D // 2:].astype(BF16).astype(F32), jnp.uint32)
        hin_ref[...] = (lo >> 16) | hi
    else:
        hin_ref[...] = hin.astype(BF16)


def _merge_call(x, mod3, yr, bv, hm, yn, z, W, n_ctx, lat_t, router, tm=256):
    ntok = x.shape[0]
    midx = _mod_index(tm, n_ctx, lat_t)
    with_router = router is not None
    row = lambda w: pl.BlockSpec((tm, w), lambda i: (i, 0))
    dirs = pl.BlockSpec((2, tm, HW), lambda i: (0, i, 0))
    zspec = lambda w, cb: pl.BlockSpec((tm, w), lambda i: (i, cb))
    full = lambda a: pl.BlockSpec(a.shape, lambda i: (0,) * a.ndim)
    consts = [W['avg64'], W['avg128'], W['rg2'], W['lnw'], W['lnb'], W['mlw'], W['wor'], W['wom'], W['won'],
              W['wout'], W['gpost'], W['gpre']]
    in_specs = [row(D), pl.BlockSpec((1, 8, D), lambda i: (midx(i), 0, 0)), dirs, dirs, dirs, row(HW),
                zspec(LANE, ZB_GL), zspec(HW, ZB_MO // 4),
                zspec(D, ZB_GATE // 8), zspec(D, ZB_GATE // 8 + 1), zspec(D, ZB_GATE // 8 + 2)]
    in_specs += [full(a) for a in consts]
    args = [x, mod3, yr, bv, hm, yn, z, z, z, z, z] + consts
    if with_router:
        out_shape = [jax.ShapeDtypeStruct((ntok, D), F32), jax.ShapeDtypeStruct((ntok, D // 2), jnp.uint32)]
        out_specs = [row(D), row(D // 2)]
    else:
        out_shape = [jax.ShapeDtypeStruct((ntok, D), F32), jax.ShapeDtypeStruct((ntok, D), BF16)]
        out_specs = [row(D), row(D)]
    if with_router:
        in_specs += [full(router[0]), full(router[1])]
        args += list(router)
        out_shape.append(jax.ShapeDtypeStruct((ntok, LANE), F32))
        out_specs.append(row(LANE))
    return pl.pallas_call(
        functools.partial(_merge_body, with_router=with_router),
        out_shape=out_shape,
        grid=(ntok // tm,),
        in_specs=in_specs,
        out_specs=out_specs,
        compiler_params=_params(("parallel",)),
        name="branch_merge",
    )(*args)


def _top2_select(lg):
    lane = lax.broadcasted_iota(jnp.int32, lg.shape, 1)
    m1 = jnp.max(lg, axis=1, keepdims=True)
    i1 = jnp.min(jnp.where(lg == m1, lane, LANE), axis=1, keepdims=True)
    lg2 = jnp.where(lane == i1, -jnp.inf, lg)
    m2 = jnp.max(lg2, axis=1, keepdims=True)
    i2 = jnp.min(jnp.where(lg2 == m2, lane, LANE), axis=1, keepdims=True)
    e2 = jnp.exp(m2 - m1)
    den = 1.0 + e2
    return jnp.where(lane == 0, i1.astype(F32),
                     jnp.where(lane == 1, i2.astype(F32),
                               jnp.where(lane == 2, 1.0 / den, jnp.where(lane == 3, e2 / den, 0.0))))


def _ffn_body(h_ref, x1_ref, mod_ref, gpost_ref, wg_ref, wu_ref, wd_ref, o_ref, acc_s):
    j = pl.program_id(1)

    @pl.when(j == 0)
    def _():
        acc_s[...] = jnp.zeros_like(acc_s)

    hin = h_ref[...]
    hg = jnp.dot(hin, wg_ref[...], preferred_element_type=F32)
    hu = jnp.dot(hin, wu_ref[...], preferred_element_type=F32)
    acc_s[...] += _dot(hg * jax.nn.sigmoid(hg) * hu, wd_ref[...])

    @pl.when(j == pl.num_programs(1) - 1)
    def _():
        m = mod_ref[0]
        o_ref[...] = x1_ref[...] + m[5:6] * _rms(acc_s[...], gpost_ref[...])


def _ffn_call(hin, x1, mod3, gpost, wg, wu, wd, n_ctx, lat_t, tm=512, tf=1408):
    ntok = hin.shape[0]
    dff = wg.shape[1]
    midx = _mod_index(tm, n_ctx, lat_t)
    return pl.pallas_call(
        _ffn_body,
        out_shape=jax.ShapeDtypeStruct((ntok, D), F32),
        grid=(ntok // tm, dff // tf),
        in_specs=[pl.BlockSpec((tm, D), lambda i, j: (i, 0)),
                  pl.BlockSpec((tm, D), lambda i, j: (i, 0)),
                  pl.BlockSpec((1, 8, D), lambda i, j: (midx(i), 0, 0)),
                  pl.BlockSpec((1, D), lambda i, j: (0, 0)),
                  pl.BlockSpec((D, tf), lambda i, j: (0, j)),
                  pl.BlockSpec((D, tf), lambda i, j: (0, j)),
                  pl.BlockSpec((tf, D), lambda i, j: (j, 0))],
        out_specs=pl.BlockSpec((tm, D), lambda i, j: (i, 0)),
        scratch_shapes=[pltpu.VMEM((tm, D), F32)],
        compiler_params=_params(("parallel", "arbitrary")),
        name="dense_swiglu",
    )(hin, x1, mod3, gpost, wg, wu, wd)


MOE_TILE = 512
MOE_TF = 512
CMB_TILE = 128
SLAB = CMB_TILE + 8


def _moe_plan(sel, ntok):
    i32 = jnp.int32
    e1, e2 = sel[:, 0].astype(i32), sel[:, 1].astype(i32)
    ar = jnp.arange(ntok, dtype=i32)
    ex = jnp.arange(N_EXPERTS, dtype=i32)
    oh = ((e1[:, None] == ex) | (e2[:, None] == ex)).astype(i32)
    csum = jnp.cumsum(oh, axis=0)
    total = csum[-1]
    gsz = (total + MOE_TILE - 1) // MOE_TILE * MOE_TILE
    gend = jnp.cumsum(gsz)
    gstart = gend - gsz
    dstart = jnp.cumsum(total) - total
    key, wts = lax.sort((jnp.concatenate([e1 * ntok + ar, e2 * ntok + ar]),
                         jnp.concatenate([sel[:, 2], sel[:, 3]])), num_keys=1)
    tok_d = key % ntok
    n_pad = 2 * ntok + (N_EXPERTS + 1) * MOE_TILE
    p = jnp.arange(n_pad, dtype=i32)
    ge = jnp.minimum(jnp.sum((p[:, None] >= gend[None, :]).astype(i32), axis=1), N_EXPERTS - 1)
    pick = lambda tab: jnp.sum(jnp.where(ge[:, None] == ex, tab[None, :], 0), axis=1)
    q = p - pick(gstart)
    live = (q < pick(total)) & (p < gend[-1])
    src = jnp.clip(pick(dstart) + q, 0, 2 * ntok - 1)
    tok_p = jnp.where(live, tok_d[src], 0)
    w_p = jnp.where(live, wts[src], 0.0)
    tstart = jnp.arange(n_pad // MOE_TILE, dtype=i32) * MOE_TILE
    texp = jnp.minimum(jnp.sum((tstart[:, None] >= gend[None, :]).astype(i32), axis=1), N_EXPERTS - 1)
    tval = (tstart < gend[-1]).astype(i32)
    r0 = (csum - oh)[::CMB_TILE]
    r1 = jnp.concatenate([r0[1:], total[None]], axis=0)
    alo = (gstart[None, :] + r0).reshape(-1)
    ahi = (gstart[None, :] + r1).reshape(-1)
    return dict(tok=tok_p, w=w_p.reshape(n_pad, 1), texp=texp, tval=tval,
                a8=alo // 8, alo=alo, ahi=ahi, n_pad=n_pad)


def _moe_group_body(texp_ref, tval_ref, tok_ref, hpk_ref, ws_ref, wg_ref, wu_ref, wd_ref, ys_ref, xs_s, acc_s):
    del texp_ref
    i = pl.program_id(0)
    j = pl.program_id(1)
    half = D // 2

    @pl.when(j == 0)
    def _():
        base = i * MOE_TILE

        def gather(r, carry):
            xs_s[pl.ds(r, 1), :] = hpk_ref[pl.ds(tok_ref[base + r], 1), :]
            return carry

        lax.fori_loop(0, MOE_TILE, gather, 0, unroll=8)
        acc_s[...] = jnp.zeros_like(acc_s)

    @pl.when(tval_ref[i] > 0)
    def _():
        w = xs_s[...]
        xa = lax.bitcast_convert_type(w << 16, F32).astype(BF16)
        xb = lax.bitcast_convert_type(w & jnp.uint32(0xFFFF0000), F32).astype(BF16)
        mm = lambda u, ref, rows: jnp.dot(u, ref[0, rows, :], preferred_element_type=F32)
        hg = mm(xa, wg_ref, slice(0, half)) + mm(xb, wg_ref, slice(half, D))
        hu = mm(xa, wu_ref, slice(0, half)) + mm(xb, wu_ref, slice(half, D))
        acc_s[...] += _dot(hg * jax.nn.sigmoid(hg) * hu, wd_ref[0])

    @pl.when(j == pl.num_programs(1) - 1)
    def _():
        ys_ref[...] = acc_s[...] * ws_ref[...]


def _moe_group_call(hpk, plan, wg, wu, wd):
    n_pad = plan['n_pad']
    dff = wg.shape[2]
    grid_spec = pltpu.PrefetchScalarGridSpec(
        num_scalar_prefetch=3,
        grid=(n_pad // MOE_TILE, dff // MOE_TF),
        in_specs=[pl.BlockSpec(memory_space=pltpu.VMEM),
                  pl.BlockSpec((MOE_TILE, 1), lambda i, j, te, tv, tk: (i, 0)),
                  pl.BlockSpec((1, D, MOE_TF), lambda i, j, te, tv, tk: (te[i], 0, j)),
                  pl.BlockSpec((1, D, MOE_TF), lambda i, j, te, tv, tk: (te[i], 0, j)),
                  pl.BlockSpec((1, MOE_TF, D), lambda i, j, te, tv, tk: (te[i], j, 0))],
        out_specs=pl.BlockSpec((MOE_TILE, D), lambda i, j, te, tv, tk: (i, 0)),
        scratch_shapes=[pltpu.VMEM((MOE_TILE, D // 2), jnp.uint32), pltpu.VMEM((MOE_TILE, D), F32)])
    return pl.pallas_call(
        _moe_group_body,
        out_shape=jax.ShapeDtypeStruct((n_pad, D), F32),
        grid_spec=grid_spec,
        compiler_params=_params(("arbitrary", "arbitrary")),
        name="moe_expert_swiglu",
    )(plan['texp'], plan['tval'], plan['tok'], hpk, plan['w'], wg, wu, wd)


def _moe_combine_body(a8_ref, alo_ref, ahi_ref, *refs):
    ys_refs, tk_refs = refs[:N_EXPERTS], refs[N_EXPERTS:2 * N_EXPERTS]
    x1_ref, mod_ref, gpost_ref, o_ref = refs[2 * N_EXPERTS:]
    t = pl.program_id(0)
    tloc = t * CMB_TILE + lax.broadcasted_iota(jnp.int32, (SLAB, CMB_TILE), 1)
    srow = lax.broadcasted_iota(jnp.int32, (SLAB, 1), 0)
    f = jnp.zeros((CMB_TILE, D), F32)
    for e in range(N_EXPERTS):
        k = t * N_EXPERTS + e
        row = a8_ref[k] * 8 + srow
        tok = jnp.where((row >= alo_ref[k]) & (row < ahi_ref[k]), tk_refs[e][...], -1)
        onehot = jnp.where(tok == tloc, 1.0, 0.0).astype(BF16)
        f = f + _dot_rhs2(onehot, ys_refs[e][...], TN)
    m = mod_ref[0]
    o_ref[...] = x1_ref[...] + m[5:6] * _rms(f, gpost_ref[...])


def _moe_combine_call(ys, plan, x1, mod3, gpost, n_ctx, lat_t):
    ntok = x1.shape[0]
    midx = _mod_index(CMB_TILE, n_ctx, lat_t)

    def slab(w, e):
        return pl.BlockSpec((pl.Element(SLAB), pl.Element(w)),
                            lambda t, a8, lo, hi: (a8[t * N_EXPERTS + e] * 8, 0))

    experts = range(N_EXPERTS)
    grid_spec = pltpu.PrefetchScalarGridSpec(
        num_scalar_prefetch=3,
        grid=(ntok // CMB_TILE,),
        in_specs=[slab(D, e) for e in experts] + [slab(1, e) for e in experts] + [
            pl.BlockSpec((CMB_TILE, D), lambda t, a8, lo, hi: (t, 0)),
            pl.BlockSpec((1, 8, D), lambda t, a8, lo, hi: (midx(t), 0, 0)),
            pl.BlockSpec((1, D), lambda t, a8, lo, hi: (0, 0))],
        out_specs=pl.BlockSpec((CMB_TILE, D), lambda t, a8, lo, hi: (t, 0)))
    tok2d = plan['tok'].reshape(-1, 1)
    return pl.pallas_call(
        _moe_combine_body,
        out_shape=jax.ShapeDtypeStruct((ntok, D), F32),
        grid_spec=grid_spec,
        compiler_params=_params(("parallel",)),
        name="moe_combine",
    )(plan['a8'], plan['alo'], plan['ahi'], *([ys] * N_EXPERTS), *([tok2d] * N_EXPERTS), x1, mod3, gpost)


def _rope_tables(t):
    i = jnp.arange(M_HD)
    half, pair, f = i // 64, (i % 64) // 32, i % 32
    tt = jnp.arange(t)
    pos = jnp.stack([tt // GRID_W, tt % GRID_W], axis=-1).astype(F32)
    inv = ROPE_BASE ** (-f.astype(F32) / 32)
    ang = pos[:, half] * inv[None, :]
    sin = jnp.sin(ang)
    return (jnp.cos(ang), jnp.where(pair == 0, -sin, 0.0)[...], jnp.where(pair == 1, sin, 0.0))


def _pad_in_proj(w):
    off = {}
    o = 0
    for name, width in (('r', 512), ('k', 512), ('v', 512), ('wl', 128), ('al', 128), ('gl', 128),
                        ('mq', 512), ('mk', 512), ('mv', 512), ('mo', 512), ('mif', 16),
                        ('nq', 512), ('nk', 512), ('nv', 512), ('gate', 3072)):
        off[name] = (o, o + width)
        o += width
    col = lambda n: w[:, off[n][0]:off[n][1]]
    parts = [col(n) for n in ('r', 'k', 'v', 'mq', 'mk', 'mv', 'mo', 'nq', 'nk', 'nv', 'gate', 'wl', 'al', 'gl', 'mif')]
    parts.append(jnp.zeros((w.shape[0], LANE - 16), w.dtype))
    return jnp.concatenate(parts, axis=1).astype(BF16)


def _block_avg(width, group):
    i = jnp.arange(width) // group
    return jnp.where(i[:, None] == i[None, :], 1.0 / group, 0.0).astype(BF16)


def _zero_pad_rows(w2):
    z = jnp.zeros_like(w2[0])
    return jnp.stack([jnp.concatenate([w2[0], z], axis=0), jnp.concatenate([z, w2[1]], axis=0)])


def kernel(x_prompt, x_sample, state_rwkv, state_mlstm_C, state_mlstm_n, state_mlstm_m, cache_nat_k, cache_nat_v,
           c, c_ctx, w_mod, b_mod, g_pre_mix, g_post_mix, g_pre_ffn, g_post_ffn, w_in,
           rw_w0, rw_w2, rw_a0, rw_a2, rw_g2, rw_k_k, rw_k_a, rw_r_k, rw_ln_w, rw_ln_b,
           ml_b_i, ml_b_f, ml_norm_w, nat_rpb, w_o_rwkv, w_o_mlstm, w_o_nat, w_out,
           ff_w_gate, ff_w_up, ff_w_down, moe_w_router, moe_b_router, moe_w_gate, moe_w_up, moe_w_down):
    cb, ct = x_prompt.shape[:2]
    lb, lt = x_sample.shape[:2]
    n_ctx, n_lat = cb * ct, lb * lt
    x = jnp.concatenate([x_prompt.reshape(n_ctx, D), x_sample.reshape(n_lat, D)], axis=0)
    cvec = jnp.zeros((8, D), F32).at[0].set(c_ctx).at[1:1 + lb].set(c)
    rope = _rope_tables(lt)
    avg64, avg128 = _block_avg(HW, R_HD), _block_avg(HW, M_HD)
    lat_tb = min(lt, 512)
    states = []
    for l in range(DEPTH):
        mod = _mod_call(cvec, w_mod[l], b_mod[l])
        mod3 = jnp.pad(mod[:1 + lb].reshape(1 + lb, 6, D), ((0, 0), (0, 2), (0, 0)))
        z = _proj_call(x, mod3, g_pre_mix[l], _pad_in_proj(w_in[l]), n_ctx, lt)

        rp = dict(w0=rw_w0[l].reshape(2, 1, HW), w2=_zero_pad_rows(rw_w2[l]),
                  a0=rw_a0[l].reshape(2, 1, HW), a2=_zero_pad_rows(rw_a2[l]),
                  kk=rw_k_k[l].reshape(1, HW), ka=rw_k_a[l].reshape(1, HW), rk=rw_r_k[l].reshape(1, HW))
        yr, bv, s_rw = _rwkv_call(z, rp, None, None, row0=0, nb=cb, t=ct, tb=RW_BLOCK, has_out=True)
        yr, bv = _rwkv_call(z, rp, state_rwkv[:, l], (yr, bv), row0=n_ctx, nb=lb, t=lt, tb=RW_BLOCK,
                            has_out=False)

        gt = z[:, ZB_MIF * LANE:ZB_MIF * LANE + 16].reshape(-1, CHUNK, 16).transpose(0, 2, 1)
        hm, c_m, n_m, m_m = _mlstm_call(z, gt, None, ml_b_i[l], ml_b_f[l], None, None,
                                        row0=0, nb=cb, t=ct, tb=RW_BLOCK, has_out=True)
        init = (state_mlstm_C[:, l], state_mlstm_n[:, l][:, :, :, None, :], state_mlstm_m[:, l])
        hm, = _mlstm_call(z, gt, rope, ml_b_i[l], ml_b_f[l], init, (hm,),
                          row0=n_ctx, nb=lb, t=lt, tb=lat_tb, has_out=False)

        yn, k_c, v_c = _ctx_attn_call(z, nb=cb, t=ct)
        yn = _natten_call(z, cache_nat_k[:, l], cache_nat_v[:, l], _natten_table(nat_rpb[l]), yn,
                          row0=n_ctx, nb=lb, t=lt)
        states.append((s_rw, c_m, n_m[:, :, :, 0, :], m_m[:, :, :, 0, 0], k_c, v_c))

        mw = dict(avg64=avg64, avg128=avg128, rg2=rw_g2[l].astype(BF16), lnw=rw_ln_w[l].reshape(1, HW),
                  lnb=rw_ln_b[l].reshape(1, HW), mlw=ml_norm_w[l].reshape(1, HW),
                  wor=w_o_rwkv[l].astype(BF16), wom=w_o_mlstm[l].astype(BF16), won=w_o_nat[l].astype(BF16),
                  wout=w_out[l].astype(BF16), gpost=g_post_mix[l].reshape(1, D), gpre=g_pre_ffn[l].reshape(1, D))
        j = l // 2
        if l % 2 == 0:
            router = None
        else:
            wr = jnp.pad(moe_w_router[j], ((0, 0), (0, LANE - N_EXPERTS)))
            br = jnp.pad(moe_b_router[j], (0, LANE - N_EXPERTS), constant_values=NEG).reshape(1, LANE)
            router = (wr, br)
        merged = _merge_call(x, mod3, yr, bv, hm, yn, z, mw, n_ctx, lt, router)
        gpost = g_post_ffn[l].reshape(1, D)
        if l % 2 == 0:
            x = _ffn_call(merged[1], merged[0], mod3, gpost, ff_w_gate[j].astype(BF16),
                          ff_w_up[j].astype(BF16), ff_w_down[j].astype(BF16), n_ctx, lt)
        else:
            plan = _moe_plan(merged[2], n_ctx + n_lat)
            ys = _moe_group_call(merged[1], plan, moe_w_gate[j].astype(BF16), moe_w_up[j].astype(BF16),
                                 moe_w_down[j].astype(BF16))
            x = _moe_combine_call(ys, plan, merged[0], mod3, gpost, n_ctx, lt)

    stack = lambda k: jnp.stack([s[k] for s in states], axis=1)
    return (x[:n_ctx].reshape(cb, ct, D), x[n_ctx:].reshape(lb, lt, D),
            stack(0), stack(1), stack(2), stack(3), stack(4), stack(5))
```

```python
import functools
import math

import jax
import jax.numpy as jnp
from jax import lax
from jax.experimental import pallas as pl
from jax.experimental.pallas import tpu as pltpu

F32 = jnp.float32
BF16 = jnp.bfloat16

D = 1024
N_CTX_B, N_CTX_T = 32, 256
N_LAT_B, N_LAT_T = 2, 4096
DEPTH = 2
GRID_W = 64
R_HEADS, R_HD = 8, 64
M_HEADS, M_HD = 4, 128
N_HEADS, N_HD = 8, 64
HW = 512
WIN_ROWS, WIN_COLS = 8, 16
N_EXPERTS = 8
RWKV_GN_EPS = 64e-5
NORM_EPS = 1e-6
ROPE_BASE = 10000.0
CHUNK = 64
NEG = -1e30
LANE = 128
VMEM_LIMIT = 56 * 1024 * 1024

ZB_R, ZB_K, ZB_V = 0, 4, 8
ZB_MQ, ZB_MK, ZB_MV, ZB_MO = 12, 16, 20, 24
ZB_NQ, ZB_NK, ZB_NV = 28, 32, 36
ZB_GATE = 40
ZB_WL, ZB_AL, ZB_GL, ZB_MIF = 64, 65, 66, 67
P_PAD = 68 * LANE

NN = (((1,), (0,)), ((), ()))
NT = (((1,), (1,)), ((), ()))
TN = (((0,), (0,)), ((), ()))


def _dot(a, b, dims=NN):
    return lax.dot_general(a.astype(BF16), b.astype(BF16), dims, preferred_element_type=F32)


def _split(x):
    hi = x.astype(BF16)
    return hi, (x - hi.astype(F32)).astype(BF16)


def _dot3(a, b, dims=NN):
    ah, al = _split(a)
    bh, bl = _split(b)
    d = lambda u, w: lax.dot_general(u, w, dims, preferred_element_type=F32)
    return d(ah, bh) + (d(ah, bl) + d(al, bh))


def _dot_lhs2(a, b_exact, dims=NN):
    ah, al = _split(a)
    d = lambda u: lax.dot_general(u, b_exact, dims, preferred_element_type=F32)
    return d(ah) + d(al)


def _dot_rhs2(a_exact, b, dims=NN):
    bh, bl = _split(b)
    d = lambda w: lax.dot_general(a_exact, w, dims, preferred_element_type=F32)
    return d(bh) + d(bl)


def _log_sigmoid(x):
    return jnp.minimum(x, 0.0) - jnp.log1p(jnp.exp(-jnp.abs(x)))


def _rms(x, g):
    return x * lax.rsqrt(jnp.mean(x * x, axis=-1, keepdims=True) + NORM_EPS) * g


def _params(sem):
    return pltpu.CompilerParams(dimension_semantics=sem, vmem_limit_bytes=VMEM_LIMIT)


def _alias_prev(in_specs, args, prev, first_out=0):
    aliases = {}
    for k, a in enumerate(prev or ()):
        aliases[len(args)] = first_out + k
        in_specs.append(pl.BlockSpec(memory_space=pl.ANY))
        args.append(a)
    return aliases


def _mod_body(c_ref, w_ref, b_ref, o_ref):
    c = c_ref[...]
    o_ref[...] = _dot(c * jax.nn.sigmoid(c), w_ref[...]) + b_ref[...]


def _mod_call(cvec, w_mod, b_mod):
    tn = 1536
    return pl.pallas_call(
        _mod_body,
        out_shape=jax.ShapeDtypeStruct((8, 6 * D), F32),
        grid=(6 * D // tn,),
        in_specs=[pl.BlockSpec((8, D), lambda j: (0, 0)),
                  pl.BlockSpec((D, tn), lambda j: (0, j)),
                  pl.BlockSpec((1, tn), lambda j: (0, j))],
        out_specs=pl.BlockSpec((8, tn), lambda j: (0, j)),
        compiler_params=_params(("arbitrary",)),
        name="adaln_mod",
    )(cvec, w_mod, b_mod.reshape(1, 6 * D))


def _mod_index(tile_rows, n_ctx, lat_t):
    def f(i):
        start = i * tile_rows
        return jnp.where(start < n_ctx, 0, 1 + (start - n_ctx) // lat_t)
    return f


def _proj_body(x_ref, mod_ref, g_ref, w_ref, z_ref, hin_s):
    @pl.when(pl.program_id(1) == 0)
    def _():
        m = mod_ref[0]
        hin_s[...] = (_rms(x_ref[...], g_ref[...]) * (1.0 + m[1:2]) + m[0:1]).astype(BF16)
    z_ref[...] = jnp.dot(hin_s[...], w_ref[...], preferred_element_type=F32)


def _proj_call(x, mod3, g, w_bf, n_ctx, lat_t, tm=1024, tn=2176):
    ntok = x.shape[0]
    midx = _mod_index(tm, n_ctx, lat_t)
    return pl.pallas_call(
        _proj_body,
        out_shape=jax.ShapeDtypeStruct((ntok, P_PAD), F32),
        grid=(ntok // tm, P_PAD // tn),
        in_specs=[pl.BlockSpec((tm, D), lambda i, j: (i, 0)),
                  pl.BlockSpec((1, 8, D), lambda i, j: (midx(i), 0, 0)),
                  pl.BlockSpec((1, D), lambda i, j: (0, 0)),
                  pl.BlockSpec((D, tn), lambda i, j: (0, j))],
        out_specs=pl.BlockSpec((tm, tn), lambda i, j: (i, j)),
        scratch_shapes=[pltpu.VMEM((tm, D), BF16)],
        compiler_params=_params(("parallel", "arbitrary")),
        name="in_proj",
    )(x, mod3, g.reshape(1, D), w_bf)


RW_TILE = 2 * CHUNK
RW_BLOCK = 512


def _rwkv_body(*refs, tb_rows, nseq, has_init, has_out, n_alias):
    (r_ref, k_ref, v_ref, wl_ref, al_ref, w0_ref, w2_ref, a0_ref, a2_ref,
     kkw_ref, kaw_ref, rkw_ref, hsum_ref) = refs[:13]
    rest = list(refs[13:])
    s0_ref = rest.pop(0) if has_init else None
    del rest[:n_alias]
    y_ref, bv_ref = rest.pop(0), rest.pop(0)
    sout_ref = rest.pop(0) if has_out else None
    r2_s, m_s, g_s, st_s = rest
    C = CHUNK
    TL = RW_TILE
    nchunk = tb_rows // nseq // C
    d = pl.program_id(2)
    tb = pl.program_id(3)
    n_tb = pl.num_programs(3)

    ti = lax.broadcasted_iota(jnp.int32, (TL, TL), 0)
    si = lax.broadcasted_iota(jnp.int32, (TL, TL), 1)
    head_blk = (ti >> 6) == (si >> 6)

    @pl.when(tb == 0)
    def _():
        for q in range(nseq):
            if has_init:
                s0 = s0_ref[q, 0]
                two = jnp.concatenate([jnp.concatenate([s0[0], s0[0]], axis=1),
                                       jnp.concatenate([s0[1], s0[1]], axis=1)], axis=0)
                st_s[q] = jnp.where(head_blk, two, 0.0)
            else:
                st_s[q] = jnp.zeros((LANE, LANE), F32)

    hsum = hsum_ref[...]
    r = r_ref[...]
    kf = k_ref[...]
    v = v_ref[...]
    w_pre = w0_ref[0] + _dot(jnp.tanh(wl_ref[...]), w2_ref[0])
    lw = -math.exp(-0.5) * jax.nn.sigmoid(w_pre)
    a = jax.nn.sigmoid(a0_ref[0] + _dot(al_ref[...], a2_ref[0]))
    kd = kf * (1.0 + (a - 1.0) * kaw_ref[...])
    kkf = kf * kkw_ref[...]
    kn = kkf * lax.rsqrt(_dot_lhs2(kkf * kkf, hsum) + 1e-12)
    bd = kn * a
    bv_ref[0] = _dot_lhs2(r * kd * rkw_ref[...], hsum) * v

    dlt = (ti - si) * (1 - 2 * d)
    strict = head_blk & (dlt > 0)
    incl = head_blk & (dlt >= 0)
    cum_ones = jnp.concatenate([jnp.where(incl, 1.0, 0.0), jnp.where(head_blk, 1.0, 0.0)], axis=0).astype(BF16)
    is_diag = ti == si
    eye = jnp.where(is_diag, 1.0, 0.0)
    pair = (ti >> 1) == (si >> 1)
    sibling = [((ti >> lg) ^ (si >> lg)) == 1 for lg in range(1, 6)]
    head0 = si < R_HD

    n_tiles = tb_rows // TL
    tiles = []
    for tile in range(n_tiles):
        rows = slice(TL * tile, TL * (tile + 1))
        cs = _dot_rhs2(cum_ones, lw[rows])
        cum, tot = cs[:TL], cs[TL:]
        e_neg = jnp.exp(-cum)
        e_hat = jnp.exp(tot - cum)
        rt = r[rows] * jnp.exp(cum)
        at = -kn[rows] * jnp.exp(cum - lw[rows])
        tiles.append(dict(rows=rows, tot=tot, rt=rt, at=at, bh=bd[rows] * e_hat, kh=kd[rows] * e_hat,
                          rhs=jnp.concatenate([bd[rows] * e_neg, kd[rows] * e_neg], axis=0).astype(BF16),
                          at_bf=at.astype(BF16), v_bf=v[rows].astype(BF16)))
    chains = [(tl, h) for tl in tiles for h in range(2)]
    a_all = []
    for tl, h in chains:
        hm = head0 if h == 0 else ~head0
        lhs = jnp.concatenate([jnp.where(hm, tl['at'], 0.0), jnp.where(hm, tl['rt'], 0.0)], axis=0)
        a_all.append(_dot(lhs, tl['rhs'], NT))
    n_mat = [jnp.where(strict, a[:TL, :TL], 0.0) for a in a_all]
    a_ak = [jnp.where(strict, a[:TL, TL:], 0.0).astype(BF16) for a in a_all]
    a_rb = [jnp.where(incl, a[TL:, :TL], 0.0).astype(BF16) for a in a_all]
    a_rk = [jnp.where(incl, a[TL:, TL:], 0.0).astype(BF16) for a in a_all]
    t_inv = [eye + jnp.where(pair, n, 0.0) for n in n_mat]
    for sib in sibling:
        x_mat = [_dot(jnp.where(sib, n, 0.0), t) for n, t in zip(n_mat, t_inv)]
        t_inv = [t + _dot(t, x) for t, x in zip(t_inv, x_mat)]
    p_h = [_dot(t, tl['at_bf']) for t, (tl, _) in zip(t_inv, chains)]
    av = [_dot(a, tl['v_bf']) for a, (tl, _) in zip(a_ak, chains)]
    q_h = [_dot(t, x) for t, x in zip(t_inv, av)]
    r2_h = [_dot(a, p) for a, p in zip(a_rb, p_h)]
    y0_h = [_dot(a, q) + _dot(k_, tl['v_bf']) for a, q, k_, (tl, _) in zip(a_rb, q_h, a_rk, chains)]
    for i, tl in enumerate(tiles):
        rows = tl['rows']
        pick = lambda u: jnp.where(head0, u[2 * i], u[2 * i + 1])
        p_m, q_m = pick(p_h), pick(q_h)
        r2_s[rows, :] = tl['rt'] + pick(r2_h)
        y_ref[0, rows, :] = pick(y0_h)
        for c in range(TL // C):
            cr = slice(C * c, C * (c + 1))
            decay = jnp.where(is_diag, jnp.exp(tl['tot'][C * c:C * c + 1]), 0.0)
            m_s[i * (TL // C) + c] = decay + jnp.where(head_blk, _dot(p_m[cr], tl['bh'][cr], TN), 0.0)
            g_s[i * (TL // C) + c] = jnp.where(
                head_blk, _dot(q_m[cr], tl['bh'][cr], TN) + _dot(v[rows][cr], tl['kh'][cr], TN), 0.0)

    def step(i, states):
        ci = jnp.where(d == 0, i, nchunk - 1 - i)
        new = []
        for q, s in enumerate(states):
            cq = q * nchunk + ci
            rows = pl.ds(pl.multiple_of(cq * C, C), C)
            y_ref[0, rows, :] = y_ref[0, rows, :] + _dot(r2_s[rows, :], s, NT)
            new.append(_dot3(s, m_s[cq]) + g_s[cq])
        return tuple(new)

    s_fin = lax.fori_loop(0, nchunk, step, tuple(st_s[q] for q in range(nseq)))
    for q in range(nseq):
        st_s[q] = s_fin[q]

    if has_out:
        @pl.when(tb == n_tb - 1)
        def _():
            for q in range(nseq):
                sout_ref[q, 0, 0, 0] = s_fin[q][:R_HD, :R_HD]
                sout_ref[q, 0, 0, 1] = s_fin[q][R_HD:, R_HD:]


def _rwkv_call(z, P, s0, prev, *, row0, nb, t, tb, layer=None, prev_state=None):
    nseq = max(tb // t, 1)
    n_tb = max(t // tb, 1)
    blk0 = row0 // tb
    has_init = s0 is not None

    def rowblk(b, d, i):
        return blk0 + b * n_tb + jnp.where(d == 0, i, n_tb - 1 - i)

    def zspec(cb):
        return pl.BlockSpec((tb, LANE), lambda b, hp, d, i: (rowblk(b, d, i), cb(hp)))

    vec = lambda: pl.BlockSpec((1, LANE), lambda b, hp, d, i: (0, hp))
    dvec = lambda: pl.BlockSpec((1, 1, LANE), lambda b, hp, d, i: (d, 0, hp))
    dmat = lambda: pl.BlockSpec((1, LANE, LANE), lambda b, hp, d, i: (d, 0, hp))
    st_spec = pl.BlockSpec((nseq, 1, 2, R_HD, R_HD), lambda b, hp, d, i: (b, d, hp, 0, 0))
    in_specs = [zspec(lambda hp: ZB_R + hp), zspec(lambda hp: ZB_K + hp), zspec(lambda hp: ZB_V + hp),
                zspec(lambda hp: ZB_WL), zspec(lambda hp: ZB_AL),
                dvec(), dmat(), dvec(), dmat(), vec(), vec(), vec(),
                pl.BlockSpec((LANE, LANE), lambda b, hp, d, i: (0, 0))]
    args = [z, z, z, z, z, P['w0'], P['w2'], P['a0'], P['a2'], P['kk'], P['ka'], P['rk'],
            _block_avg(LANE, R_HD) * R_HD]
    if has_init:
        in_specs.append(st_spec)
        args.append(s0)
    has_out = layer is not None
    aliases = _alias_prev(in_specs, args, prev)
    aliases.update(_alias_prev(in_specs, args, prev_state, first_out=2))
    yspec = pl.BlockSpec((1, tb, LANE), lambda b, hp, d, i: (d, rowblk(b, d, i), hp))
    out_shape = [jax.ShapeDtypeStruct((2, z.shape[0], HW), F32)] * 2
    out_specs = [yspec, yspec]
    if has_out:
        out_shape.append(jax.ShapeDtypeStruct((nb, DEPTH, 2, R_HEADS, R_HD, R_HD), F32))
        out_specs.append(pl.BlockSpec((nseq, 1, 1, 2, R_HD, R_HD), lambda b, hp, d, i: (b, layer, d, hp, 0, 0)))
    return pl.pallas_call(
        functools.partial(_rwkv_body, tb_rows=tb, nseq=nseq, has_init=has_init, has_out=has_out,
                          n_alias=len(aliases)),
        out_shape=out_shape,
        grid=(nb // nseq, R_HEADS // 2, 2, n_tb),
        in_specs=in_specs,
        out_specs=out_specs,
        input_output_aliases=aliases,
        scratch_shapes=[pltpu.VMEM((tb, LANE), F32), pltpu.VMEM((tb // CHUNK, LANE, LANE), F32),
                        pltpu.VMEM((tb // CHUNK, LANE, LANE), F32), pltpu.VMEM((nseq, LANE, LANE), F32)],
        compiler_params=_params(("parallel", "parallel", "arbitrary", "arbitrary")),
        name="rwkv7_chunked",
    )(*args)


def _mlstm_body(*refs, tb_rows, nseq, use_rope, has_init, has_out, n_alias):
    refs = list(refs)
    q_ref, k_ref, v_ref, g_ref, gt_ref = [refs.pop(0) for _ in range(5)]
    cos_ref, sa_ref, sb_ref = [refs.pop(0) for _ in range(3)] if use_rope else (None,) * 3
    bi_ref, bf_ref = refs.pop(0), refs.pop(0)
    c0_ref, n0_ref, m0_ref = [refs.pop(0) for _ in range(3)] if has_init else (None,) * 3
    del refs[:n_alias]
    h_ref = refs.pop(0)
    cout_ref, nout_ref, mout_ref = [refs.pop(0) for _ in range(3)] if has_out else (None,) * 3
    sc_s, kv_s, nk_s, cprev_s, nprev_s, c_s, n_s, m_s = refs
    L = CHUNK
    nck = tb_rows // L
    ncs = nck // nseq
    b = pl.program_id(0)
    h = pl.program_id(1)
    d = pl.program_id(2)
    tb = pl.program_id(3)
    n_tb = pl.num_programs(3)
    bi = bi_ref[d, h]
    bf = bf_ref[d, h]
    ji = d * (2 * M_HEADS) + h
    jf = ji + M_HEADS

    @pl.when(tb == 0)
    def _():
        for q in range(nseq):
            if has_init:
                c_s[q] = c0_ref[q, 0, 0]
                n_s[q] = n0_ref[q, 0, 0]
                m_s[q] = jnp.full((1, M_HD), m0_ref[b * nseq + q, d, h], F32)
            else:
                c_s[q] = jnp.zeros((M_HD, M_HD), F32)
                n_s[q] = jnp.zeros((1, M_HD), F32)
                m_s[q] = jnp.zeros((1, M_HD), F32)

    g = g_ref[...]
    lane = lax.broadcasted_iota(jnp.int32, g.shape, 1)
    icol_all = jnp.sum(jnp.where(lane == ji, g, 0.0), axis=1, keepdims=True) + bi
    fcol_all = _log_sigmoid(jnp.sum(jnp.where(lane == jf, g, 0.0), axis=1, keepdims=True) + bf)

    ti = lax.broadcasted_iota(jnp.int32, (L, L), 0)
    si = lax.broadcasted_iota(jnp.int32, (L, L), 1)
    sgn = 1 - 2 * d
    incl = (ti - si) * sgn >= 0
    incl_t = (si - ti) * sgn >= 0
    row128 = lambda x: jnp.broadcast_to(x, (1, M_HD))

    ch = []
    for c in range(nck):
        rows = slice(L * c, L * (c + 1))
        irow = gt_ref[c, pl.ds(ji, 1), :] + bi
        frow = _log_sigmoid(gt_ref[c, pl.ds(jf, 1), :] + bf)
        bsum_col = jnp.sum(jnp.where(incl, frow, 0.0), axis=1, keepdims=True)
        bsum_row = jnp.sum(jnp.where(incl_t, fcol_all[rows], 0.0), axis=0, keepdims=True)
        blast = jnp.sum(frow, axis=1, keepdims=True)
        wlog = blast - bsum_col + icol_all[rows]
        dmat = jnp.where(incl, bsum_col - bsum_row + irow, NEG)
        sc_s[c, 0:1, :] = row128(blast)
        sc_s[c, 1:2, :] = row128(jnp.max(wlog, axis=0, keepdims=True))
        ch.append(dict(rows=rows, bsum_col=bsum_col, blast=blast, wlog=wlog, dmat=dmat,
                       dmax=jnp.max(dmat, axis=1, keepdims=True)))

    def m_step(i, ms):
        ci = jnp.where(d == 0, i, ncs - 1 - i)
        new = []
        for q, m in enumerate(ms):
            cq = q * ncs + ci
            m_new = jnp.maximum(sc_s[cq, 0:1, :] + m, sc_s[cq, 1:2, :])
            sc_s[cq, 2:3, :] = m
            sc_s[cq, 3:4, :] = m_new
            new.append(m_new)
        return tuple(new)

    m_fin = lax.fori_loop(0, ncs, m_step, tuple(m_s[q] for q in range(nseq)))

    for c, cd in enumerate(ch):
        rows = cd['rows']
        qc, kc = q_ref[rows, :], k_ref[rows, :]
        if use_rope:
            cos, sa, sb = cos_ref[rows, :], sa_ref[rows, :], sb_ref[rows, :]
            qc = qc * cos + pltpu.roll(qc, 96, 1) * sa + pltpu.roll(qc, 32, 1) * sb
            kc = kc * cos + pltpu.roll(kc, 96, 1) * sa + pltpu.roll(kc, 32, 1) * sb
        cd['q'] = qc
        cd['k'] = kc * (M_HD ** -0.5)
        cd['v'] = v_ref[rows, :]
        cd['m_prev'] = sc_s[c, 2:3, 0:1]
        cd['m_new'] = sc_s[c, 3:4, 0:1]
    for cd in ch:
        cd['qk'] = _dot(cd['q'], cd['k'], NT)
    for c, cd in enumerate(ch):
        kw = cd['k'] * jnp.exp(cd['wlog'] - cd['m_new'])
        kv_s[c] = _dot3(kw, cd['v'], TN)
        nk_s[c] = jnp.sum(kw, axis=0, keepdims=True)
        sc_s[c, 4:5, :] = row128(jnp.exp(cd['blast'] + cd['m_prev'] - cd['m_new']))
    for cd in ch:
        inter = cd['bsum_col'] + cd['m_prev']
        cd['mt'] = jnp.maximum(inter, cd['dmax'])
        cd['iw'] = jnp.exp(inter - cd['mt'])
        sm = cd['qk'] * jnp.exp(cd['dmat'] - cd['mt'])
        cd['ssum'] = jnp.sum(sm, axis=1, keepdims=True)
        cd['sv'] = _dot(sm, cd['v'])

    def c_step(i, carry):
        ci = jnp.where(d == 0, i, ncs - 1 - i)
        new = []
        for q, (c_st, n_st) in enumerate(carry):
            cq = q * ncs + ci
            dec = sc_s[cq, 4:5, 0:1]
            cprev_s[cq] = c_st
            nprev_s[cq] = n_st
            new.append((dec * c_st + kv_s[cq], dec * n_st + nk_s[cq]))
        return tuple(new)

    st_fin = lax.fori_loop(0, ncs, c_step, tuple((c_s[q], n_s[q]) for q in range(nseq)))
    for q in range(nseq):
        c_s[q], n_s[q] = st_fin[q]
        m_s[q] = m_fin[q]

    qc_prev = [_dot(cd['q'], cprev_s[c]) for c, cd in enumerate(ch)]
    for c, cd in enumerate(ch):
        num = cd['iw'] * qc_prev[c] + cd['sv']
        den = cd['iw'] * jnp.sum(cd['q'] * nprev_s[c], axis=1, keepdims=True) + cd['ssum']
        h_ref[0, cd['rows'], :] = num / jnp.maximum(jnp.abs(den), jnp.exp(-cd['mt']))

    if has_out:
        @pl.when(tb == n_tb - 1)
        def _():
            for q in range(nseq):
                cout_ref[q, 0, 0, 0] = st_fin[q][0]
                nout_ref[q, 0, 0, 0] = st_fin[q][1]
                mout_ref[q, 0, 0, 0] = m_fin[q]


def _mlstm_call(z, gt, rope, b_i, b_f, init, prev, *, row0, nb, t, tb, layer=None, prev_state=None):
    nseq = max(tb // t, 1)
    n_tb = max(t // tb, 1)
    blk0 = row0 // tb
    cpb = tb // CHUNK
    use_rope = rope is not None
    has_init = init is not None

    def rowblk(b, d, i):
        return blk0 + b * n_tb + jnp.where(d == 0, i, n_tb - 1 - i)

    def zspec(cb):
        return pl.BlockSpec((tb, LANE), lambda b, h, d, i: (rowblk(b, d, i), cb(h)))

    smem = pl.BlockSpec(memory_space=pltpu.SMEM)
    in_specs = [zspec(lambda h: ZB_MQ + h), zspec(lambda h: ZB_MK + h), zspec(lambda h: ZB_MV + h),
                zspec(lambda h: ZB_MIF),
                pl.BlockSpec((cpb, 16, CHUNK), lambda b, h, d, i: (rowblk(b, d, i), 0, 0))]
    args = [z, z, z, z, gt]
    if use_rope:
        tspec = pl.BlockSpec((tb, LANE), lambda b, h, d, i: (jnp.where(d == 0, i, n_tb - 1 - i), 0))
        in_specs += [tspec] * 3
        args += list(rope)
    in_specs += [smem, smem]
    args += [b_i, b_f]
    cspec = pl.BlockSpec((nseq, 1, 1, M_HD, M_HD), lambda b, h, d, i: (b, d, h, 0, 0))
    nspec = pl.BlockSpec((nseq, 1, 1, 1, M_HD), lambda b, h, d, i: (b, d, h, 0, 0))
    if has_init:
        in_specs += [cspec, nspec, smem]
        args += list(init)
    has_out = layer is not None
    aliases = _alias_prev(in_specs, args, prev)
    aliases.update(_alias_prev(in_specs, args, prev_state, first_out=1))
    out_shape = [jax.ShapeDtypeStruct((2, z.shape[0], HW), F32)]
    out_specs = [pl.BlockSpec((1, tb, LANE), lambda b, h, d, i: (d, rowblk(b, d, i), h))]
    if has_out:
        vec_shape = jax.ShapeDtypeStruct((nb, DEPTH, 2, M_HEADS, 1, M_HD), F32)
        out_shape += [jax.ShapeDtypeStruct((nb, DEPTH, 2, M_HEADS, M_HD, M_HD), F32), vec_shape, vec_shape]
        vec_spec = pl.BlockSpec((nseq, 1, 1, 1, 1, M_HD), lambda b, h, d, i: (b, layer, d, h, 0, 0))
        out_specs += [pl.BlockSpec((nseq, 1, 1, 1, M_HD, M_HD), lambda b, h, d, i: (b, layer, d, h, 0, 0)),
                      vec_spec, vec_spec]
    return pl.pallas_call(
        functools.partial(_mlstm_body, tb_rows=tb, nseq=nseq, use_rope=use_rope, has_init=has_init,
                          has_out=has_out, n_alias=len(aliases)),
        out_shape=out_shape,
        grid=(nb // nseq, M_HEADS, 2, n_tb),
        in_specs=in_specs,
        out_specs=out_specs,
        input_output_aliases=aliases,
        scratch_shapes=[pltpu.VMEM((cpb, 8, M_HD), F32),
                        pltpu.VMEM((cpb, M_HD, M_HD), F32), pltpu.VMEM((cpb, 1, M_HD), F32),
                        pltpu.VMEM((cpb, M_HD, M_HD), F32), pltpu.VMEM((cpb, 1, M_HD), F32),
                        pltpu.VMEM((nseq, M_HD, M_HD), F32), pltpu.VMEM((nseq, 1, M_HD), F32),
                        pltpu.VMEM((nseq, 1, M_HD), F32)],
        compiler_params=_params(("parallel", "parallel", "arbitrary", "arbitrary")),
        name="mlstm_chunkwise",
    )(*args)


def _ctx_attn_body(q_ref, k_ref, v_ref, *refs):
    y_ref, ko_ref, vo_ref = refs[-3:]
    q = q_ref[...] * (N_HD ** -0.5)
    k = k_ref[...]
    v = v_ref[...]
    for h in range(2):
        sl = slice(N_HD * h, N_HD * (h + 1))
        kh, vh = k[:, sl], v[:, sl]
        ko_ref[0, 0, h] = kh
        vo_ref[0, 0, h] = vh
        s = _dot(q[:, sl], kh, NT)
        p = jnp.exp(s - jnp.max(s, axis=-1, keepdims=True))
        y_ref[:, sl] = _dot(p, vh) / jnp.sum(p, axis=-1, keepdims=True)


def _ctx_attn_call(z, prev_kv, *, nb, t, layer):
    zspec = lambda cb: pl.BlockSpec((t, LANE), lambda b, hp: (b, cb + hp))
    kv_spec = pl.BlockSpec((1, 1, 2, t, N_HD), lambda b, hp: (b, layer, hp, 0, 0))
    kv_shape = jax.ShapeDtypeStruct((nb, DEPTH, N_HEADS, t, N_HD), F32)
    in_specs = [zspec(ZB_NQ), zspec(ZB_NK), zspec(ZB_NV)]
    args = [z, z, z]
    aliases = _alias_prev(in_specs, args, prev_kv, first_out=1)
    return pl.pallas_call(
        _ctx_attn_body,
        out_shape=[jax.ShapeDtypeStruct((z.shape[0], HW), F32), kv_shape, kv_shape],
        grid=(nb, N_HEADS // 2),
        in_specs=in_specs,
        out_specs=[pl.BlockSpec((t, LANE), lambda b, hp: (b, hp)), kv_spec, kv_spec],
        input_output_aliases=aliases,
        compiler_params=_params(("parallel", "parallel")),
        name="context_attention",
    )(*args)


def _natten_body(q_ref, k_ref, v_ref, kc_ref, vc_ref, tab_ref, prev_ref, y_ref, *, rows_per_step, rows_n):
    del prev_ref
    rb = pl.program_id(2)
    n_loc = WIN_ROWS * GRID_W

    unroll = 4

    def rows(it, carry):
        work = []
        for u in range(unroll):
            i = it * unroll + u
            r = rb * rows_per_step + i
            rstart = jnp.clip(r - WIN_ROWS // 2, 0, rows_n - WIN_ROWS)
            qrows = pl.ds(pl.multiple_of(i * GRID_W, GRID_W), GRID_W)
            krows = pl.ds(pl.multiple_of(rstart * GRID_W, GRID_W), n_loc)
            work += [(qrows, krows, r - rstart, h, slice(N_HD * h, N_HD * (h + 1))) for h in range(2)]
        q = [q_ref[qr, sl] * (N_HD ** -0.5) for qr, _, _, _, sl in work]
        s_loc = [_dot(q_, k_ref[kr, sl], NT) + tab_ref[h, var] for q_, (_, kr, var, h, sl) in zip(q, work)]
        s_ctx = [_dot(q_, kc_ref[0, h], NT) for q_, (_, _, _, h, _) in zip(q, work)]
        m = [jnp.maximum(jnp.max(a, axis=-1, keepdims=True), jnp.max(c, axis=-1, keepdims=True))
             for a, c in zip(s_loc, s_ctx)]
        p_loc = [jnp.exp(a - m_) for a, m_ in zip(s_loc, m)]
        p_ctx = [jnp.exp(c - m_) for c, m_ in zip(s_ctx, m)]
        o = [_dot(a, v_ref[kr, sl]) + _dot(c, vc_ref[0, h])
             for a, c, (_, kr, _, h, sl) in zip(p_loc, p_ctx, work)]
        for o_, a, c, (qr, _, _, _, sl) in zip(o, p_loc, p_ctx, work):
            y_ref[qr, sl] = o_ / (jnp.sum(a, axis=-1, keepdims=True) + jnp.sum(c, axis=-1, keepdims=True))
        return carry

    lax.fori_loop(0, rows_per_step // unroll, rows, 0)


def _natten_call(z, k_ctx, v_ctx, tab, prev, *, row0, nb, t, rows_per_step=8):
    rows_n = t // GRID_W
    tq = rows_per_step * GRID_W
    qblk0 = row0 // tq
    sblk0 = row0 // t
    n_rb = rows_n // rows_per_step
    past = k_ctx.shape[2]
    kv_spec = lambda cb: pl.BlockSpec((t, LANE), lambda b, hp, rb: (sblk0 + b, cb + hp))
    cache_spec = pl.BlockSpec((1, 2, past, N_HD), lambda b, hp, rb: (b, hp, 0, 0))
    return pl.pallas_call(
        functools.partial(_natten_body, rows_per_step=rows_per_step, rows_n=rows_n),
        out_shape=jax.ShapeDtypeStruct((z.shape[0], HW), F32),
        grid=(nb, N_HEADS // 2, n_rb),
        in_specs=[pl.BlockSpec((tq, LANE), lambda b, hp, rb: (qblk0 + b * n_rb + rb, ZB_NQ + hp)),
                  kv_spec(ZB_NK), kv_spec(ZB_NV), cache_spec, cache_spec,
                  pl.BlockSpec((2, WIN_ROWS, GRID_W, WIN_ROWS * GRID_W), lambda b, hp, rb: (hp, 0, 0, 0)),
                  pl.BlockSpec(memory_space=pl.ANY)],
        out_specs=pl.BlockSpec((tq, LANE), lambda b, hp, rb: (qblk0 + b * n_rb + rb, hp)),
        input_output_aliases={6: 0},
        compiler_params=_params(("parallel", "parallel", "arbitrary")),
        name="neighbourhood_attention",
    )(z, z, z, k_ctx, v_ctx, tab, prev)


def _natten_table(rpb):
    cq = jnp.arange(GRID_W)[:, None]
    ck = jnp.arange(GRID_W)[None, :]
    dc = jnp.clip(ck - cq, -(WIN_COLS - 1), WIN_COLS - 1) + (WIN_COLS - 1)
    cstart = jnp.clip(cq - WIN_COLS // 2, 0, GRID_W - WIN_COLS)
    in_band = (ck >= cstart) & (ck < cstart + WIN_COLS)
    onehot = ((dc[None] == jnp.arange(2 * WIN_COLS - 1)[:, None, None]) & in_band[None]).astype(F32)
    cols = jnp.einsum('hrd,dqk->hrqk', rpb, onehot, precision=lax.Precision.HIGHEST)
    cols = cols + jnp.where(in_band, 0.0, NEG)
    tab = jnp.stack([cols[:, WIN_ROWS - 1 - var:2 * WIN_ROWS - 1 - var] for var in range(WIN_ROWS)], axis=1)
    return tab.transpose(0, 1, 3, 2, 4).reshape(rpb.shape[0], WIN_ROWS, GRID_W, WIN_ROWS * GRID_W)


def _group_norm(y, avg_bf, eps):
    mu = _dot_lhs2(y, avg_bf)
    yc = y - mu
    var = _dot_lhs2(yc * yc, avg_bf)
    return yc * lax.rsqrt(var + eps)


def _merge_body(*refs, with_router):
    refs = list(refs)
    (x_ref, mod_ref, yr_ref, bv_ref, hm_ref, yn_ref, gl_ref, mo_ref, g0_ref, g1_ref, g2_ref,
     avg64_ref, avg128_ref, rg2_ref, lnw_ref, lnb_ref, mlw_ref, wor_ref, wom_ref, won_ref, wout_ref,
     gpost_ref, gpre_ref) = [refs.pop(0) for _ in range(23)]
    wr_ref, br_ref = (refs.pop(0), refs.pop(0)) if with_router else (None, None)
    x1_ref, hin_ref = refs.pop(0), refs.pop(0)
    sel_ref = refs.pop(0) if with_router else None
    m = mod_ref[0]
    yr = yr_ref[0] + yr_ref[1]
    yn_r = _group_norm(yr, avg64_ref[...], RWKV_GN_EPS) * lnw_ref[...] + lnb_ref[...]
    g = _dot(jax.nn.sigmoid(gl_ref[...]), rg2_ref[...])
    out_r = (yn_r + bv_ref[0] + bv_ref[1]) * g
    hn = _group_norm(hm_ref[0] + hm_ref[1], avg128_ref[...], NORM_EPS) * mlw_ref[...]
    out_m = jax.nn.sigmoid(mo_ref[...]) * hn
    merged = (jax.nn.sigmoid(g0_ref[...]) * _dot(out_r, wor_ref[...])
              + jax.nn.sigmoid(g1_ref[...]) * _dot(out_m, wom_ref[...])
              + jax.nn.sigmoid(g2_ref[...]) * _dot(yn_ref[...], won_ref[...]))
    o = _dot(merged, wout_ref[...])
    x1 = x_ref[...] + m[2:3] * _rms(o, gpost_ref[...])
    x1_ref[...] = x1
    hin = _rms(x1, gpre_ref[...]) * (1.0 + m[4:5]) + m[3:4]
    if with_router:
        sel_ref[...] = _top2_select(_dot3(hin, wr_ref[...]) + br_ref[...])
        hin_ref[0] = hin[:, :D // 2]
        hin_ref[1] = hin[:, D // 2:]
    else:
        hin_ref[...] = hin.astype(BF16)


def _merge_call(x, mod3, yr, bv, hm, yn, z, W, n_ctx, lat_t, router, tm=256):
    ntok = x.shape[0]
    midx = _mod_index(tm, n_ctx, lat_t)
    with_router = router is not None
    row = lambda w: pl.BlockSpec((tm, w), lambda i: (i, 0))
    dirs = pl.BlockSpec((2, tm, HW), lambda i: (0, i, 0))
    zspec = lambda w, cb: pl.BlockSpec((tm, w), lambda i: (i, cb))
    full = lambda a: pl.BlockSpec(a.shape, lambda i: (0,) * a.ndim)
    consts = [W['avg64'], W['avg128'], W['rg2'], W['lnw'], W['lnb'], W['mlw'], W['wor'], W['wom'], W['won'],
              W['wout'], W['gpost'], W['gpre']]
    in_specs = [row(D), pl.BlockSpec((1, 8, D), lambda i: (midx(i), 0, 0)), dirs, dirs, dirs, row(HW),
                zspec(LANE, ZB_GL), zspec(HW, ZB_MO // 4),
                zspec(D, ZB_GATE // 8), zspec(D, ZB_GATE // 8 + 1), zspec(D, ZB_GATE // 8 + 2)]
    in_specs += [full(a) for a in consts]
    args = [x, mod3, yr, bv, hm, yn, z, z, z, z, z] + consts
    if with_router:
        out_shape = [jax.ShapeDtypeStruct((ntok, D), F32), jax.ShapeDtypeStruct((2, ntok, D // 2), F32)]
        out_specs = [row(D), pl.BlockSpec((2, tm, D // 2), lambda i: (0, i, 0))]
    else:
        out_shape = [jax.ShapeDtypeStruct((ntok, D), F32), jax.ShapeDtypeStruct((ntok, D), BF16)]
        out_specs = [row(D), row(D)]
    if with_router:
        in_specs += [full(router[0]), full(router[1])]
        args += list(router)
        out_shape.append(jax.ShapeDtypeStruct((ntok, LANE), F32))
        out_specs.append(row(LANE))
    return pl.pallas_call(
        functools.partial(_merge_body, with_router=with_router),
        out_shape=out_shape,
        grid=(ntok // tm,),
        in_specs=in_specs,
        out_specs=out_specs,
        compiler_params=_params(("parallel",)),
        name="branch_merge",
    )(*args)


def _top2_select(lg):
    lane = lax.broadcasted_iota(jnp.int32, lg.shape, 1)
    m1 = jnp.max(lg, axis=1, keepdims=True)
    i1 = jnp.min(jnp.where(lg == m1, lane, LANE), axis=1, keepdims=True)
    lg2 = jnp.where(lane == i1, -jnp.inf, lg)
    m2 = jnp.max(lg2, axis=1, keepdims=True)
    i2 = jnp.min(jnp.where(lg2 == m2, lane, LANE), axis=1, keepdims=True)
    e2 = jnp.exp(m2 - m1)
    den = 1.0 + e2
    return jnp.where(lane == 0, i1.astype(F32),
                     jnp.where(lane == 1, i2.astype(F32),
                               jnp.where(lane == 2, 1.0 / den, jnp.where(lane == 3, e2 / den, 0.0))))


def _ffn_body(h_ref, x1_ref, mod_ref, gpost_ref, wg_ref, wu_ref, wd_ref, o_ref, acc_s):
    j = pl.program_id(1)

    @pl.when(j == 0)
    def _():
        acc_s[...] = jnp.zeros_like(acc_s)

    hin = h_ref[...]
    hg = jnp.dot(hin, wg_ref[...], preferred_element_type=F32)
    hu = jnp.dot(hin, wu_ref[...], preferred_element_type=F32)
    acc_s[...] += _dot(hg * jax.nn.sigmoid(hg) * hu, wd_ref[...])

    @pl.when(j == pl.num_programs(1) - 1)
    def _():
        m = mod_ref[0]
        o_ref[...] = x1_ref[...] + m[5:6] * _rms(acc_s[...], gpost_ref[...])


def _ffn_call(hin, x1, mod3, gpost, wg, wu, wd, n_ctx, lat_t, tm=512, tf=1408):
    ntok = hin.shape[0]
    dff = wg.shape[1]
    midx = _mod_index(tm, n_ctx, lat_t)
    return pl.pallas_call(
        _ffn_body,
        out_shape=jax.ShapeDtypeStruct((ntok, D), F32),
        grid=(ntok // tm, dff // tf),
        in_specs=[pl.BlockSpec((tm, D), lambda i, j: (i, 0)),
                  pl.BlockSpec((tm, D), lambda i, j: (i, 0)),
                  pl.BlockSpec((1, 8, D), lambda i, j: (midx(i), 0, 0)),
                  pl.BlockSpec((1, D), lambda i, j: (0, 0)),
                  pl.BlockSpec((D, tf), lambda i, j: (0, j)),
                  pl.BlockSpec((D, tf), lambda i, j: (0, j)),
                  pl.BlockSpec((tf, D), lambda i, j: (j, 0))],
        out_specs=pl.BlockSpec((tm, D), lambda i, j: (i, 0)),
        scratch_shapes=[pltpu.VMEM((tm, D), F32)],
        compiler_params=_params(("parallel", "arbitrary")),
        name="dense_swiglu",
    )(hin, x1, mod3, gpost, wg, wu, wd)


MOE_TILE = 512
MOE_TF = 896
CMB_TILE = 128
SLAB = CMB_TILE + 8


def _moe_plan(sel, ntok):
    i32 = jnp.int32
    e1, e2 = sel[:, 0].astype(i32), sel[:, 1].astype(i32)
    ar = jnp.arange(ntok, dtype=i32)
    ex = jnp.arange(N_EXPERTS, dtype=i32)
    oh = ((e1[:, None] == ex) | (e2[:, None] == ex)).astype(i32)
    csum = jnp.cumsum(oh, axis=0)
    total = csum[-1]
    gsz = (total + MOE_TILE - 1) // MOE_TILE * MOE_TILE
    gend = jnp.cumsum(gsz)
    gstart = gend - gsz
    dstart = jnp.cumsum(total) - total
    key, wts = lax.sort((jnp.concatenate([e1 * ntok + ar, e2 * ntok + ar]),
                         jnp.concatenate([sel[:, 2], sel[:, 3]])), num_keys=1)
    tok_d = key % ntok
    n_pad = 2 * ntok + (N_EXPERTS + 1) * MOE_TILE
    p = jnp.arange(n_pad, dtype=i32)
    ge = jnp.minimum(jnp.sum((p[:, None] >= gend[None, :]).astype(i32), axis=1), N_EXPERTS - 1)
    pick = lambda tab: jnp.sum(jnp.where(ge[:, None] == ex, tab[None, :], 0), axis=1)
    q = p - pick(gstart)
    live = (q < pick(total)) & (p < gend[-1])
    src = jnp.clip(pick(dstart) + q, 0, 2 * ntok - 1)
    tok_p = jnp.where(live, tok_d[src], 0)
    w_p = jnp.where(live, wts[src], 0.0)
    tstart = jnp.arange(n_pad // MOE_TILE, dtype=i32) * MOE_TILE
    texp = jnp.minimum(jnp.sum((tstart[:, None] >= gend[None, :]).astype(i32), axis=1), N_EXPERTS - 1)
    tval = (tstart < gend[-1]).astype(i32)
    r0 = (csum - oh)[::CMB_TILE]
    r1 = jnp.concatenate([r0[1:], total[None]], axis=0)
    alo = (gstart[None, :] + r0).reshape(-1)
    ahi = (gstart[None, :] + r1).reshape(-1)
    return dict(tok=tok_p, w=w_p.reshape(n_pad, 1), texp=texp, tval=tval,
                a8=alo // 8, alo=alo, ahi=ahi, n_pad=n_pad)


def _moe_gather_body(tok_ref, src_ref, o_ref, buf_s):
    base = pl.program_id(0) * MOE_TILE

    def gather(r, carry):
        buf_s[pl.ds(r, 1), :] = src_ref[pl.ds(tok_ref[base + r], 1), :]
        return carry

    lax.fori_loop(0, MOE_TILE, gather, 0, unroll=8)
    o_ref[...] = buf_s[...].astype(BF16)


def _moe_gather_call(src, plan):
    n_pad = plan['n_pad']
    width = src.shape[1]
    grid_spec = pltpu.PrefetchScalarGridSpec(
        num_scalar_prefetch=1,
        grid=(n_pad // MOE_TILE,),
        in_specs=[pl.BlockSpec(memory_space=pltpu.VMEM)],
        out_specs=pl.BlockSpec((MOE_TILE, width), lambda i, tk: (i, 0)),
        scratch_shapes=[pltpu.VMEM((MOE_TILE, width), F32)])
    return pl.pallas_call(
        _moe_gather_body,
        out_shape=jax.ShapeDtypeStruct((n_pad, width), BF16),
        grid_spec=grid_spec,
        compiler_params=_params(("arbitrary",)),
        name="moe_row_gather",
    )(plan['tok'], src)


def _moe_group_body(texp_ref, tval_ref, xa_ref, xb_ref, ws_ref, wg_ref, wu_ref, wd_ref, ys_ref, acc_s):
    del texp_ref
    i = pl.program_id(0)
    j = pl.program_id(1)
    half = D // 2

    @pl.when(j == 0)
    def _():
        acc_s[...] = jnp.zeros_like(acc_s)

    @pl.when(tval_ref[i] > 0)
    def _():
        xa, xb = xa_ref[...], xb_ref[...]
        mm = lambda u, ref, rows: jnp.dot(u, ref[0, rows, :], preferred_element_type=F32)
        hg = mm(xa, wg_ref, slice(0, half)) + mm(xb, wg_ref, slice(half, D))
        hu = mm(xa, wu_ref, slice(0, half)) + mm(xb, wu_ref, slice(half, D))
        acc_s[...] += _dot(hg * jax.nn.sigmoid(hg) * hu, wd_ref[0])

    @pl.when(j == pl.num_programs(1) - 1)
    def _():
        ys_ref[...] = acc_s[...] * ws_ref[...]


def _moe_group_call(xa, xb, plan, wg, wu, wd):
    n_pad = plan['n_pad']
    dff = wg.shape[2]
    xspec = pl.BlockSpec((MOE_TILE, D // 2), lambda i, j, te, tv: (i, 0))
    grid_spec = pltpu.PrefetchScalarGridSpec(
        num_scalar_prefetch=2,
        grid=(n_pad // MOE_TILE, dff // MOE_TF),
        in_specs=[xspec, xspec,
                  pl.BlockSpec((MOE_TILE, 1), lambda i, j, te, tv: (i, 0)),
                  pl.BlockSpec((1, D, MOE_TF), lambda i, j, te, tv: (te[i], 0, j)),
                  pl.BlockSpec((1, D, MOE_TF), lambda i, j, te, tv: (te[i], 0, j)),
                  pl.BlockSpec((1, MOE_TF, D), lambda i, j, te, tv: (te[i], j, 0))],
        out_specs=pl.BlockSpec((MOE_TILE, D), lambda i, j, te, tv: (i, 0)),
        scratch_shapes=[pltpu.VMEM((MOE_TILE, D), F32)])
    return pl.pallas_call(
        _moe_group_body,
        out_shape=jax.ShapeDtypeStruct((n_pad, D), F32),
        grid_spec=grid_spec,
        compiler_params=_params(("parallel", "arbitrary")),
        name="moe_expert_swiglu",
    )(plan['texp'], plan['tval'], xa, xb, plan['w'], wg, wu, wd)


def _moe_combine_body(a8_ref, alo_ref, ahi_ref, *refs):
    ys_refs, tk_refs = refs[:N_EXPERTS], refs[N_EXPERTS:2 * N_EXPERTS]
    x1_ref, mod_ref, gpost_ref, o_ref = refs[2 * N_EXPERTS:]
    t = pl.program_id(0)
    tloc = t * CMB_TILE + lax.broadcasted_iota(jnp.int32, (SLAB, CMB_TILE), 1)
    srow = lax.broadcasted_iota(jnp.int32, (SLAB, 1), 0)
    f = jnp.zeros((CMB_TILE, D), F32)
    for e in range(N_EXPERTS):
        k = t * N_EXPERTS + e
        row = a8_ref[k] * 8 + srow
        tok = jnp.where((row >= alo_ref[k]) & (row < ahi_ref[k]), tk_refs[e][...], -1)
        onehot = jnp.where(tok == tloc, 1.0, 0.0).astype(BF16)
        f = f + _dot_rhs2(onehot, ys_refs[e][...], TN)
    m = mod_ref[0]
    o_ref[...] = x1_ref[...] + m[5:6] * _rms(f, gpost_ref[...])


def _moe_combine_call(ys, plan, x1, mod3, gpost, n_ctx, lat_t):
    ntok = x1.shape[0]
    midx = _mod_index(CMB_TILE, n_ctx, lat_t)

    def slab(w, e):
        return pl.BlockSpec((pl.Element(SLAB), pl.Element(w)),
                            lambda t, a8, lo, hi: (a8[t * N_EXPERTS + e] * 8, 0))

    experts = range(N_EXPERTS)
    grid_spec = pltpu.PrefetchScalarGridSpec(
        num_scalar_prefetch=3,
        grid=(ntok // CMB_TILE,),
        in_specs=[slab(D, e) for e in experts] + [slab(1, e) for e in experts] + [
            pl.BlockSpec((CMB_TILE, D), lambda t, a8, lo, hi: (t, 0)),
            pl.BlockSpec((1, 8, D), lambda t, a8, lo, hi: (midx(t), 0, 0)),
            pl.BlockSpec((1, D), lambda t, a8, lo, hi: (0, 0))],
        out_specs=pl.BlockSpec((CMB_TILE, D), lambda t, a8, lo, hi: (t, 0)))
    tok2d = plan['tok'].reshape(-1, 1)
    return pl.pallas_call(
        _moe_combine_body,
        out_shape=jax.ShapeDtypeStruct((ntok, D), F32),
        grid_spec=grid_spec,
        compiler_params=_params(("parallel",)),
        name="moe_combine",
    )(plan['a8'], plan['alo'], plan['ahi'], *([ys] * N_EXPERTS), *([tok2d] * N_EXPERTS), x1, mod3, gpost)


def _rope_tables(t):
    i = jnp.arange(M_HD)
    half, pair, f = i // 64, (i % 64) // 32, i % 32
    tt = jnp.arange(t)
    pos = jnp.stack([tt // GRID_W, tt % GRID_W], axis=-1).astype(F32)
    inv = ROPE_BASE ** (-f.astype(F32) / 32)
    ang = pos[:, half] * inv[None, :]
    sin = jnp.sin(ang)
    return (jnp.cos(ang), jnp.where(pair == 0, -sin, 0.0)[...], jnp.where(pair == 1, sin, 0.0))


def _pad_in_proj(w):
    off = {}
    o = 0
    for name, width in (('r', 512), ('k', 512), ('v', 512), ('wl', 128), ('al', 128), ('gl', 128),
                        ('mq', 512), ('mk', 512), ('mv', 512), ('mo', 512), ('mif', 16),
                        ('nq', 512), ('nk', 512), ('nv', 512), ('gate', 3072)):
        off[name] = (o, o + width)
        o += width
    col = lambda n: w[:, off[n][0]:off[n][1]]
    parts = [col(n) for n in ('r', 'k', 'v', 'mq', 'mk', 'mv', 'mo', 'nq', 'nk', 'nv', 'gate', 'wl', 'al', 'gl', 'mif')]
    parts.append(jnp.zeros((w.shape[0], LANE - 16), w.dtype))
    return jnp.concatenate(parts, axis=1).astype(BF16)


def _block_avg(width, group):
    i = jnp.arange(width) // group
    return jnp.where(i[:, None] == i[None, :], 1.0 / group, 0.0).astype(BF16)


def _zero_pad_rows(w2):
    z = jnp.zeros_like(w2[0])
    return jnp.stack([jnp.concatenate([w2[0], z], axis=0), jnp.concatenate([z, w2[1]], axis=0)])


def kernel(x_prompt, x_sample, state_rwkv, state_mlstm_C, state_mlstm_n, state_mlstm_m, cache_nat_k, cache_nat_v,
           c, c_ctx, w_mod, b_mod, g_pre_mix, g_post_mix, g_pre_ffn, g_post_ffn, w_in,
           rw_w0, rw_w2, rw_a0, rw_a2, rw_g2, rw_k_k, rw_k_a, rw_r_k, rw_ln_w, rw_ln_b,
           ml_b_i, ml_b_f, ml_norm_w, nat_rpb, w_o_rwkv, w_o_mlstm, w_o_nat, w_out,
           ff_w_gate, ff_w_up, ff_w_down, moe_w_router, moe_b_router, moe_w_gate, moe_w_up, moe_w_down):
    cb, ct = x_prompt.shape[:2]
    lb, lt = x_sample.shape[:2]
    n_ctx, n_lat = cb * ct, lb * lt
    x = jnp.concatenate([x_prompt.reshape(n_ctx, D), x_sample.reshape(n_lat, D)], axis=0)
    cvec = jnp.zeros((8, D), F32).at[0].set(c_ctx).at[1:1 + lb].set(c)
    rope = _rope_tables(lt)
    avg64, avg128 = _block_avg(HW, R_HD), _block_avg(HW, M_HD)
    lat_tb = min(lt, 512)
    s_rw = ml_st = kv_c = None
    for l in range(DEPTH):
        mod = _mod_call(cvec, w_mod[l], b_mod[l])
        mod3 = jnp.pad(mod[:1 + lb].reshape(1 + lb, 6, D), ((0, 0), (0, 2), (0, 0)))
        z = _proj_call(x, mod3, g_pre_mix[l], _pad_in_proj(w_in[l]), n_ctx, lt)

        rp = dict(w0=rw_w0[l].reshape(2, 1, HW), w2=_zero_pad_rows(rw_w2[l]),
                  a0=rw_a0[l].reshape(2, 1, HW), a2=_zero_pad_rows(rw_a2[l]),
                  kk=rw_k_k[l].reshape(1, HW), ka=rw_k_a[l].reshape(1, HW), rk=rw_r_k[l].reshape(1, HW))
        yr, bv, s_rw = _rwkv_call(z, rp, None, None, row0=0, nb=cb, t=ct, tb=RW_BLOCK, layer=l,
                                  prev_state=None if s_rw is None else (s_rw,))
        yr, bv = _rwkv_call(z, rp, state_rwkv[:, l], (yr, bv), row0=n_ctx, nb=lb, t=lt, tb=RW_BLOCK)

        gt = z[:, ZB_MIF * LANE:ZB_MIF * LANE + 16].reshape(-1, CHUNK, 16).transpose(0, 2, 1)
        hm, *ml_st = _mlstm_call(z, gt, None, ml_b_i[l], ml_b_f[l], None, None,
                                 row0=0, nb=cb, t=ct, tb=RW_BLOCK, layer=l, prev_state=ml_st)
        init = (state_mlstm_C[:, l], state_mlstm_n[:, l][:, :, :, None, :], state_mlstm_m[:, l])
        hm, = _mlstm_call(z, gt, rope, ml_b_i[l], ml_b_f[l], init, (hm,),
                          row0=n_ctx, nb=lb, t=lt, tb=lat_tb)

        yn, *kv_c = _ctx_attn_call(z, kv_c, nb=cb, t=ct, layer=l)
        yn = _natten_call(z, cache_nat_k[:, l], cache_nat_v[:, l], _natten_table(nat_rpb[l]), yn,
                          row0=n_ctx, nb=lb, t=lt)

        mw = dict(avg64=avg64, avg128=avg128, rg2=rw_g2[l].astype(BF16), lnw=rw_ln_w[l].reshape(1, HW),
                  lnb=rw_ln_b[l].reshape(1, HW), mlw=ml_norm_w[l].reshape(1, HW),
                  wor=w_o_rwkv[l].astype(BF16), wom=w_o_mlstm[l].astype(BF16), won=w_o_nat[l].astype(BF16),
                  wout=w_out[l].astype(BF16), gpost=g_post_mix[l].reshape(1, D), gpre=g_pre_ffn[l].reshape(1, D))
        j = l // 2
        if l % 2 == 0:
            router = None
        else:
            wr = jnp.pad(moe_w_router[j], ((0, 0), (0, LANE - N_EXPERTS)))
            br = jnp.pad(moe_b_router[j], (0, LANE - N_EXPERTS), constant_values=NEG).reshape(1, LANE)
            router = (wr, br)
        merged = _merge_call(x, mod3, yr, bv, hm, yn, z, mw, n_ctx, lt, router)
        gpost = g_post_ffn[l].reshape(1, D)
        if l % 2 == 0:
            x = _ffn_call(merged[1], merged[0], mod3, gpost, ff_w_gate[j].astype(BF16),
                          ff_w_up[j].astype(BF16), ff_w_down[j].astype(BF16), n_ctx, lt)
        else:
            plan = _moe_plan(merged[2], n_ctx + n_lat)
            xa, xb = _moe_gather_call(merged[1][0], plan), _moe_gather_call(merged[1][1], plan)
            ys = _moe_group_call(xa, xb, plan, moe_w_gate[j].astype(BF16), moe_w_up[j].astype(BF16),
                                 moe_w_down[j].astype(BF16))
            x = _moe_combine_call(ys, plan, merged[0], mod3, gpost, n_ctx, lt)

    c_m, n_m, m_m = ml_st
    return (x[:n_ctx].reshape(cb, ct, D), x[n_ctx:].reshape(lb, lt, D),
            s_rw, c_m, n_m[:, :, :, :, 0, :], m_m[:, :, :, :, 0, 0], kv_c[0], kv_c[1])
```

```python
import functools
import math

import jax
import jax.numpy as jnp
from jax import lax
from jax.experimental import pallas as pl
from jax.experimental.pallas import tpu as pltpu

F32 = jnp.float32
BF16 = jnp.bfloat16

D = 1024
N_CTX_B, N_CTX_T = 32, 256
N_LAT_B, N_LAT_T = 2, 4096
DEPTH = 2
GRID_W = 64
R_HEADS, R_HD = 8, 64
M_HEADS, M_HD = 4, 128
N_HEADS, N_HD = 8, 64
HW = 512
WIN_ROWS, WIN_COLS = 8, 16
N_EXPERTS = 8
RWKV_GN_EPS = 64e-5
NORM_EPS = 1e-6
ROPE_BASE = 10000.0
CHUNK = 64
NEG = -1e30
LANE = 128
VMEM_LIMIT = 56 * 1024 * 1024

ZB_R, ZB_K, ZB_V = 0, 4, 8
ZB_MQ, ZB_MK, ZB_MV, ZB_MO = 12, 16, 20, 24
ZB_NQ, ZB_NK, ZB_NV = 28, 32, 36
ZB_GATE = 40
ZB_WL, ZB_AL, ZB_GL, ZB_MIF = 64, 65, 66, 67
P_PAD = 68 * LANE

NN = (((1,), (0,)), ((), ()))
NT = (((1,), (1,)), ((), ()))
TN = (((0,), (0,)), ((), ()))


def _dot(a, b, dims=NN):
    return lax.dot_general(a.astype(BF16), b.astype(BF16), dims, preferred_element_type=F32)


def _split(x):
    hi = x.astype(BF16)
    return hi, (x - hi.astype(F32)).astype(BF16)


def _dot3(a, b, dims=NN):
    ah, al = _split(a)
    bh, bl = _split(b)
    d = lambda u, w: lax.dot_general(u, w, dims, preferred_element_type=F32)
    return d(ah, bh) + (d(ah, bl) + d(al, bh))


def _dot_lhs2(a, b_exact, dims=NN):
    ah, al = _split(a)
    d = lambda u: lax.dot_general(u, b_exact, dims, preferred_element_type=F32)
    return d(ah) + d(al)


def _dot_rhs2(a_exact, b, dims=NN):
    bh, bl = _split(b)
    d = lambda w: lax.dot_general(a_exact, w, dims, preferred_element_type=F32)
    return d(bh) + d(bl)


def _log_sigmoid(x):
    return jnp.minimum(x, 0.0) - jnp.log1p(jnp.exp(-jnp.abs(x)))


def _rms(x, g):
    return x * lax.rsqrt(jnp.mean(x * x, axis=-1, keepdims=True) + NORM_EPS) * g


def _params(sem):
    return pltpu.CompilerParams(dimension_semantics=sem, vmem_limit_bytes=VMEM_LIMIT)


def _alias_prev(in_specs, args, prev, first_out=0):
    aliases = {}
    for k, a in enumerate(prev or ()):
        aliases[len(args)] = first_out + k
        in_specs.append(pl.BlockSpec(memory_space=pl.ANY))
        args.append(a)
    return aliases


def _mod_body(c_ref, w_ref, b_ref, o_ref):
    c = c_ref[...]
    o_ref[...] = _dot(c * jax.nn.sigmoid(c), w_ref[...]) + b_ref[...]


def _mod_call(cvec, w_mod, b_mod):
    tn = 1536
    return pl.pallas_call(
        _mod_body,
        out_shape=jax.ShapeDtypeStruct((8, 6 * D), F32),
        grid=(6 * D // tn,),
        in_specs=[pl.BlockSpec((8, D), lambda j: (0, 0)),
                  pl.BlockSpec((D, tn), lambda j: (0, j)),
                  pl.BlockSpec((1, tn), lambda j: (0, j))],
        out_specs=pl.BlockSpec((8, tn), lambda j: (0, j)),
        compiler_params=_params(("arbitrary",)),
        name="adaln_mod",
    )(cvec, w_mod, b_mod.reshape(1, 6 * D))


def _mod_index(tile_rows, n_ctx, lat_t):
    def f(i):
        start = i * tile_rows
        return jnp.where(start < n_ctx, 0, 1 + (start - n_ctx) // lat_t)
    return f


def _proj_body(x_ref, mod_ref, g_ref, w_ref, z_ref, hin_s):
    @pl.when(pl.program_id(1) == 0)
    def _():
        m = mod_ref[0]
        hin_s[...] = (_rms(x_ref[...], g_ref[...]) * (1.0 + m[1:2]) + m[0:1]).astype(BF16)
    z_ref[...] = jnp.dot(hin_s[...], w_ref[...], preferred_element_type=F32)


def _proj_call(x, mod3, g, w_bf, n_ctx, lat_t, tm=1024, tn=2176):
    ntok = x.shape[0]
    midx = _mod_index(tm, n_ctx, lat_t)
    return pl.pallas_call(
        _proj_body,
        out_shape=jax.ShapeDtypeStruct((ntok, P_PAD), F32),
        grid=(ntok // tm, P_PAD // tn),
        in_specs=[pl.BlockSpec((tm, D), lambda i, j: (i, 0)),
                  pl.BlockSpec((1, 8, D), lambda i, j: (midx(i), 0, 0)),
                  pl.BlockSpec((1, D), lambda i, j: (0, 0)),
                  pl.BlockSpec((D, tn), lambda i, j: (0, j))],
        out_specs=pl.BlockSpec((tm, tn), lambda i, j: (i, j)),
        scratch_shapes=[pltpu.VMEM((tm, D), BF16)],
        compiler_params=_params(("parallel", "arbitrary")),
        name="in_proj",
    )(x, mod3, g.reshape(1, D), w_bf)


RW_TILE = 2 * CHUNK
RW_BLOCK = 1024
ML_BLOCK = 512


def _rwkv_body(*refs, tb_rows, nseq, has_init, has_out, n_alias):
    (r_ref, k_ref, v_ref, wl_ref, al_ref, w0_ref, w2_ref, a0_ref, a2_ref,
     kkw_ref, kaw_ref, rkw_ref, hsum_ref) = refs[:13]
    rest = list(refs[13:])
    s0_ref = rest.pop(0) if has_init else None
    del rest[:n_alias]
    y_ref, bv_ref = rest.pop(0), rest.pop(0)
    sout_ref = rest.pop(0) if has_out else None
    r2_s, m_s, g_s, st_s = rest
    C = CHUNK
    TL = RW_TILE
    nchunk = tb_rows // nseq // C
    d = pl.program_id(2)
    tb = pl.program_id(3)
    n_tb = pl.num_programs(3)

    ti = lax.broadcasted_iota(jnp.int32, (TL, TL), 0)
    si = lax.broadcasted_iota(jnp.int32, (TL, TL), 1)
    head_blk = (ti >> 6) == (si >> 6)

    @pl.when(tb == 0)
    def _():
        for q in range(nseq):
            if has_init:
                s0 = s0_ref[q, 0]
                two = jnp.concatenate([jnp.concatenate([s0[0], s0[0]], axis=1),
                                       jnp.concatenate([s0[1], s0[1]], axis=1)], axis=0)
                st_s[q] = jnp.where(head_blk, two, 0.0)
            else:
                st_s[q] = jnp.zeros((LANE, LANE), F32)

    hsum = hsum_ref[...]
    r = r_ref[...]
    kf = k_ref[...]
    v = v_ref[...]
    w_pre = w0_ref[0] + _dot(jnp.tanh(wl_ref[...]), w2_ref[0])
    lw = -math.exp(-0.5) * jax.nn.sigmoid(w_pre)
    a = jax.nn.sigmoid(a0_ref[0] + _dot(al_ref[...], a2_ref[0]))
    kd = kf * (1.0 + (a - 1.0) * kaw_ref[...])
    kkf = kf * kkw_ref[...]
    kn = kkf * lax.rsqrt(_dot_lhs2(kkf * kkf, hsum) + 1e-12)
    bd = kn * a
    bv_ref[0] = _dot_lhs2(r * kd * rkw_ref[...], hsum) * v

    dlt = (ti - si) * (1 - 2 * d)
    strict = head_blk & (dlt > 0)
    incl = head_blk & (dlt >= 0)
    cum_ones = jnp.concatenate([jnp.where(incl, 1.0, 0.0), jnp.where(head_blk, 1.0, 0.0)], axis=0).astype(BF16)
    is_diag = ti == si
    eye = jnp.where(is_diag, 1.0, 0.0)
    pair = (ti >> 1) == (si >> 1)
    sibling = [((ti >> lg) ^ (si >> lg)) == 1 for lg in range(1, 6)]
    head0 = si < R_HD

    n_tiles = tb_rows // TL
    tiles = []
    for tile in range(n_tiles):
        rows = slice(TL * tile, TL * (tile + 1))
        cs = _dot_rhs2(cum_ones, lw[rows])
        cum, tot = cs[:TL], cs[TL:]
        e_neg = jnp.exp(-cum)
        e_hat = jnp.exp(tot - cum)
        rt = r[rows] * jnp.exp(cum)
        at = -kn[rows] * jnp.exp(cum - lw[rows])
        tiles.append(dict(rows=rows, tot=tot, rt=rt, at=at, bh=bd[rows] * e_hat, kh=kd[rows] * e_hat,
                          rhs=jnp.concatenate([bd[rows] * e_neg, kd[rows] * e_neg], axis=0).astype(BF16),
                          at_bf=at.astype(BF16), v_bf=v[rows].astype(BF16)))
    chains = [(tl, h) for tl in tiles for h in range(2)]
    a_all = []
    for tl, h in chains:
        hm = head0 if h == 0 else ~head0
        lhs = jnp.concatenate([jnp.where(hm, tl['at'], 0.0), jnp.where(hm, tl['rt'], 0.0)], axis=0)
        a_all.append(_dot(lhs, tl['rhs'], NT))
    n_mat = [jnp.where(strict, a[:TL, :TL], 0.0) for a in a_all]
    a_ak = [jnp.where(strict, a[:TL, TL:], 0.0).astype(BF16) for a in a_all]
    a_rb = [jnp.where(incl, a[TL:, :TL], 0.0).astype(BF16) for a in a_all]
    a_rk = [jnp.where(incl, a[TL:, TL:], 0.0).astype(BF16) for a in a_all]
    t_inv = [eye + jnp.where(pair, n, 0.0) for n in n_mat]
    for sib in sibling:
        x_mat = [_dot(jnp.where(sib, n, 0.0), t) for n, t in zip(n_mat, t_inv)]
        t_inv = [t + _dot(t, x) for t, x in zip(t_inv, x_mat)]
    p_h = [_dot(t, tl['at_bf']) for t, (tl, _) in zip(t_inv, chains)]
    av = [_dot(a, tl['v_bf']) for a, (tl, _) in zip(a_ak, chains)]
    q_h = [_dot(t, x) for t, x in zip(t_inv, av)]
    r2_h = [_dot(a, p) for a, p in zip(a_rb, p_h)]
    y0_h = [_dot(a, q) + _dot(k_, tl['v_bf']) for a, q, k_, (tl, _) in zip(a_rb, q_h, a_rk, chains)]
    p_m, q_m = [], []
    for i, tl in enumerate(tiles):
        rows = tl['rows']
        pick = lambda u: jnp.where(head0, u[2 * i], u[2 * i + 1])
        p_m.append(pick(p_h))
        q_m.append(pick(q_h))
        r2_s[rows, :] = tl['rt'] + pick(r2_h)
        y_ref[0, rows, :] = pick(y0_h)
        for c in range(TL // C):
            cr = slice(C * c, C * (c + 1))
            decay = jnp.where(is_diag, jnp.exp(tl['tot'][C * c:C * c + 1]), 0.0)
            m_s[i * (TL // C) + c] = decay + jnp.where(head_blk, _dot(p_m[i][cr], tl['bh'][cr], TN), 0.0)
            g_s[i * (TL // C) + c] = jnp.where(
                head_blk, _dot(q_m[i][cr], tl['bh'][cr], TN) + _dot(v[rows][cr], tl['kh'][cr], TN), 0.0)

    def step(i, states):
        ci = jnp.where(d == 0, i, nchunk - 1 - i)
        new = []
        for q, s in enumerate(states):
            cq = q * nchunk + ci
            rows = pl.ds(pl.multiple_of(cq * C, C), C)
            y_ref[0, rows, :] = y_ref[0, rows, :] + _dot(r2_s[rows, :], s, NT)
            new.append(_dot3(s, m_s[cq]) + g_s[cq])
        return tuple(new)

    s_fin = lax.fori_loop(0, nchunk, step, tuple(st_s[q] for q in range(nseq)))
    for q in range(nseq):
        st_s[q] = s_fin[q]

    if has_out:
        @pl.when(tb == n_tb - 1)
        def _():
            for q in range(nseq):
                sout_ref[q, 0, 0, 0] = s_fin[q][:R_HD, :R_HD]
                sout_ref[q, 0, 0, 1] = s_fin[q][R_HD:, R_HD:]


def _rwkv_call(z, P, s0, prev, *, row0, nb, t, tb, layer=None, prev_state=None):
    nseq = max(tb // t, 1)
    n_tb = max(t // tb, 1)
    blk0 = row0 // tb
    has_init = s0 is not None

    def rowblk(b, d, i):
        return blk0 + b * n_tb + jnp.where(d == 0, i, n_tb - 1 - i)

    def zspec(cb):
        return pl.BlockSpec((tb, LANE), lambda b, hp, d, i: (rowblk(b, d, i), cb(hp)))

    vec = lambda: pl.BlockSpec((1, LANE), lambda b, hp, d, i: (0, hp))
    dvec = lambda: pl.BlockSpec((1, 1, LANE), lambda b, hp, d, i: (d, 0, hp))
    dmat = lambda: pl.BlockSpec((1, LANE, LANE), lambda b, hp, d, i: (d, 0, hp))
    st_spec = pl.BlockSpec((nseq, 1, 2, R_HD, R_HD), lambda b, hp, d, i: (b, d, hp, 0, 0))
    in_specs = [zspec(lambda hp: ZB_R + hp), zspec(lambda hp: ZB_K + hp), zspec(lambda hp: ZB_V + hp),
                zspec(lambda hp: ZB_WL), zspec(lambda hp: ZB_AL),
                dvec(), dmat(), dvec(), dmat(), vec(), vec(), vec(),
                pl.BlockSpec((LANE, LANE), lambda b, hp, d, i: (0, 0))]
    args = [z, z, z, z, z, P['w0'], P['w2'], P['a0'], P['a2'], P['kk'], P['ka'], P['rk'],
            _block_avg(LANE, R_HD) * R_HD]
    if has_init:
        in_specs.append(st_spec)
        args.append(s0)
    has_out = layer is not None
    aliases = _alias_prev(in_specs, args, prev)
    aliases.update(_alias_prev(in_specs, args, prev_state, first_out=2))
    yspec = pl.BlockSpec((1, tb, LANE), lambda b, hp, d, i: (d, rowblk(b, d, i), hp))
    out_shape = [jax.ShapeDtypeStruct((2, z.shape[0], HW), F32)] * 2
    out_specs = [yspec, yspec]
    if has_out:
        out_shape.append(jax.ShapeDtypeStruct((nb, DEPTH, 2, R_HEADS, R_HD, R_HD), F32))
        out_specs.append(pl.BlockSpec((nseq, 1, 1, 2, R_HD, R_HD), lambda b, hp, d, i: (b, layer, d, hp, 0, 0)))
    return pl.pallas_call(
        functools.partial(_rwkv_body, tb_rows=tb, nseq=nseq, has_init=has_init, has_out=has_out,
                          n_alias=len(aliases)),
        out_shape=out_shape,
        grid=(nb // nseq, R_HEADS // 2, 2, n_tb),
        in_specs=in_specs,
        out_specs=out_specs,
        input_output_aliases=aliases,
        scratch_shapes=[pltpu.VMEM((tb, LANE), F32), pltpu.VMEM((tb // CHUNK, LANE, LANE), F32),
                        pltpu.VMEM((tb // CHUNK, LANE, LANE), F32), pltpu.VMEM((nseq, LANE, LANE), F32)],
        compiler_params=_params(("parallel", "parallel", "arbitrary", "arbitrary")),
        name="rwkv7_chunked",
    )(*args)


def _mlstm_body(*refs, tb_rows, nseq, use_rope, has_init, has_out, n_alias):
    refs = list(refs)
    q_ref, k_ref, v_ref, g_ref, gt_ref = [refs.pop(0) for _ in range(5)]
    cos_ref, sa_ref, sb_ref = [refs.pop(0) for _ in range(3)] if use_rope else (None,) * 3
    bi_ref, bf_ref = refs.pop(0), refs.pop(0)
    c0_ref, n0_ref, m0_ref = [refs.pop(0) for _ in range(3)] if has_init else (None,) * 3
    del refs[:n_alias]
    h_ref = refs.pop(0)
    cout_ref, nout_ref, mout_ref = [refs.pop(0) for _ in range(3)] if has_out else (None,) * 3
    sc_s, kv_s, nk_s, cprev_s, nprev_s, c_s, n_s, m_s = refs
    L = CHUNK
    nck = tb_rows // L
    ncs = nck // nseq
    b = pl.program_id(0)
    h = pl.program_id(1)
    d = pl.program_id(2)
    tb = pl.program_id(3)
    n_tb = pl.num_programs(3)
    bi = bi_ref[d, h]
    bf = bf_ref[d, h]
    ji = d * (2 * M_HEADS) + h
    jf = ji + M_HEADS

    @pl.when(tb == 0)
    def _():
        for q in range(nseq):
            if has_init:
                c_s[q] = c0_ref[q, 0, 0]
                n_s[q] = n0_ref[q, 0, 0]
                m_s[q] = jnp.full((1, M_HD), m0_ref[b * nseq + q, d, h], F32)
            else:
                c_s[q] = jnp.zeros((M_HD, M_HD), F32)
                n_s[q] = jnp.zeros((1, M_HD), F32)
                m_s[q] = jnp.zeros((1, M_HD), F32)

    g = g_ref[...]
    lane = lax.broadcasted_iota(jnp.int32, g.shape, 1)
    icol_all = jnp.sum(jnp.where(lane == ji, g, 0.0), axis=1, keepdims=True) + bi
    fcol_all = _log_sigmoid(jnp.sum(jnp.where(lane == jf, g, 0.0), axis=1, keepdims=True) + bf)

    ti = lax.broadcasted_iota(jnp.int32, (L, L), 0)
    si = lax.broadcasted_iota(jnp.int32, (L, L), 1)
    sgn = 1 - 2 * d
    incl = (ti - si) * sgn >= 0
    incl_t = (si - ti) * sgn >= 0
    row128 = lambda x: jnp.broadcast_to(x, (1, M_HD))

    ch = []
    for c in range(nck):
        rows = slice(L * c, L * (c + 1))
        irow = gt_ref[c, pl.ds(ji, 1), :] + bi
        frow = _log_sigmoid(gt_ref[c, pl.ds(jf, 1), :] + bf)
        bsum_col = jnp.sum(jnp.where(incl, frow, 0.0), axis=1, keepdims=True)
        bsum_row = jnp.sum(jnp.where(incl_t, fcol_all[rows], 0.0), axis=0, keepdims=True)
        blast = jnp.sum(frow, axis=1, keepdims=True)
        wlog = blast - bsum_col + icol_all[rows]
        dmat = jnp.where(incl, bsum_col - bsum_row + irow, NEG)
        sc_s[c, 0:1, :] = row128(blast)
        sc_s[c, 1:2, :] = row128(jnp.max(wlog, axis=0, keepdims=True))
        ch.append(dict(rows=rows, bsum_col=bsum_col, blast=blast, wlog=wlog, dmat=dmat,
                       dmax=jnp.max(dmat, axis=1, keepdims=True)))

    def m_step(i, ms):
        ci = jnp.where(d == 0, i, ncs - 1 - i)
        new = []
        for q, m in enumerate(ms):
            cq = q * ncs + ci
            m_new = jnp.maximum(sc_s[cq, 0:1, :] + m, sc_s[cq, 1:2, :])
            sc_s[cq, 2:3, :] = m
            sc_s[cq, 3:4, :] = m_new
            new.append(m_new)
        return tuple(new)

    m_fin = lax.fori_loop(0, ncs, m_step, tuple(m_s[q] for q in range(nseq)))

    for c, cd in enumerate(ch):
        rows = cd['rows']
        qc, kc = q_ref[rows, :], k_ref[rows, :]
        if use_rope:
            cos, sa, sb = cos_ref[rows, :], sa_ref[rows, :], sb_ref[rows, :]
            qc = qc * cos + pltpu.roll(qc, 96, 1) * sa + pltpu.roll(qc, 32, 1) * sb
            kc = kc * cos + pltpu.roll(kc, 96, 1) * sa + pltpu.roll(kc, 32, 1) * sb
        cd['q'] = qc
        cd['k'] = kc * (M_HD ** -0.5)
        cd['v'] = v_ref[rows, :]
        cd['m_prev'] = sc_s[c, 2:3, 0:1]
        cd['m_new'] = sc_s[c, 3:4, 0:1]
    for cd in ch:
        cd['qk'] = _dot(cd['q'], cd['k'], NT)
    for c, cd in enumerate(ch):
        kw = cd['k'] * jnp.exp(cd['wlog'] - cd['m_new'])
        kv_s[c] = _dot3(kw, cd['v'], TN)
        nk_s[c] = jnp.sum(kw, axis=0, keepdims=True)
        sc_s[c, 4:5, :] = row128(jnp.exp(cd['blast'] + cd['m_prev'] - cd['m_new']))
    for cd in ch:
        inter = cd['bsum_col'] + cd['m_prev']
        cd['mt'] = jnp.maximum(inter, cd['dmax'])
        cd['iw'] = jnp.exp(inter - cd['mt'])
        sm = cd['qk'] * jnp.exp(cd['dmat'] - cd['mt'])
        cd['ssum'] = jnp.sum(sm, axis=1, keepdims=True)
        cd['sv'] = _dot(sm, cd['v'])

    def c_step(i, carry):
        ci = jnp.where(d == 0, i, ncs - 1 - i)
        new = []
        for q, (c_st, n_st) in enumerate(carry):
            cq = q * ncs + ci
            dec = sc_s[cq, 4:5, 0:1]
            cprev_s[cq] = c_st
            nprev_s[cq] = n_st
            new.append((dec * c_st + kv_s[cq], dec * n_st + nk_s[cq]))
        return tuple(new)

    st_fin = lax.fori_loop(0, ncs, c_step, tuple((c_s[q], n_s[q]) for q in range(nseq)))
    for q in range(nseq):
        c_s[q], n_s[q] = st_fin[q]
        m_s[q] = m_fin[q]

    qc_prev = [_dot(cd['q'], cprev_s[c]) for c, cd in enumerate(ch)]
    for c, cd in enumerate(ch):
        num = cd['iw'] * qc_prev[c] + cd['sv']
        den = cd['iw'] * jnp.sum(cd['q'] * nprev_s[c], axis=1, keepdims=True) + cd['ssum']
        h_ref[0, cd['rows'], :] = num / jnp.maximum(jnp.abs(den), jnp.exp(-cd['mt']))

    if has_out:
        @pl.when(tb == n_tb - 1)
        def _():
            for q in range(nseq):
                cout_ref[q, 0, 0, 0] = st_fin[q][0]
                nout_ref[q, 0, 0, 0] = st_fin[q][1]
                mout_ref[q, 0, 0, 0] = m_fin[q]


def _mlstm_call(z, gt, rope, b_i, b_f, init, prev, *, row0, nb, t, tb, layer=None, prev_state=None):
    nseq = max(tb // t, 1)
    n_tb = max(t // tb, 1)
    blk0 = row0 // tb
    cpb = tb // CHUNK
    use_rope = rope is not None
    has_init = init is not None

    def rowblk(b, d, i):
        return blk0 + b * n_tb + jnp.where(d == 0, i, n_tb - 1 - i)

    def zspec(cb):
        return pl.BlockSpec((tb, LANE), lambda b, h, d, i: (rowblk(b, d, i), cb(h)))

    smem = pl.BlockSpec(memory_space=pltpu.SMEM)
    in_specs = [zspec(lambda h: ZB_MQ + h), zspec(lambda h: ZB_MK + h), zspec(lambda h: ZB_MV + h),
                zspec(lambda h: ZB_MIF),
                pl.BlockSpec((cpb, 16, CHUNK), lambda b, h, d, i: (rowblk(b, d, i), 0, 0))]
    args = [z, z, z, z, gt]
    if use_rope:
        tspec = pl.BlockSpec((tb, LANE), lambda b, h, d, i: (jnp.where(d == 0, i, n_tb - 1 - i), 0))
        in_specs += [tspec] * 3
        args += list(rope)
    in_specs += [smem, smem]
    args += [b_i, b_f]
    cspec = pl.BlockSpec((nseq, 1, 1, M_HD, M_HD), lambda b, h, d, i: (b, d, h, 0, 0))
    nspec = pl.BlockSpec((nseq, 1, 1, 1, M_HD), lambda b, h, d, i: (b, d, h, 0, 0))
    if has_init:
        in_specs += [cspec, nspec, smem]
        args += list(init)
    has_out = layer is not None
    aliases = _alias_prev(in_specs, args, prev)
    aliases.update(_alias_prev(in_specs, args, prev_state, first_out=1))
    out_shape = [jax.ShapeDtypeStruct((2, z.shape[0], HW), F32)]
    out_specs = [pl.BlockSpec((1, tb, LANE), lambda b, h, d, i: (d, rowblk(b, d, i), h))]
    if has_out:
        vec_shape = jax.ShapeDtypeStruct((nb, DEPTH, 2, M_HEADS, 1, M_HD), F32)
        out_shape += [jax.ShapeDtypeStruct((nb, DEPTH, 2, M_HEADS, M_HD, M_HD), F32), vec_shape, vec_shape]
        vec_spec = pl.BlockSpec((nseq, 1, 1, 1, 1, M_HD), lambda b, h, d, i: (b, layer, d, h, 0, 0))
        out_specs += [pl.BlockSpec((nseq, 1, 1, 1, M_HD, M_HD), lambda b, h, d, i: (b, layer, d, h, 0, 0)),
                      vec_spec, vec_spec]
    return pl.pallas_call(
        functools.partial(_mlstm_body, tb_rows=tb, nseq=nseq, use_rope=use_rope, has_init=has_init,
                          has_out=has_out, n_alias=len(aliases)),
        out_shape=out_shape,
        grid=(nb // nseq, M_HEADS, 2, n_tb),
        in_specs=in_specs,
        out_specs=out_specs,
        input_output_aliases=aliases,
        scratch_shapes=[pltpu.VMEM((cpb, 8, M_HD), F32),
                        pltpu.VMEM((cpb, M_HD, M_HD), F32), pltpu.VMEM((cpb, 1, M_HD), F32),
                        pltpu.VMEM((cpb, M_HD, M_HD), F32), pltpu.VMEM((cpb, 1, M_HD), F32),
                        pltpu.VMEM((nseq, M_HD, M_HD), F32), pltpu.VMEM((nseq, 1, M_HD), F32),
                        pltpu.VMEM((nseq, 1, M_HD), F32)],
        compiler_params=_params(("parallel", "parallel", "arbitrary", "arbitrary")),
        name="mlstm_chunkwise",
    )(*args)


def _ctx_attn_body(q_ref, k_ref, v_ref, *refs):
    y_ref, ko_ref, vo_ref = refs[-3:]
    q = q_ref[...] * (N_HD ** -0.5)
    k = k_ref[...]
    v = v_ref[...]
    heads = [slice(N_HD * h, N_HD * (h + 1)) for h in range(2)]
    for h, sl in enumerate(heads):
        ko_ref[0, 0, h] = k[:, sl]
        vo_ref[0, 0, h] = v[:, sl]
    s = [_dot(q[:, sl], k[:, sl], NT) for sl in heads]
    p = [jnp.exp(s_ - jnp.max(s_, axis=-1, keepdims=True)) for s_ in s]
    o = [_dot(p_, v[:, sl]) for p_, sl in zip(p, heads)]
    for o_, p_, sl in zip(o, p, heads):
        y_ref[:, sl] = o_ / jnp.sum(p_, axis=-1, keepdims=True)


def _ctx_attn_call(z, prev_kv, *, nb, t, layer):
    zspec = lambda cb: pl.BlockSpec((t, LANE), lambda b, hp: (b, cb + hp))
    kv_spec = pl.BlockSpec((1, 1, 2, t, N_HD), lambda b, hp: (b, layer, hp, 0, 0))
    kv_shape = jax.ShapeDtypeStruct((nb, DEPTH, N_HEADS, t, N_HD), F32)
    in_specs = [zspec(ZB_NQ), zspec(ZB_NK), zspec(ZB_NV)]
    args = [z, z, z]
    aliases = _alias_prev(in_specs, args, prev_kv, first_out=1)
    return pl.pallas_call(
        _ctx_attn_body,
        out_shape=[jax.ShapeDtypeStruct((z.shape[0], HW), F32), kv_shape, kv_shape],
        grid=(nb, N_HEADS // 2),
        in_specs=in_specs,
        out_specs=[pl.BlockSpec((t, LANE), lambda b, hp: (b, hp)), kv_spec, kv_spec],
        input_output_aliases=aliases,
        compiler_params=_params(("parallel", "parallel")),
        name="context_attention",
    )(*args)


def _natten_body(q_ref, k_ref, v_ref, kc_ref, vc_ref, tab_ref, prev_ref, y_ref, *, rows_per_step, rows_n):
    del prev_ref
    rb = pl.program_id(2)
    n_loc = WIN_ROWS * GRID_W

    unroll = 4

    def rows(it, carry):
        work = []
        for u in range(unroll):
            i = it * unroll + u
            r = rb * rows_per_step + i
            rstart = jnp.clip(r - WIN_ROWS // 2, 0, rows_n - WIN_ROWS)
            qrows = pl.ds(pl.multiple_of(i * GRID_W, GRID_W), GRID_W)
            krows = pl.ds(pl.multiple_of(rstart * GRID_W, GRID_W), n_loc)
            work += [(qrows, krows, r - rstart, h, slice(N_HD * h, N_HD * (h + 1))) for h in range(2)]
        q = [q_ref[qr, sl] * (N_HD ** -0.5) for qr, _, _, _, sl in work]
        s_loc = [_dot(q_, k_ref[kr, sl], NT) + tab_ref[h, var] for q_, (_, kr, var, h, sl) in zip(q, work)]
        s_ctx = [_dot(q_, kc_ref[0, h], NT) for q_, (_, _, _, h, _) in zip(q, work)]
        m = [jnp.maximum(jnp.max(a, axis=-1, keepdims=True), jnp.max(c, axis=-1, keepdims=True))
             for a, c in zip(s_loc, s_ctx)]
        p_loc = [jnp.exp(a - m_) for a, m_ in zip(s_loc, m)]
        p_ctx = [jnp.exp(c - m_) for c, m_ in zip(s_ctx, m)]
        o = [_dot(a, v_ref[kr, sl]) + _dot(c, vc_ref[0, h])
             for a, c, (_, kr, _, h, sl) in zip(p_loc, p_ctx, work)]
        for o_, a, c, (qr, _, _, _, sl) in zip(o, p_loc, p_ctx, work):
            y_ref[qr, sl] = o_ / (jnp.sum(a, axis=-1, keepdims=True) + jnp.sum(c, axis=-1, keepdims=True))
        return carry

    lax.fori_loop(0, rows_per_step // unroll, rows, 0)


def _natten_call(z, k_ctx, v_ctx, tab, prev, *, row0, nb, t, rows_per_step=8):
    rows_n = t // GRID_W
    tq = rows_per_step * GRID_W
    qblk0 = row0 // tq
    sblk0 = row0 // t
    n_rb = rows_n // rows_per_step
    past = k_ctx.shape[2]
    kv_spec = lambda cb: pl.BlockSpec((t, LANE), lambda b, hp, rb: (sblk0 + b, cb + hp))
    cache_spec = pl.BlockSpec((1, 2, past, N_HD), lambda b, hp, rb: (b, hp, 0, 0))
    return pl.pallas_call(
        functools.partial(_natten_body, rows_per_step=rows_per_step, rows_n=rows_n),
        out_shape=jax.ShapeDtypeStruct((z.shape[0], HW), F32),
        grid=(nb, N_HEADS // 2, n_rb),
        in_specs=[pl.BlockSpec((tq, LANE), lambda b, hp, rb: (qblk0 + b * n_rb + rb, ZB_NQ + hp)),
                  kv_spec(ZB_NK), kv_spec(ZB_NV), cache_spec, cache_spec,
                  pl.BlockSpec((2, WIN_ROWS, GRID_W, WIN_ROWS * GRID_W), lambda b, hp, rb: (hp, 0, 0, 0)),
                  pl.BlockSpec(memory_space=pl.ANY)],
        out_specs=pl.BlockSpec((tq, LANE), lambda b, hp, rb: (qblk0 + b * n_rb + rb, hp)),
        input_output_aliases={6: 0},
        compiler_params=_params(("parallel", "parallel", "arbitrary")),
        name="neighbourhood_attention",
    )(z, z, z, k_ctx, v_ctx, tab, prev)


def _natten_table(rpb):
    cq = jnp.arange(GRID_W)[:, None]
    ck = jnp.arange(GRID_W)[None, :]
    dc = jnp.clip(ck - cq, -(WIN_COLS - 1), WIN_COLS - 1) + (WIN_COLS - 1)
    cstart = jnp.clip(cq - WIN_COLS // 2, 0, GRID_W - WIN_COLS)
    in_band = (ck >= cstart) & (ck < cstart + WIN_COLS)
    onehot = ((dc[None] == jnp.arange(2 * WIN_COLS - 1)[:, None, None]) & in_band[None]).astype(F32)
    cols = jnp.einsum('hrd,dqk->hrqk', rpb, onehot, precision=lax.Precision.HIGHEST)
    cols = cols + jnp.where(in_band, 0.0, NEG)
    tab = jnp.stack([cols[:, WIN_ROWS - 1 - var:2 * WIN_ROWS - 1 - var] for var in range(WIN_ROWS)], axis=1)
    return tab.transpose(0, 1, 3, 2, 4).reshape(rpb.shape[0], WIN_ROWS, GRID_W, WIN_ROWS * GRID_W)


def _group_norm(y, avg_bf, eps):
    mu = _dot_lhs2(y, avg_bf)
    yc = y - mu
    var = _dot_lhs2(yc * yc, avg_bf)
    return yc * lax.rsqrt(var + eps)


def _merge_body(*refs, with_router):
    refs = list(refs)
    (x_ref, mod_ref, yr_ref, bv_ref, hm_ref, yn_ref, gl_ref, mo_ref, g0_ref, g1_ref, g2_ref,
     avg64_ref, avg128_ref, rg2_ref, lnw_ref, lnb_ref, mlw_ref, wor_ref, wom_ref, won_ref, wout_ref,
     gpost_ref, gpre_ref) = [refs.pop(0) for _ in range(23)]
    wr_ref, br_ref = (refs.pop(0), refs.pop(0)) if with_router else (None, None)
    x1_ref, hin_ref = refs.pop(0), refs.pop(0)
    sel_ref = refs.pop(0) if with_router else None
    m = mod_ref[0]
    yr = yr_ref[0] + yr_ref[1]
    yn_r = _group_norm(yr, avg64_ref[...], RWKV_GN_EPS) * lnw_ref[...] + lnb_ref[...]
    g = _dot(jax.nn.sigmoid(gl_ref[...]), rg2_ref[...])
    out_r = (yn_r + bv_ref[0] + bv_ref[1]) * g
    hn = _group_norm(hm_ref[0] + hm_ref[1], avg128_ref[...], NORM_EPS) * mlw_ref[...]
    out_m = jax.nn.sigmoid(mo_ref[...]) * hn
    merged = (jax.nn.sigmoid(g0_ref[...]) * _dot(out_r, wor_ref[...])
              + jax.nn.sigmoid(g1_ref[...]) * _dot(out_m, wom_ref[...])
              + jax.nn.sigmoid(g2_ref[...]) * _dot(yn_ref[...], won_ref[...]))
    o = _dot(merged, wout_ref[...])
    x1 = x_ref[...] + m[2:3] * _rms(o, gpost_ref[...])
    x1_ref[...] = x1
    hin = _rms(x1, gpre_ref[...]) * (1.0 + m[4:5]) + m[3:4]
    if with_router:
        sel_ref[...] = _top2_select(_dot3(hin, wr_ref[...]) + br_ref[...])
        hin_ref[0] = hin[:, :D // 2]
        hin_ref[1] = hin[:, D // 2:]
    else:
        hin_ref[...] = hin.astype(BF16)


def _merge_call(x, mod3, yr, bv, hm, yn, z, W, n_ctx, lat_t, router, tm=256):
    ntok = x.shape[0]
    midx = _mod_index(tm, n_ctx, lat_t)
    with_router = router is not None
    row = lambda w: pl.BlockSpec((tm, w), lambda i: (i, 0))
    dirs = pl.BlockSpec((2, tm, HW), lambda i: (0, i, 0))
    zspec = lambda w, cb: pl.BlockSpec((tm, w), lambda i: (i, cb))
    full = lambda a: pl.BlockSpec(a.shape, lambda i: (0,) * a.ndim)
    consts = [W['avg64'], W['avg128'], W['rg2'], W['lnw'], W['lnb'], W['mlw'], W['wor'], W['wom'], W['won'],
              W['wout'], W['gpost'], W['gpre']]
    in_specs = [row(D), pl.BlockSpec((1, 8, D), lambda i: (midx(i), 0, 0)), dirs, dirs, dirs, row(HW),
                zspec(LANE, ZB_GL), zspec(HW, ZB_MO // 4),
                zspec(D, ZB_GATE // 8), zspec(D, ZB_GATE // 8 + 1), zspec(D, ZB_GATE // 8 + 2)]
    in_specs += [full(a) for a in consts]
    args = [x, mod3, yr, bv, hm, yn, z, z, z, z, z] + consts
    if with_router:
        out_shape = [jax.ShapeDtypeStruct((ntok, D), F32), jax.ShapeDtypeStruct((2, ntok, D // 2), F32)]
        out_specs = [row(D), pl.BlockSpec((2, tm, D // 2), lambda i: (0, i, 0))]
    else:
        out_shape = [jax.ShapeDtypeStruct((ntok, D), F32), jax.ShapeDtypeStruct((ntok, D), BF16)]
        out_specs = [row(D), row(D)]
    if with_router:
        in_specs += [full(router[0]), full(router[1])]
        args += list(router)
        out_shape.append(jax.ShapeDtypeStruct((ntok, LANE), F32))
        out_specs.append(row(LANE))
    return pl.pallas_call(
        functools.partial(_merge_body, with_router=with_router),
        out_shape=out_shape,
        grid=(ntok // tm,),
        in_specs=in_specs,
        out_specs=out_specs,
        compiler_params=_params(("parallel",)),
        name="branch_merge",
    )(*args)


def _top2_select(lg):
    lane = lax.broadcasted_iota(jnp.int32, lg.shape, 1)
    m1 = jnp.max(lg, axis=1, keepdims=True)
    i1 = jnp.min(jnp.where(lg == m1, lane, LANE), axis=1, keepdims=True)
    lg2 = jnp.where(lane == i1, -jnp.inf, lg)
    m2 = jnp.max(lg2, axis=1, keepdims=True)
    i2 = jnp.min(jnp.where(lg2 == m2, lane, LANE), axis=1, keepdims=True)
    e2 = jnp.exp(m2 - m1)
    den = 1.0 + e2
    return jnp.where(lane == 0, i1.astype(F32),
                     jnp.where(lane == 1, i2.astype(F32),
                               jnp.where(lane == 2, 1.0 / den, jnp.where(lane == 3, e2 / den, 0.0))))


def _ffn_body(h_ref, x1_ref, mod_ref, gpost_ref, wg_ref, wu_ref, wd_ref, o_ref, acc_s):
    j = pl.program_id(1)

    @pl.when(j == 0)
    def _():
        acc_s[...] = jnp.zeros_like(acc_s)

    hin = h_ref[...]
    hg = jnp.dot(hin, wg_ref[...], preferred_element_type=F32)
    hu = jnp.dot(hin, wu_ref[...], preferred_element_type=F32)
    acc_s[...] += _dot(hg * jax.nn.sigmoid(hg) * hu, wd_ref[...])

    @pl.when(j == pl.num_programs(1) - 1)
    def _():
        m = mod_ref[0]
        o_ref[...] = x1_ref[...] + m[5:6] * _rms(acc_s[...], gpost_ref[...])


def _ffn_call(hin, x1, mod3, gpost, wg, wu, wd, n_ctx, lat_t, tm=512, tf=1408):
    ntok = hin.shape[0]
    dff = wg.shape[1]
    midx = _mod_index(tm, n_ctx, lat_t)
    return pl.pallas_call(
        _ffn_body,
        out_shape=jax.ShapeDtypeStruct((ntok, D), F32),
        grid=(ntok // tm, dff // tf),
        in_specs=[pl.BlockSpec((tm, D), lambda i, j: (i, 0)),
                  pl.BlockSpec((tm, D), lambda i, j: (i, 0)),
                  pl.BlockSpec((1, 8, D), lambda i, j: (midx(i), 0, 0)),
                  pl.BlockSpec((1, D), lambda i, j: (0, 0)),
                  pl.BlockSpec((D, tf), lambda i, j: (0, j)),
                  pl.BlockSpec((D, tf), lambda i, j: (0, j)),
                  pl.BlockSpec((tf, D), lambda i, j: (j, 0))],
        out_specs=pl.BlockSpec((tm, D), lambda i, j: (i, 0)),
        scratch_shapes=[pltpu.VMEM((tm, D), F32)],
        compiler_params=_params(("parallel", "arbitrary")),
        name="dense_swiglu",
    )(hin, x1, mod3, gpost, wg, wu, wd)


MOE_TILE = 512
MOE_TF = 896
CMB_TILE = 128
SLAB = CMB_TILE + 8


def _moe_plan(sel, ntok):
    i32 = jnp.int32
    e1, e2 = sel[:, 0].astype(i32), sel[:, 1].astype(i32)
    ar = jnp.arange(ntok, dtype=i32)
    ex = jnp.arange(N_EXPERTS, dtype=i32)
    oh = ((e1[:, None] == ex) | (e2[:, None] == ex)).astype(i32)
    csum = jnp.cumsum(oh, axis=0)
    total = csum[-1]
    gsz = (total + MOE_TILE - 1) // MOE_TILE * MOE_TILE
    gend = jnp.cumsum(gsz)
    gstart = gend - gsz
    dstart = jnp.cumsum(total) - total
    key, wts = lax.sort((jnp.concatenate([e1 * ntok + ar, e2 * ntok + ar]),
                         jnp.concatenate([sel[:, 2], sel[:, 3]])), num_keys=1)
    tok_d = key % ntok
    n_pad = 2 * ntok + (N_EXPERTS + 1) * MOE_TILE
    p = jnp.arange(n_pad, dtype=i32)
    ge = jnp.minimum(jnp.sum((p[:, None] >= gend[None, :]).astype(i32), axis=1), N_EXPERTS - 1)
    pick = lambda tab: jnp.sum(jnp.where(ge[:, None] == ex, tab[None, :], 0), axis=1)
    q = p - pick(gstart)
    live = (q < pick(total)) & (p < gend[-1])
    src = jnp.clip(pick(dstart) + q, 0, 2 * ntok - 1)
    tok_p = jnp.where(live, tok_d[src], 0)
    w_p = jnp.where(live, wts[src], 0.0)
    tstart = jnp.arange(n_pad // MOE_TILE, dtype=i32) * MOE_TILE
    texp = jnp.minimum(jnp.sum((tstart[:, None] >= gend[None, :]).astype(i32), axis=1), N_EXPERTS - 1)
    tval = (tstart < gend[-1]).astype(i32)
    r0 = (csum - oh)[::CMB_TILE]
    r1 = jnp.concatenate([r0[1:], total[None]], axis=0)
    alo = (gstart[None, :] + r0).reshape(-1)
    ahi = (gstart[None, :] + r1).reshape(-1)
    return dict(tok=tok_p, w=w_p.reshape(n_pad, 1), texp=texp, tval=tval,
                a8=alo // 8, alo=alo, ahi=ahi, n_pad=n_pad)


def _moe_gather_body(tok_ref, src_ref, o_ref, buf_s):
    base = pl.program_id(0) * MOE_TILE

    def gather(r, carry):
        buf_s[pl.ds(r, 1), :] = src_ref[pl.ds(tok_ref[base + r], 1), :]
        return carry

    lax.fori_loop(0, MOE_TILE, gather, 0, unroll=8)
    o_ref[...] = buf_s[...].astype(BF16)


def _moe_gather_call(src, plan):
    n_pad = plan['n_pad']
    width = src.shape[1]
    grid_spec = pltpu.PrefetchScalarGridSpec(
        num_scalar_prefetch=1,
        grid=(n_pad // MOE_TILE,),
        in_specs=[pl.BlockSpec(memory_space=pltpu.VMEM)],
        out_specs=pl.BlockSpec((MOE_TILE, width), lambda i, tk: (i, 0)),
        scratch_shapes=[pltpu.VMEM((MOE_TILE, width), F32)])
    return pl.pallas_call(
        _moe_gather_body,
        out_shape=jax.ShapeDtypeStruct((n_pad, width), BF16),
        grid_spec=grid_spec,
        compiler_params=_params(("arbitrary",)),
        name="moe_row_gather",
    )(plan['tok'], src)


def _moe_group_body(texp_ref, tval_ref, xa_ref, xb_ref, ws_ref, wg_ref, wu_ref, wd_ref, ys_ref, acc_s):
    del texp_ref
    i = pl.program_id(0)
    j = pl.program_id(1)
    half = D // 2

    @pl.when(j == 0)
    def _():
        acc_s[...] = jnp.zeros_like(acc_s)

    @pl.when(tval_ref[i] > 0)
    def _():
        xa, xb = xa_ref[...], xb_ref[...]
        mm = lambda u, ref, rows: jnp.dot(u, ref[0, rows, :], preferred_element_type=F32)
        hg = mm(xa, wg_ref, slice(0, half)) + mm(xb, wg_ref, slice(half, D))
        hu = mm(xa, wu_ref, slice(0, half)) + mm(xb, wu_ref, slice(half, D))
        acc_s[...] += _dot(hg * jax.nn.sigmoid(hg) * hu, wd_ref[0])

    @pl.when(j == pl.num_programs(1) - 1)
    def _():
        ys_ref[...] = acc_s[...] * ws_ref[...]


def _moe_group_call(xa, xb, plan, wg, wu, wd):
    n_pad = plan['n_pad']
    dff = wg.shape[2]
    xspec = pl.BlockSpec((MOE_TILE, D // 2), lambda i, j, te, tv: (i, 0))
    grid_spec = pltpu.PrefetchScalarGridSpec(
        num_scalar_prefetch=2,
        grid=(n_pad // MOE_TILE, dff // MOE_TF),
        in_specs=[xspec, xspec,
                  pl.BlockSpec((MOE_TILE, 1), lambda i, j, te, tv: (i, 0)),
                  pl.BlockSpec((1, D, MOE_TF), lambda i, j, te, tv: (te[i], 0, j)),
                  pl.BlockSpec((1, D, MOE_TF), lambda i, j, te, tv: (te[i], 0, j)),
                  pl.BlockSpec((1, MOE_TF, D), lambda i, j, te, tv: (te[i], j, 0))],
        out_specs=pl.BlockSpec((MOE_TILE, D), lambda i, j, te, tv: (i, 0)),
        scratch_shapes=[pltpu.VMEM((MOE_TILE, D), F32)])
    return pl.pallas_call(
        _moe_group_body,
        out_shape=jax.ShapeDtypeStruct((n_pad, D), F32),
        grid_spec=grid_spec,
        compiler_params=_params(("parallel", "arbitrary")),
        name="moe_expert_swiglu",
    )(plan['texp'], plan['tval'], xa, xb, plan['w'], wg, wu, wd)


def _moe_combine_body(a8_ref, alo_ref, ahi_ref, *refs):
    ys_refs, tk_refs = refs[:N_EXPERTS], refs[N_EXPERTS:2 * N_EXPERTS]
    x1_ref, mod_ref, gpost_ref, o_ref = refs[2 * N_EXPERTS:]
    t = pl.program_id(0)
    tloc = t * CMB_TILE + lax.broadcasted_iota(jnp.int32, (SLAB, CMB_TILE), 1)
    srow = lax.broadcasted_iota(jnp.int32, (SLAB, 1), 0)
    f = jnp.zeros((CMB_TILE, D), F32)
    for e in range(N_EXPERTS):
        k = t * N_EXPERTS + e
        row = a8_ref[k] * 8 + srow
        tok = jnp.where((row >= alo_ref[k]) & (row < ahi_ref[k]), tk_refs[e][...], -1)
        onehot = jnp.where(tok == tloc, 1.0, 0.0).astype(BF16)
        f = f + _dot_rhs2(onehot, ys_refs[e][...], TN)
    m = mod_ref[0]
    o_ref[...] = x1_ref[...] + m[5:6] * _rms(f, gpost_ref[...])


def _moe_combine_call(ys, plan, x1, mod3, gpost, n_ctx, lat_t):
    ntok = x1.shape[0]
    midx = _mod_index(CMB_TILE, n_ctx, lat_t)

    def slab(w, e):
        return pl.BlockSpec((pl.Element(SLAB), pl.Element(w)),
                            lambda t, a8, lo, hi: (a8[t * N_EXPERTS + e] * 8, 0))

    experts = range(N_EXPERTS)
    grid_spec = pltpu.PrefetchScalarGridSpec(
        num_scalar_prefetch=3,
        grid=(ntok // CMB_TILE,),
        in_specs=[slab(D, e) for e in experts] + [slab(1, e) for e in experts] + [
            pl.BlockSpec((CMB_TILE, D), lambda t, a8, lo, hi: (t, 0)),
            pl.BlockSpec((1, 8, D), lambda t, a8, lo, hi: (midx(t), 0, 0)),
            pl.BlockSpec((1, D), lambda t, a8, lo, hi: (0, 0))],
        out_specs=pl.BlockSpec((CMB_TILE, D), lambda t, a8, lo, hi: (t, 0)))
    tok2d = plan['tok'].reshape(-1, 1)
    return pl.pallas_call(
        _moe_combine_body,
        out_shape=jax.ShapeDtypeStruct((ntok, D), F32),
        grid_spec=grid_spec,
        compiler_params=_params(("parallel",)),
        name="moe_combine",
    )(plan['a8'], plan['alo'], plan['ahi'], *([ys] * N_EXPERTS), *([tok2d] * N_EXPERTS), x1, mod3, gpost)


def _rope_tables(t):
    i = jnp.arange(M_HD)
    half, pair, f = i // 64, (i % 64) // 32, i % 32
    tt = jnp.arange(t)
    pos = jnp.stack([tt // GRID_W, tt % GRID_W], axis=-1).astype(F32)
    inv = ROPE_BASE ** (-f.astype(F32) / 32)
    ang = pos[:, half] * inv[None, :]
    sin = jnp.sin(ang)
    return (jnp.cos(ang), jnp.where(pair == 0, -sin, 0.0)[...], jnp.where(pair == 1, sin, 0.0))


def _pad_in_proj(w):
    off = {}
    o = 0
    for name, width in (('r', 512), ('k', 512), ('v', 512), ('wl', 128), ('al', 128), ('gl', 128),
                        ('mq', 512), ('mk', 512), ('mv', 512), ('mo', 512), ('mif', 16),
                        ('nq', 512), ('nk', 512), ('nv', 512), ('gate', 3072)):
        off[name] = (o, o + width)
        o += width
    col = lambda n: w[:, off[n][0]:off[n][1]]
    parts = [col(n) for n in ('r', 'k', 'v', 'mq', 'mk', 'mv', 'mo', 'nq', 'nk', 'nv', 'gate', 'wl', 'al', 'gl', 'mif')]
    parts.append(jnp.zeros((w.shape[0], LANE - 16), w.dtype))
    return jnp.concatenate(parts, axis=1).astype(BF16)


def _block_avg(width, group):
    i = jnp.arange(width) // group
    return jnp.where(i[:, None] == i[None, :], 1.0 / group, 0.0).astype(BF16)


def _zero_pad_rows(w2):
    z = jnp.zeros_like(w2[0])
    return jnp.stack([jnp.concatenate([w2[0], z], axis=0), jnp.concatenate([z, w2[1]], axis=0)])


def kernel(x_prompt, x_sample, state_rwkv, state_mlstm_C, state_mlstm_n, state_mlstm_m, cache_nat_k, cache_nat_v,
           c, c_ctx, w_mod, b_mod, g_pre_mix, g_post_mix, g_pre_ffn, g_post_ffn, w_in,
           rw_w0, rw_w2, rw_a0, rw_a2, rw_g2, rw_k_k, rw_k_a, rw_r_k, rw_ln_w, rw_ln_b,
           ml_b_i, ml_b_f, ml_norm_w, nat_rpb, w_o_rwkv, w_o_mlstm, w_o_nat, w_out,
           ff_w_gate, ff_w_up, ff_w_down, moe_w_router, moe_b_router, moe_w_gate, moe_w_up, moe_w_down):
    cb, ct = x_prompt.shape[:2]
    lb, lt = x_sample.shape[:2]
    n_ctx, n_lat = cb * ct, lb * lt
    x = jnp.concatenate([x_prompt.reshape(n_ctx, D), x_sample.reshape(n_lat, D)], axis=0)
    cvec = jnp.zeros((8, D), F32).at[0].set(c_ctx).at[1:1 + lb].set(c)
    rope = _rope_tables(lt)
    avg64, avg128 = _block_avg(HW, R_HD), _block_avg(HW, M_HD)
    s_rw = ml_st = kv_c = None
    for l in range(DEPTH):
        mod = _mod_call(cvec, w_mod[l], b_mod[l])
        mod3 = jnp.pad(mod[:1 + lb].reshape(1 + lb, 6, D), ((0, 0), (0, 2), (0, 0)))
        z = _proj_call(x, mod3, g_pre_mix[l], _pad_in_proj(w_in[l]), n_ctx, lt)

        rp = dict(w0=rw_w0[l].reshape(2, 1, HW), w2=_zero_pad_rows(rw_w2[l]),
                  a0=rw_a0[l].reshape(2, 1, HW), a2=_zero_pad_rows(rw_a2[l]),
                  kk=rw_k_k[l].reshape(1, HW), ka=rw_k_a[l].reshape(1, HW), rk=rw_r_k[l].reshape(1, HW))
        yr, bv, s_rw = _rwkv_call(z, rp, None, None, row0=0, nb=cb, t=ct, tb=RW_BLOCK, layer=l,
                                  prev_state=None if s_rw is None else (s_rw,))
        yr, bv = _rwkv_call(z, rp, state_rwkv[:, l], (yr, bv), row0=n_ctx, nb=lb, t=lt, tb=RW_BLOCK)

        gt = z[:, ZB_MIF * LANE:ZB_MIF * LANE + 16].reshape(-1, CHUNK, 16).transpose(0, 2, 1)
        hm, *ml_st = _mlstm_call(z, gt, None, ml_b_i[l], ml_b_f[l], None, None,
                                 row0=0, nb=cb, t=ct, tb=ML_BLOCK, layer=l, prev_state=ml_st)
        init = (state_mlstm_C[:, l], state_mlstm_n[:, l][:, :, :, None, :], state_mlstm_m[:, l])
        hm, = _mlstm_call(z, gt, rope, ml_b_i[l], ml_b_f[l], init, (hm,),
                          row0=n_ctx, nb=lb, t=lt, tb=ML_BLOCK)

        yn, *kv_c = _ctx_attn_call(z, kv_c, nb=cb, t=ct, layer=l)
        yn = _natten_call(z, cache_nat_k[:, l], cache_nat_v[:, l], _natten_table(nat_rpb[l]), yn,
                          row0=n_ctx, nb=lb, t=lt)

        mw = dict(avg64=avg64, avg128=avg128, rg2=rw_g2[l].astype(BF16), lnw=rw_ln_w[l].reshape(1, HW),
                  lnb=rw_ln_b[l].reshape(1, HW), mlw=ml_norm_w[l].reshape(1, HW),
                  wor=w_o_rwkv[l].astype(BF16), wom=w_o_mlstm[l].astype(BF16), won=w_o_nat[l].astype(BF16),
                  wout=w_out[l].astype(BF16), gpost=g_post_mix[l].reshape(1, D), gpre=g_pre_ffn[l].reshape(1, D))
        j = l // 2
        if l % 2 == 0:
            router = None
        else:
            wr = jnp.pad(moe_w_router[j], ((0, 0), (0, LANE - N_EXPERTS)))
            br = jnp.pad(moe_b_router[j], (0, LANE - N_EXPERTS), constant_values=NEG).reshape(1, LANE)
            router = (wr, br)
        merged = _merge_call(x, mod3, yr, bv, hm, yn, z, mw, n_ctx, lt, router)
        gpost = g_post_ffn[l].reshape(1, D)
        if l % 2 == 0:
            x = _ffn_call(merged[1], merged[0], mod3, gpost, ff_w_gate[j].astype(BF16),
                          ff_w_up[j].astype(BF16), ff_w_down[j].astype(BF16), n_ctx, lt)
        else:
            plan = _moe_plan(merged[2], n_ctx + n_lat)
            xa, xb = _moe_gather_call(merged[1][0], plan), _moe_gather_call(merged[1][1], plan)
            ys = _moe_group_call(xa, xb, plan, moe_w_gate[j].astype(BF16), moe_w_up[j].astype(BF16),
                                 moe_w_down[j].astype(BF16))
            x = _moe_combine_call(ys, plan, merged[0], mod3, gpost, n_ctx, lt)

    c_m, n_m, m_m = ml_st
    return (x[:n_ctx].reshape(cb, ct, D), x[n_ctx:].reshape(lb, lt, D),
            s_rw, c_m, n_m[:, :, :, :, 0, :], m_m[:, :, :, :, 0, 0], kv_c[0], kv_c[1])
```

```python
import functools
import math

import jax
import jax.numpy as jnp
from jax import lax
from jax.experimental import pallas as pl
from jax.experimental.pallas import tpu as pltpu

F32 = jnp.float32
BF16 = jnp.bfloat16

D = 1024
N_CTX_B, N_CTX_T = 32, 256
N_LAT_B, N_LAT_T = 2, 4096
DEPTH = 2
GRID_W = 64
R_HEADS, R_HD = 8, 64
M_HEADS, M_HD = 4, 128
N_HEADS, N_HD = 8, 64
HW = 512
WIN_ROWS, WIN_COLS = 8, 16
N_EXPERTS = 8
RWKV_GN_EPS = 64e-5
NORM_EPS = 1e-6
ROPE_BASE = 10000.0
CHUNK = 64
NEG = -1e30
LANE = 128
VMEM_LIMIT = 56 * 1024 * 1024

ZB_R, ZB_K, ZB_V = 0, 4, 8
ZB_MQ, ZB_MK, ZB_MV, ZB_MO = 12, 16, 20, 24
ZB_NQ, ZB_NK, ZB_NV = 28, 32, 36
ZB_GATE = 40
ZB_WL, ZB_AL, ZB_GL, ZB_MIF = 64, 65, 66, 67
P_PAD = 68 * LANE

NN = (((1,), (0,)), ((), ()))
NT = (((1,), (1,)), ((), ()))
TN = (((0,), (0,)), ((), ()))


def _dot(a, b, dims=NN):
    return lax.dot_general(a.astype(BF16), b.astype(BF16), dims, preferred_element_type=F32)


def _split(x):
    hi = x.astype(BF16)
    return hi, (x - hi.astype(F32)).astype(BF16)


def _dot3(a, b, dims=NN):
    ah, al = _split(a)
    bh, bl = _split(b)
    d = lambda u, w: lax.dot_general(u, w, dims, preferred_element_type=F32)
    return d(ah, bh) + (d(ah, bl) + d(al, bh))


def _dot_lhs2(a, b_exact, dims=NN):
    ah, al = _split(a)
    d = lambda u: lax.dot_general(u, b_exact, dims, preferred_element_type=F32)
    return d(ah) + d(al)


def _dot_rhs2(a_exact, b, dims=NN):
    bh, bl = _split(b)
    d = lambda w: lax.dot_general(a_exact, w, dims, preferred_element_type=F32)
    return d(bh) + d(bl)


def _log_sigmoid(x):
    return jnp.minimum(x, 0.0) - jnp.log1p(jnp.exp(-jnp.abs(x)))


def _rms(x, g):
    return x * lax.rsqrt(jnp.mean(x * x, axis=-1, keepdims=True) + NORM_EPS) * g


def _params(sem):
    return pltpu.CompilerParams(dimension_semantics=sem, vmem_limit_bytes=VMEM_LIMIT)


def _alias_prev(in_specs, args, prev, first_out=0):
    aliases = {}
    for k, a in enumerate(prev or ()):
        aliases[len(args)] = first_out + k
        in_specs.append(pl.BlockSpec(memory_space=pl.ANY))
        args.append(a)
    return aliases


def _mod_body(c_ref, w_ref, b_ref, o_ref):
    c = c_ref[...]
    o_ref[...] = _dot(c * jax.nn.sigmoid(c), w_ref[...]) + b_ref[...]


def _mod_call(cvec, w_mod, b_mod):
    tn = 1536
    return pl.pallas_call(
        _mod_body,
        out_shape=jax.ShapeDtypeStruct((8, 6 * D), F32),
        grid=(6 * D // tn,),
        in_specs=[pl.BlockSpec((8, D), lambda j: (0, 0)),
                  pl.BlockSpec((D, tn), lambda j: (0, j)),
                  pl.BlockSpec((1, tn), lambda j: (0, j))],
        out_specs=pl.BlockSpec((8, tn), lambda j: (0, j)),
        compiler_params=_params(("arbitrary",)),
        name="adaln_mod",
    )(cvec, w_mod, b_mod.reshape(1, 6 * D))


def _mod_index(tile_rows, n_ctx, lat_t):
    def f(i):
        start = i * tile_rows
        return jnp.where(start < n_ctx, 0, 1 + (start - n_ctx) // lat_t)
    return f


def _proj_body(x_ref, mod_ref, g_ref, w_ref, z_ref, hin_s):
    @pl.when(pl.program_id(1) == 0)
    def _():
        m = mod_ref[0]
        hin_s[...] = (_rms(x_ref[...], g_ref[...]) * (1.0 + m[1:2]) + m[0:1]).astype(BF16)
    z_ref[...] = jnp.dot(hin_s[...], w_ref[...], preferred_element_type=F32)


def _proj_call(x, mod3, g, w_bf, n_ctx, lat_t, tm=1024, tn=2176):
    ntok = x.shape[0]
    midx = _mod_index(tm, n_ctx, lat_t)
    return pl.pallas_call(
        _proj_body,
        out_shape=jax.ShapeDtypeStruct((ntok, P_PAD), F32),
        grid=(ntok // tm, P_PAD // tn),
        in_specs=[pl.BlockSpec((tm, D), lambda i, j: (i, 0)),
                  pl.BlockSpec((1, 8, D), lambda i, j: (midx(i), 0, 0)),
                  pl.BlockSpec((1, D), lambda i, j: (0, 0)),
                  pl.BlockSpec((D, tn), lambda i, j: (0, j))],
        out_specs=pl.BlockSpec((tm, tn), lambda i, j: (i, j)),
        scratch_shapes=[pltpu.VMEM((tm, D), BF16)],
        compiler_params=_params(("parallel", "arbitrary")),
        name="in_proj",
    )(x, mod3, g.reshape(1, D), w_bf)


RW_TILE = 2 * CHUNK
RW_BLOCK = 1024
ML_BLOCK = 512


def _rwkv_body(*refs, tb_rows, nseq, has_init, has_out, n_alias):
    (r_ref, k_ref, v_ref, wl_ref, al_ref, w0_ref, w2_ref, a0_ref, a2_ref,
     kkw_ref, kaw_ref, rkw_ref, hsum_ref) = refs[:13]
    rest = list(refs[13:])
    s0_ref = rest.pop(0) if has_init else None
    del rest[:n_alias]
    y_ref, bv_ref = rest.pop(0), rest.pop(0)
    sout_ref = rest.pop(0) if has_out else None
    r2_s, m_s, g_s, st_s = rest
    C = CHUNK
    TL = RW_TILE
    nchunk = tb_rows // nseq // C
    d = pl.program_id(2)
    tb = pl.program_id(3)
    n_tb = pl.num_programs(3)

    ti = lax.broadcasted_iota(jnp.int32, (TL, TL), 0)
    si = lax.broadcasted_iota(jnp.int32, (TL, TL), 1)
    head_blk = (ti >> 6) == (si >> 6)

    @pl.when(tb == 0)
    def _():
        for q in range(nseq):
            if has_init:
                s0 = s0_ref[q, 0]
                two = jnp.concatenate([jnp.concatenate([s0[0], s0[0]], axis=1),
                                       jnp.concatenate([s0[1], s0[1]], axis=1)], axis=0)
                st_s[q] = jnp.where(head_blk, two, 0.0)
            else:
                st_s[q] = jnp.zeros((LANE, LANE), F32)

    hsum = hsum_ref[...]
    r = r_ref[...]
    kf = k_ref[...]
    v = v_ref[...]
    w_pre = w0_ref[0] + _dot(jnp.tanh(wl_ref[...]), w2_ref[0])
    lw = -math.exp(-0.5) * jax.nn.sigmoid(w_pre)
    a = jax.nn.sigmoid(a0_ref[0] + _dot(al_ref[...], a2_ref[0]))
    kd = kf * (1.0 + (a - 1.0) * kaw_ref[...])
    kkf = kf * kkw_ref[...]
    kn = kkf * lax.rsqrt(_dot_lhs2(kkf * kkf, hsum) + 1e-12)
    bd = kn * a
    bv_ref[0] = _dot_lhs2(r * kd * rkw_ref[...], hsum) * v

    dlt = (ti - si) * (1 - 2 * d)
    strict = head_blk & (dlt > 0)
    incl = head_blk & (dlt >= 0)
    cum_ones = jnp.concatenate([jnp.where(incl, 1.0, 0.0), jnp.where(head_blk, 1.0, 0.0)], axis=0).astype(BF16)
    is_diag = ti == si
    eye = jnp.where(is_diag, 1.0, 0.0)
    pair = (ti >> 1) == (si >> 1)
    sibling = [((ti >> lg) ^ (si >> lg)) == 1 for lg in range(1, 6)]
    head0 = si < R_HD

    n_tiles = tb_rows // TL
    tiles = []
    for tile in range(n_tiles):
        rows = slice(TL * tile, TL * (tile + 1))
        cs = _dot_rhs2(cum_ones, lw[rows])
        cum, tot = cs[:TL], cs[TL:]
        e_neg = jnp.exp(-cum)
        e_hat = jnp.exp(tot - cum)
        rt = r[rows] * jnp.exp(cum)
        at = -kn[rows] * jnp.exp(cum - lw[rows])
        tiles.append(dict(rows=rows, tot=tot, rt=rt, at=at, bh=bd[rows] * e_hat, kh=kd[rows] * e_hat,
                          rhs=jnp.concatenate([bd[rows] * e_neg, kd[rows] * e_neg], axis=0).astype(BF16),
                          at_bf=at.astype(BF16), v_bf=v[rows].astype(BF16)))
    chains = [(tl, h) for tl in tiles for h in range(2)]
    a_all = []
    for tl, h in chains:
        hm = head0 if h == 0 else ~head0
        lhs = jnp.concatenate([jnp.where(hm, tl['at'], 0.0), jnp.where(hm, tl['rt'], 0.0)], axis=0)
        a_all.append(_dot(lhs, tl['rhs'], NT))
    n_mat = [jnp.where(strict, a[:TL, :TL], 0.0) for a in a_all]
    a_ak = [jnp.where(strict, a[:TL, TL:], 0.0).astype(BF16) for a in a_all]
    a_rb = [jnp.where(incl, a[TL:, :TL], 0.0).astype(BF16) for a in a_all]
    a_rk = [jnp.where(incl, a[TL:, TL:], 0.0).astype(BF16) for a in a_all]
    t_inv = [eye + jnp.where(pair, n, 0.0) for n in n_mat]
    for sib in sibling:
        x_mat = [_dot(jnp.where(sib, n, 0.0), t) for n, t in zip(n_mat, t_inv)]
        t_inv = [t + _dot(t, x) for t, x in zip(t_inv, x_mat)]
    p_h = [_dot(t, tl['at_bf']) for t, (tl, _) in zip(t_inv, chains)]
    av = [_dot(a, tl['v_bf']) for a, (tl, _) in zip(a_ak, chains)]
    q_h = [_dot(t, x) for t, x in zip(t_inv, av)]
    r2_h = [_dot(a, p) for a, p in zip(a_rb, p_h)]
    y0_h = [_dot(a, q) + _dot(k_, tl['v_bf']) for a, q, k_, (tl, _) in zip(a_rb, q_h, a_rk, chains)]
    p_m, q_m = [], []
    for i, tl in enumerate(tiles):
        rows = tl['rows']
        pick = lambda u: jnp.where(head0, u[2 * i], u[2 * i + 1])
        p_m.append(pick(p_h))
        q_m.append(pick(q_h))
        r2_s[rows, :] = tl['rt'] + pick(r2_h)
        y_ref[0, rows, :] = pick(y0_h)
        for c in range(TL // C):
            cr = slice(C * c, C * (c + 1))
            decay = jnp.where(is_diag, jnp.exp(tl['tot'][C * c:C * c + 1]), 0.0)
            m_s[i * (TL // C) + c] = decay + jnp.where(head_blk, _dot(p_m[i][cr], tl['bh'][cr], TN), 0.0)
            g_s[i * (TL // C) + c] = jnp.where(
                head_blk, _dot(q_m[i][cr], tl['bh'][cr], TN) + _dot(v[rows][cr], tl['kh'][cr], TN), 0.0)

    def step(i, states):
        ci = jnp.where(d == 0, i, nchunk - 1 - i)
        new = []
        for q, s in enumerate(states):
            cq = q * nchunk + ci
            rows = pl.ds(pl.multiple_of(cq * C, C), C)
            y_ref[0, rows, :] = y_ref[0, rows, :] + _dot(r2_s[rows, :], s, NT)
            new.append(_dot3(s, m_s[cq]) + g_s[cq])
        return tuple(new)

    s_fin = lax.fori_loop(0, nchunk, step, tuple(st_s[q] for q in range(nseq)))
    for q in range(nseq):
        st_s[q] = s_fin[q]

    if has_out:
        @pl.when(tb == n_tb - 1)
        def _():
            for q in range(nseq):
                sout_ref[q, 0, 0, 0] = s_fin[q][:R_HD, :R_HD]
                sout_ref[q, 0, 0, 1] = s_fin[q][R_HD:, R_HD:]


def _rwkv_call(z, P, s0, prev, *, row0, nb, t, tb, layer=None, prev_state=None):
    nseq = max(tb // t, 1)
    n_tb = max(t // tb, 1)
    blk0 = row0 // tb
    has_init = s0 is not None

    def rowblk(b, d, i):
        return blk0 + b * n_tb + jnp.where(d == 0, i, n_tb - 1 - i)

    def zspec(cb):
        return pl.BlockSpec((tb, LANE), lambda b, hp, d, i: (rowblk(b, d, i), cb(hp)))

    vec = lambda: pl.BlockSpec((1, LANE), lambda b, hp, d, i: (0, hp))
    dvec = lambda: pl.BlockSpec((1, 1, LANE), lambda b, hp, d, i: (d, 0, hp))
    dmat = lambda: pl.BlockSpec((1, LANE, LANE), lambda b, hp, d, i: (d, 0, hp))
    st_spec = pl.BlockSpec((nseq, 1, 2, R_HD, R_HD), lambda b, hp, d, i: (b, d, hp, 0, 0))
    in_specs = [zspec(lambda hp: ZB_R + hp), zspec(lambda hp: ZB_K + hp), zspec(lambda hp: ZB_V + hp),
                zspec(lambda hp: ZB_WL), zspec(lambda hp: ZB_AL),
                dvec(), dmat(), dvec(), dmat(), vec(), vec(), vec(),
                pl.BlockSpec((LANE, LANE), lambda b, hp, d, i: (0, 0))]
    args = [z, z, z, z, z, P['w0'], P['w2'], P['a0'], P['a2'], P['kk'], P['ka'], P['rk'],
            _block_avg(LANE, R_HD) * R_HD]
    if has_init:
        in_specs.append(st_spec)
        args.append(s0)
    has_out = layer is not None
    aliases = _alias_prev(in_specs, args, prev)
    aliases.update(_alias_prev(in_specs, args, prev_state, first_out=2))
    yspec = pl.BlockSpec((1, tb, LANE), lambda b, hp, d, i: (d, rowblk(b, d, i), hp))
    out_shape = [jax.ShapeDtypeStruct((2, z.shape[0], HW), F32)] * 2
    out_specs = [yspec, yspec]
    if has_out:
        out_shape.append(jax.ShapeDtypeStruct((nb, DEPTH, 2, R_HEADS, R_HD, R_HD), F32))
        out_specs.append(pl.BlockSpec((nseq, 1, 1, 2, R_HD, R_HD), lambda b, hp, d, i: (b, layer, d, hp, 0, 0)))
    return pl.pallas_call(
        functools.partial(_rwkv_body, tb_rows=tb, nseq=nseq, has_init=has_init, has_out=has_out,
                          n_alias=len(aliases)),
        out_shape=out_shape,
        grid=(nb // nseq, R_HEADS // 2, 2, n_tb),
        in_specs=in_specs,
        out_specs=out_specs,
        input_output_aliases=aliases,
        scratch_shapes=[pltpu.VMEM((tb, LANE), F32), pltpu.VMEM((tb // CHUNK, LANE, LANE), F32),
                        pltpu.VMEM((tb // CHUNK, LANE, LANE), F32), pltpu.VMEM((nseq, LANE, LANE), F32)],
        compiler_params=_params(("parallel", "parallel", "arbitrary", "arbitrary")),
        name="rwkv7_chunked",
    )(*args)


def _mlstm_body(*refs, tb_rows, nseq, use_rope, has_init, has_out, n_alias):
    refs = list(refs)
    q_ref, k_ref, v_ref, g_ref, gt_ref = [refs.pop(0) for _ in range(5)]
    cos_ref, sa_ref, sb_ref = [refs.pop(0) for _ in range(3)] if use_rope else (None,) * 3
    bi_ref, bf_ref = refs.pop(0), refs.pop(0)
    c0_ref, n0_ref, m0_ref = [refs.pop(0) for _ in range(3)] if has_init else (None,) * 3
    del refs[:n_alias]
    h_ref = refs.pop(0)
    cout_ref, nout_ref, mout_ref = [refs.pop(0) for _ in range(3)] if has_out else (None,) * 3
    sc_s, kv_s, nk_s, cprev_s, nprev_s, c_s, n_s, m_s = refs
    L = CHUNK
    nck = tb_rows // L
    ncs = nck // nseq
    b = pl.program_id(0)
    h = pl.program_id(1)
    d = pl.program_id(2)
    tb = pl.program_id(3)
    n_tb = pl.num_programs(3)
    bi = bi_ref[d, h]
    bf = bf_ref[d, h]
    ji = d * (2 * M_HEADS) + h
    jf = ji + M_HEADS

    @pl.when(tb == 0)
    def _():
        for q in range(nseq):
            if has_init:
                c_s[q] = c0_ref[q, 0, 0]
                n_s[q] = n0_ref[q, 0, 0]
                m_s[q] = jnp.full((1, M_HD), m0_ref[b * nseq + q, d, h], F32)
            else:
                c_s[q] = jnp.zeros((M_HD, M_HD), F32)
                n_s[q] = jnp.zeros((1, M_HD), F32)
                m_s[q] = jnp.zeros((1, M_HD), F32)

    g = g_ref[...]
    lane = lax.broadcasted_iota(jnp.int32, g.shape, 1)
    icol_all = jnp.sum(jnp.where(lane == ji, g, 0.0), axis=1, keepdims=True) + bi
    fcol_all = _log_sigmoid(jnp.sum(jnp.where(lane == jf, g, 0.0), axis=1, keepdims=True) + bf)

    ti = lax.broadcasted_iota(jnp.int32, (L, L), 0)
    si = lax.broadcasted_iota(jnp.int32, (L, L), 1)
    sgn = 1 - 2 * d
    incl = (ti - si) * sgn >= 0
    incl_t = (si - ti) * sgn >= 0
    row128 = lambda x: jnp.broadcast_to(x, (1, M_HD))

    ch = []
    for c in range(nck):
        rows = slice(L * c, L * (c + 1))
        irow = gt_ref[c, pl.ds(ji, 1), :] + bi
        frow = _log_sigmoid(gt_ref[c, pl.ds(jf, 1), :] + bf)
        bsum_col = jnp.sum(jnp.where(incl, frow, 0.0), axis=1, keepdims=True)
        bsum_row = jnp.sum(jnp.where(incl_t, fcol_all[rows], 0.0), axis=0, keepdims=True)
        blast = jnp.sum(frow, axis=1, keepdims=True)
        wlog = blast - bsum_col + icol_all[rows]
        dmat = jnp.where(incl, bsum_col - bsum_row + irow, NEG)
        sc_s[c, 0:1, :] = row128(blast)
        sc_s[c, 1:2, :] = row128(jnp.max(wlog, axis=0, keepdims=True))
        ch.append(dict(rows=rows, bsum_col=bsum_col, blast=blast, wlog=wlog, dmat=dmat,
                       dmax=jnp.max(dmat, axis=1, keepdims=True)))

    def m_step(i, ms):
        ci = jnp.where(d == 0, i, ncs - 1 - i)
        new = []
        for q, m in enumerate(ms):
            cq = q * ncs + ci
            m_new = jnp.maximum(sc_s[cq, 0:1, :] + m, sc_s[cq, 1:2, :])
            sc_s[cq, 2:3, :] = m
            sc_s[cq, 3:4, :] = m_new
            new.append(m_new)
        return tuple(new)

    m_fin = lax.fori_loop(0, ncs, m_step, tuple(m_s[q] for q in range(nseq)))

    for c, cd in enumerate(ch):
        rows = cd['rows']
        qc, kc = q_ref[rows, :], k_ref[rows, :]
        if use_rope:
            cos, sa, sb = cos_ref[rows, :], sa_ref[rows, :], sb_ref[rows, :]
            qc = qc * cos + pltpu.roll(qc, 96, 1) * sa + pltpu.roll(qc, 32, 1) * sb
            kc = kc * cos + pltpu.roll(kc, 96, 1) * sa + pltpu.roll(kc, 32, 1) * sb
        cd['q'] = qc
        cd['k'] = kc * (M_HD ** -0.5)
        cd['v'] = v_ref[rows, :]
        cd['m_prev'] = sc_s[c, 2:3, 0:1]
        cd['m_new'] = sc_s[c, 3:4, 0:1]
    for cd in ch:
        cd['qk'] = _dot(cd['q'], cd['k'], NT)
    for c, cd in enumerate(ch):
        kw = cd['k'] * jnp.exp(cd['wlog'] - cd['m_new'])
        kv_s[c] = _dot3(kw, cd['v'], TN)
        nk_s[c] = jnp.sum(kw, axis=0, keepdims=True)
        sc_s[c, 4:5, :] = row128(jnp.exp(cd['blast'] + cd['m_prev'] - cd['m_new']))
    for cd in ch:
        inter = cd['bsum_col'] + cd['m_prev']
        cd['mt'] = jnp.maximum(inter, cd['dmax'])
        cd['iw'] = jnp.exp(inter - cd['mt'])
        sm = cd['qk'] * jnp.exp(cd['dmat'] - cd['mt'])
        cd['ssum'] = jnp.sum(sm, axis=1, keepdims=True)
        cd['sv'] = _dot(sm, cd['v'])

    def c_step(i, carry):
        ci = jnp.where(d == 0, i, ncs - 1 - i)
        new = []
        for q, (c_st, n_st) in enumerate(carry):
            cq = q * ncs + ci
            dec = sc_s[cq, 4:5, 0:1]
            cprev_s[cq] = c_st
            nprev_s[cq] = n_st
            new.append((dec * c_st + kv_s[cq], dec * n_st + nk_s[cq]))
        return tuple(new)

    st_fin = lax.fori_loop(0, ncs, c_step, tuple((c_s[q], n_s[q]) for q in range(nseq)))
    for q in range(nseq):
        c_s[q], n_s[q] = st_fin[q]
        m_s[q] = m_fin[q]

    qc_prev = [_dot(cd['q'], cprev_s[c]) for c, cd in enumerate(ch)]
    for c, cd in enumerate(ch):
        num = cd['iw'] * qc_prev[c] + cd['sv']
        den = cd['iw'] * jnp.sum(cd['q'] * nprev_s[c], axis=1, keepdims=True) + cd['ssum']
        h_ref[0, cd['rows'], :] = num / jnp.maximum(jnp.abs(den), jnp.exp(-cd['mt']))

    if has_out:
        @pl.when(tb == n_tb - 1)
        def _():
            for q in range(nseq):
                cout_ref[q, 0, 0, 0] = st_fin[q][0]
                nout_ref[q, 0, 0, 0] = st_fin[q][1]
                mout_ref[q, 0, 0, 0] = m_fin[q]


def _mlstm_call(z, gt, rope, b_i, b_f, init, prev, *, row0, nb, t, tb, layer=None, prev_state=None):
    nseq = max(tb // t, 1)
    n_tb = max(t // tb, 1)
    blk0 = row0 // tb
    cpb = tb // CHUNK
    use_rope = rope is not None
    has_init = init is not None

    def rowblk(b, d, i):
        return blk0 + b * n_tb + jnp.where(d == 0, i, n_tb - 1 - i)

    def zspec(cb):
        return pl.BlockSpec((tb, LANE), lambda b, h, d, i: (rowblk(b, d, i), cb(h)))

    smem = pl.BlockSpec(memory_space=pltpu.SMEM)
    in_specs = [zspec(lambda h: ZB_MQ + h), zspec(lambda h: ZB_MK + h), zspec(lambda h: ZB_MV + h),
                zspec(lambda h: ZB_MIF),
                pl.BlockSpec((cpb, 16, CHUNK), lambda b, h, d, i: (rowblk(b, d, i), 0, 0))]
    args = [z, z, z, z, gt]
    if use_rope:
        tspec = pl.BlockSpec((tb, LANE), lambda b, h, d, i: (jnp.where(d == 0, i, n_tb - 1 - i), 0))
        in_specs += [tspec] * 3
        args += list(rope)
    in_specs += [smem, smem]
    args += [b_i, b_f]
    cspec = pl.BlockSpec((nseq, 1, 1, M_HD, M_HD), lambda b, h, d, i: (b, d, h, 0, 0))
    nspec = pl.BlockSpec((nseq, 1, 1, 1, M_HD), lambda b, h, d, i: (b, d, h, 0, 0))
    if has_init:
        in_specs += [cspec, nspec, smem]
        args += list(init)
    has_out = layer is not None
    aliases = _alias_prev(in_specs, args, prev)
    aliases.update(_alias_prev(in_specs, args, prev_state, first_out=1))
    out_shape = [jax.ShapeDtypeStruct((2, z.shape[0], HW), F32)]
    out_specs = [pl.BlockSpec((1, tb, LANE), lambda b, h, d, i: (d, rowblk(b, d, i), h))]
    if has_out:
        vec_shape = jax.ShapeDtypeStruct((nb, DEPTH, 2, M_HEADS, 1, M_HD), F32)
        out_shape += [jax.ShapeDtypeStruct((nb, DEPTH, 2, M_HEADS, M_HD, M_HD), F32), vec_shape, vec_shape]
        vec_spec = pl.BlockSpec((nseq, 1, 1, 1, 1, M_HD), lambda b, h, d, i: (b, layer, d, h, 0, 0))
        out_specs += [pl.BlockSpec((nseq, 1, 1, 1, M_HD, M_HD), lambda b, h, d, i: (b, layer, d, h, 0, 0)),
                      vec_spec, vec_spec]
    return pl.pallas_call(
        functools.partial(_mlstm_body, tb_rows=tb, nseq=nseq, use_rope=use_rope, has_init=has_init,
                          has_out=has_out, n_alias=len(aliases)),
        out_shape=out_shape,
        grid=(nb // nseq, M_HEADS, 2, n_tb),
        in_specs=in_specs,
        out_specs=out_specs,
        input_output_aliases=aliases,
        scratch_shapes=[pltpu.VMEM((cpb, 8, M_HD), F32),
                        pltpu.VMEM((cpb, M_HD, M_HD), F32), pltpu.VMEM((cpb, 1, M_HD), F32),
                        pltpu.VMEM((cpb, M_HD, M_HD), F32), pltpu.VMEM((cpb, 1, M_HD), F32),
                        pltpu.VMEM((nseq, M_HD, M_HD), F32), pltpu.VMEM((nseq, 1, M_HD), F32),
                        pltpu.VMEM((nseq, 1, M_HD), F32)],
        compiler_params=_params(("parallel", "parallel", "arbitrary", "arbitrary")),
        name="mlstm_chunkwise",
    )(*args)


def _ctx_attn_body(q_ref, k_ref, v_ref, *refs):
    y_ref, ko_ref, vo_ref = refs[-3:]
    q = q_ref[...] * (N_HD ** -0.5)
    k = k_ref[...]
    v = v_ref[...]
    heads = [slice(N_HD * h, N_HD * (h + 1)) for h in range(2)]
    for h, sl in enumerate(heads):
        ko_ref[0, 0, h] = k[:, sl]
        vo_ref[0, 0, h] = v[:, sl]
    s = [_dot(q[:, sl], k[:, sl], NT) for sl in heads]
    p = [jnp.exp(s_ - jnp.max(s_, axis=-1, keepdims=True)) for s_ in s]
    o = [_dot(p_, v[:, sl]) for p_, sl in zip(p, heads)]
    for o_, p_, sl in zip(o, p, heads):
        y_ref[:, sl] = o_ / jnp.sum(p_, axis=-1, keepdims=True)


def _ctx_attn_call(z, prev_kv, *, nb, t, layer):
    zspec = lambda cb: pl.BlockSpec((t, LANE), lambda b, hp: (b, cb + hp))
    kv_spec = pl.BlockSpec((1, 1, 2, t, N_HD), lambda b, hp: (b, layer, hp, 0, 0))
    kv_shape = jax.ShapeDtypeStruct((nb, DEPTH, N_HEADS, t, N_HD), F32)
    in_specs = [zspec(ZB_NQ), zspec(ZB_NK), zspec(ZB_NV)]
    args = [z, z, z]
    aliases = _alias_prev(in_specs, args, prev_kv, first_out=1)
    return pl.pallas_call(
        _ctx_attn_body,
        out_shape=[jax.ShapeDtypeStruct((z.shape[0], HW), F32), kv_shape, kv_shape],
        grid=(nb, N_HEADS // 2),
        in_specs=in_specs,
        out_specs=[pl.BlockSpec((t, LANE), lambda b, hp: (b, hp)), kv_spec, kv_spec],
        input_output_aliases=aliases,
        compiler_params=_params(("parallel", "parallel")),
        name="context_attention",
    )(*args)


def _natten_body(q_ref, k_ref, v_ref, kc_ref, vc_ref, tab_ref, prev_ref, y_ref, *, rows_per_step, rows_n):
    del prev_ref
    rb = pl.program_id(2)
    n_loc = WIN_ROWS * GRID_W

    unroll = 4

    def rows(it, carry):
        work = []
        for u in range(unroll):
            i = it * unroll + u
            r = rb * rows_per_step + i
            rstart = jnp.clip(r - WIN_ROWS // 2, 0, rows_n - WIN_ROWS)
            qrows = pl.ds(pl.multiple_of(i * GRID_W, GRID_W), GRID_W)
            krows = pl.ds(pl.multiple_of(rstart * GRID_W, GRID_W), n_loc)
            work += [(qrows, krows, r - rstart, h, slice(N_HD * h, N_HD * (h + 1))) for h in range(2)]
        q = [q_ref[qr, sl] * (N_HD ** -0.5) for qr, _, _, _, sl in work]
        s_loc = [_dot(q_, k_ref[kr, sl], NT) + tab_ref[h, var] for q_, (_, kr, var, h, sl) in zip(q, work)]
        s_ctx = [_dot(q_, kc_ref[0, h], NT) for q_, (_, _, _, h, _) in zip(q, work)]
        m = [jnp.maximum(jnp.max(a, axis=-1, keepdims=True), jnp.max(c, axis=-1, keepdims=True))
             for a, c in zip(s_loc, s_ctx)]
        p_loc = [jnp.exp(a - m_) for a, m_ in zip(s_loc, m)]
        p_ctx = [jnp.exp(c - m_) for c, m_ in zip(s_ctx, m)]
        o = [_dot(a, v_ref[kr, sl]) + _dot(c, vc_ref[0, h])
             for a, c, (_, kr, _, h, sl) in zip(p_loc, p_ctx, work)]
        for o_, a, c, (qr, _, _, _, sl) in zip(o, p_loc, p_ctx, work):
            y_ref[qr, sl] = o_ / (jnp.sum(a, axis=-1, keepdims=True) + jnp.sum(c, axis=-1, keepdims=True))
        return carry

    lax.fori_loop(0, rows_per_step // unroll, rows, 0)


def _natten_call(z, k_ctx, v_ctx, tab, prev, *, row0, nb, t, rows_per_step=8):
    rows_n = t // GRID_W
    tq = rows_per_step * GRID_W
    qblk0 = row0 // tq
    sblk0 = row0 // t
    n_rb = rows_n // rows_per_step
    past = k_ctx.shape[2]
    kv_spec = lambda cb: pl.BlockSpec((t, LANE), lambda b, hp, rb: (sblk0 + b, cb + hp))
    cache_spec = pl.BlockSpec((1, 2, past, N_HD), lambda b, hp, rb: (b, hp, 0, 0))
    return pl.pallas_call(
        functools.partial(_natten_body, rows_per_step=rows_per_step, rows_n=rows_n),
        out_shape=jax.ShapeDtypeStruct((z.shape[0], HW), F32),
        grid=(nb, N_HEADS // 2, n_rb),
        in_specs=[pl.BlockSpec((tq, LANE), lambda b, hp, rb: (qblk0 + b * n_rb + rb, ZB_NQ + hp)),
                  kv_spec(ZB_NK), kv_spec(ZB_NV), cache_spec, cache_spec,
                  pl.BlockSpec((2, WIN_ROWS, GRID_W, WIN_ROWS * GRID_W), lambda b, hp, rb: (hp, 0, 0, 0)),
                  pl.BlockSpec(memory_space=pl.ANY)],
        out_specs=pl.BlockSpec((tq, LANE), lambda b, hp, rb: (qblk0 + b * n_rb + rb, hp)),
        input_output_aliases={6: 0},
        compiler_params=_params(("parallel", "parallel", "arbitrary")),
        name="neighbourhood_attention",
    )(z, z, z, k_ctx, v_ctx, tab, prev)


def _natten_table(rpb):
    cq = jnp.arange(GRID_W)[:, None]
    ck = jnp.arange(GRID_W)[None, :]
    dc = jnp.clip(ck - cq, -(WIN_COLS - 1), WIN_COLS - 1) + (WIN_COLS - 1)
    cstart = jnp.clip(cq - WIN_COLS // 2, 0, GRID_W - WIN_COLS)
    in_band = (ck >= cstart) & (ck < cstart + WIN_COLS)
    onehot = ((dc[None] == jnp.arange(2 * WIN_COLS - 1)[:, None, None]) & in_band[None]).astype(F32)
    cols = jnp.einsum('hrd,dqk->hrqk', rpb, onehot, precision=lax.Precision.HIGHEST)
    cols = cols + jnp.where(in_band, 0.0, NEG)
    tab = jnp.stack([cols[:, WIN_ROWS - 1 - var:2 * WIN_ROWS - 1 - var] for var in range(WIN_ROWS)], axis=1)
    return tab.transpose(0, 1, 3, 2, 4).reshape(rpb.shape[0], WIN_ROWS, GRID_W, WIN_ROWS * GRID_W)


def _group_norm(y, avg_bf, eps):
    mu = _dot_lhs2(y, avg_bf)
    yc = y - mu
    var = _dot_lhs2(yc * yc, avg_bf)
    return yc * lax.rsqrt(var + eps)


def _merge_body(*refs, with_router):
    refs = list(refs)
    (x_ref, mod_ref, yr_ref, bv_ref, hm_ref, yn_ref, gl_ref, mo_ref, g0_ref, g1_ref, g2_ref,
     avg64_ref, avg128_ref, rg2_ref, lnw_ref, lnb_ref, mlw_ref, wor_ref, wom_ref, won_ref, wout_ref,
     gpost_ref, gpre_ref) = [refs.pop(0) for _ in range(23)]
    wr_ref, br_ref = (refs.pop(0), refs.pop(0)) if with_router else (None, None)
    x1_ref, hin_ref = refs.pop(0), refs.pop(0)
    sel_ref = refs.pop(0) if with_router else None
    m = mod_ref[0]
    yr = yr_ref[0] + yr_ref[1]
    yn_r = _group_norm(yr, avg64_ref[...], RWKV_GN_EPS) * lnw_ref[...] + lnb_ref[...]
    g = _dot(jax.nn.sigmoid(gl_ref[...]), rg2_ref[...])
    out_r = (yn_r + bv_ref[0] + bv_ref[1]) * g
    hn = _group_norm(hm_ref[0] + hm_ref[1], avg128_ref[...], NORM_EPS) * mlw_ref[...]
    out_m = jax.nn.sigmoid(mo_ref[...]) * hn
    merged = (jax.nn.sigmoid(g0_ref[...]) * _dot(out_r, wor_ref[...])
              + jax.nn.sigmoid(g1_ref[...]) * _dot(out_m, wom_ref[...])
              + jax.nn.sigmoid(g2_ref[...]) * _dot(yn_ref[...], won_ref[...]))
    o = _dot(merged, wout_ref[...])
    x1 = x_ref[...] + m[2:3] * _rms(o, gpost_ref[...])
    x1_ref[...] = x1
    hin = _rms(x1, gpre_ref[...]) * (1.0 + m[4:5]) + m[3:4]
    if with_router:
        sel_ref[...] = _top2_select(_dot3(hin, wr_ref[...]) + br_ref[...])
        hin_ref[0] = hin[:, :D // 2]
        hin_ref[1] = hin[:, D // 2:]
    else:
        hin_ref[...] = hin.astype(BF16)


def _merge_call(x, mod3, yr, bv, hm, yn, z, W, n_ctx, lat_t, router, tm=256):
    ntok = x.shape[0]
    midx = _mod_index(tm, n_ctx, lat_t)
    with_router = router is not None
    row = lambda w: pl.BlockSpec((tm, w), lambda i: (i, 0))
    dirs = pl.BlockSpec((2, tm, HW), lambda i: (0, i, 0))
    zspec = lambda w, cb: pl.BlockSpec((tm, w), lambda i: (i, cb))
    full = lambda a: pl.BlockSpec(a.shape, lambda i: (0,) * a.ndim)
    consts = [W['avg64'], W['avg128'], W['rg2'], W['lnw'], W['lnb'], W['mlw'], W['wor'], W['wom'], W['won'],
              W['wout'], W['gpost'], W['gpre']]
    in_specs = [row(D), pl.BlockSpec((1, 8, D), lambda i: (midx(i), 0, 0)), dirs, dirs, dirs, row(HW),
                zspec(LANE, ZB_GL), zspec(HW, ZB_MO // 4),
                zspec(D, ZB_GATE // 8), zspec(D, ZB_GATE // 8 + 1), zspec(D, ZB_GATE // 8 + 2)]
    in_specs += [full(a) for a in consts]
    args = [x, mod3, yr, bv, hm, yn, z, z, z, z, z] + consts
    if with_router:
        out_shape = [jax.ShapeDtypeStruct((ntok, D), F32), jax.ShapeDtypeStruct((2, ntok, D // 2), F32)]
        out_specs = [row(D), pl.BlockSpec((2, tm, D // 2), lambda i: (0, i, 0))]
    else:
        out_shape = [jax.ShapeDtypeStruct((ntok, D), F32), jax.ShapeDtypeStruct((ntok, D), BF16)]
        out_specs = [row(D), row(D)]
    if with_router:
        in_specs += [full(router[0]), full(router[1])]
        args += list(router)
        out_shape.append(jax.ShapeDtypeStruct((ntok, LANE), F32))
        out_specs.append(row(LANE))
    return pl.pallas_call(
        functools.partial(_merge_body, with_router=with_router),
        out_shape=out_shape,
        grid=(ntok // tm,),
        in_specs=in_specs,
        out_specs=out_specs,
        compiler_params=_params(("parallel",)),
        name="branch_merge",
    )(*args)


def _top2_select(lg):
    lane = lax.broadcasted_iota(jnp.int32, lg.shape, 1)
    m1 = jnp.max(lg, axis=1, keepdims=True)
    i1 = jnp.min(jnp.where(lg == m1, lane, LANE), axis=1, keepdims=True)
    lg2 = jnp.where(lane == i1, -jnp.inf, lg)
    m2 = jnp.max(lg2, axis=1, keepdims=True)
    i2 = jnp.min(jnp.where(lg2 == m2, lane, LANE), axis=1, keepdims=True)
    e2 = jnp.exp(m2 - m1)
    den = 1.0 + e2
    return jnp.where(lane == 0, i1.astype(F32),
                     jnp.where(lane == 1, i2.astype(F32),
                               jnp.where(lane == 2, 1.0 / den, jnp.where(lane == 3, e2 / den, 0.0))))


def _ffn_body(h_ref, x1_ref, mod_ref, gpost_ref, wg_ref, wu_ref, wd_ref, o_ref, acc_s):
    j = pl.program_id(1)

    @pl.when(j == 0)
    def _():
        acc_s[...] = jnp.zeros_like(acc_s)

    hin = h_ref[...]
    hg = jnp.dot(hin, wg_ref[...], preferred_element_type=F32)
    hu = jnp.dot(hin, wu_ref[...], preferred_element_type=F32)
    acc_s[...] += _dot(hg * jax.nn.sigmoid(hg) * hu, wd_ref[...])

    @pl.when(j == pl.num_programs(1) - 1)
    def _():
        m = mod_ref[0]
        o_ref[...] = x1_ref[...] + m[5:6] * _rms(acc_s[...], gpost_ref[...])


def _ffn_call(hin, x1, mod3, gpost, wg, wu, wd, n_ctx, lat_t, tm=512, tf=1408):
    ntok = hin.shape[0]
    dff = wg.shape[1]
    midx = _mod_index(tm, n_ctx, lat_t)
    return pl.pallas_call(
        _ffn_body,
        out_shape=jax.ShapeDtypeStruct((ntok, D), F32),
        grid=(ntok // tm, dff // tf),
        in_specs=[pl.BlockSpec((tm, D), lambda i, j: (i, 0)),
                  pl.BlockSpec((tm, D), lambda i, j: (i, 0)),
                  pl.BlockSpec((1, 8, D), lambda i, j: (midx(i), 0, 0)),
                  pl.BlockSpec((1, D), lambda i, j: (0, 0)),
                  pl.BlockSpec((D, tf), lambda i, j: (0, j)),
                  pl.BlockSpec((D, tf), lambda i, j: (0, j)),
                  pl.BlockSpec((tf, D), lambda i, j: (j, 0))],
        out_specs=pl.BlockSpec((tm, D), lambda i, j: (i, 0)),
        scratch_shapes=[pltpu.VMEM((tm, D), F32)],
        compiler_params=_params(("parallel", "arbitrary")),
        name="dense_swiglu",
    )(hin, x1, mod3, gpost, wg, wu, wd)


MOE_TILE = 512
MOE_TF = 896
CMB_TILE = 128
SLAB_ALIGN = 8
SLAB = CMB_TILE + SLAB_ALIGN


def _moe_plan(sel, ntok):
    i32 = jnp.int32
    e1, e2 = sel[:, 0].astype(i32), sel[:, 1].astype(i32)
    ar = jnp.arange(ntok, dtype=i32)
    ex = jnp.arange(N_EXPERTS, dtype=i32)
    oh = ((e1[:, None] == ex) | (e2[:, None] == ex)).astype(i32)
    csum = jnp.cumsum(oh, axis=0)
    total = csum[-1]
    gsz = (total + MOE_TILE - 1) // MOE_TILE * MOE_TILE
    gend = jnp.cumsum(gsz)
    gstart = gend - gsz
    dstart = jnp.cumsum(total) - total
    key, wts = lax.sort((jnp.concatenate([e1 * ntok + ar, e2 * ntok + ar]),
                         jnp.concatenate([sel[:, 2], sel[:, 3]])), num_keys=1)
    tok_d = key % ntok
    n_pad = 2 * ntok + (N_EXPERTS + 1) * MOE_TILE
    p = jnp.arange(n_pad, dtype=i32)
    ge = jnp.minimum(jnp.sum((p[:, None] >= gend[None, :]).astype(i32), axis=1), N_EXPERTS - 1)
    pick = lambda tab: jnp.sum(jnp.where(ge[:, None] == ex, tab[None, :], 0), axis=1)
    q = p - pick(gstart)
    live = (q < pick(total)) & (p < gend[-1])
    src = jnp.clip(pick(dstart) + q, 0, 2 * ntok - 1)
    tok_p = jnp.where(live, tok_d[src], 0)
    w_p = jnp.where(live, wts[src], 0.0)
    tstart = jnp.arange(n_pad // MOE_TILE, dtype=i32) * MOE_TILE
    texp = jnp.minimum(jnp.sum((tstart[:, None] >= gend[None, :]).astype(i32), axis=1), N_EXPERTS - 1)
    tval = (tstart < gend[-1]).astype(i32)
    r0 = (csum - oh)[::CMB_TILE]
    r1 = jnp.concatenate([r0[1:], total[None]], axis=0)
    alo = (gstart[None, :] + r0).reshape(-1)
    ahi = (gstart[None, :] + r1).reshape(-1)
    return dict(tok=tok_p, w=w_p.reshape(n_pad, 1), texp=texp, tval=tval,
                a8=alo // SLAB_ALIGN, alo=alo, ahi=ahi, n_pad=n_pad)


def _moe_gather_body(tok_ref, src_ref, o_ref, buf_s):
    base = pl.program_id(0) * MOE_TILE

    def gather(r, carry):
        buf_s[pl.ds(r, 1), :] = src_ref[pl.ds(tok_ref[base + r], 1), :]
        return carry

    lax.fori_loop(0, MOE_TILE, gather, 0, unroll=8)
    o_ref[...] = buf_s[...].astype(BF16)


def _moe_gather_call(src, plan):
    n_pad = plan['n_pad']
    width = src.shape[1]
    grid_spec = pltpu.PrefetchScalarGridSpec(
        num_scalar_prefetch=1,
        grid=(n_pad // MOE_TILE,),
        in_specs=[pl.BlockSpec(memory_space=pltpu.VMEM)],
        out_specs=pl.BlockSpec((MOE_TILE, width), lambda i, tk: (i, 0)),
        scratch_shapes=[pltpu.VMEM((MOE_TILE, width), F32)])
    return pl.pallas_call(
        _moe_gather_body,
        out_shape=jax.ShapeDtypeStruct((n_pad, width), BF16),
        grid_spec=grid_spec,
        compiler_params=_params(("arbitrary",)),
        name="moe_row_gather",
    )(plan['tok'], src)


def _moe_group_body(texp_ref, tval_ref, xa_ref, xb_ref, ws_ref, wg_ref, wu_ref, wd_ref, ys_ref, acc_s):
    del texp_ref
    i = pl.program_id(0)
    j = pl.program_id(1)
    half = D // 2

    @pl.when(j == 0)
    def _():
        acc_s[...] = jnp.zeros_like(acc_s)

    @pl.when(tval_ref[i] > 0)
    def _():
        xa, xb = xa_ref[...], xb_ref[...]
        mm = lambda u, ref, rows: jnp.dot(u, ref[0, rows, :], preferred_element_type=F32)
        hg = mm(xa, wg_ref, slice(0, half)) + mm(xb, wg_ref, slice(half, D))
        hu = mm(xa, wu_ref, slice(0, half)) + mm(xb, wu_ref, slice(half, D))
        acc_s[...] += _dot(hg * jax.nn.sigmoid(hg) * hu, wd_ref[0])

    @pl.when(j == pl.num_programs(1) - 1)
    def _():
        ys_ref[...] = acc_s[...] * ws_ref[...]


def _moe_group_call(xa, xb, plan, wg, wu, wd):
    n_pad = plan['n_pad']
    dff = wg.shape[2]
    xspec = pl.BlockSpec((MOE_TILE, D // 2), lambda i, j, te, tv: (i, 0))
    grid_spec = pltpu.PrefetchScalarGridSpec(
        num_scalar_prefetch=2,
        grid=(n_pad // MOE_TILE, dff // MOE_TF),
        in_specs=[xspec, xspec,
                  pl.BlockSpec((MOE_TILE, 1), lambda i, j, te, tv: (i, 0)),
                  pl.BlockSpec((1, D, MOE_TF), lambda i, j, te, tv: (te[i], 0, j)),
                  pl.BlockSpec((1, D, MOE_TF), lambda i, j, te, tv: (te[i], 0, j)),
                  pl.BlockSpec((1, MOE_TF, D), lambda i, j, te, tv: (te[i], j, 0))],
        out_specs=pl.BlockSpec((MOE_TILE, D), lambda i, j, te, tv: (i, 0)),
        scratch_shapes=[pltpu.VMEM((MOE_TILE, D), F32)])
    return pl.pallas_call(
        _moe_group_body,
        out_shape=jax.ShapeDtypeStruct((n_pad, D), F32),
        grid_spec=grid_spec,
        compiler_params=_params(("parallel", "arbitrary")),
        name="moe_expert_swiglu",
    )(plan['texp'], plan['tval'], xa, xb, plan['w'], wg, wu, wd)


def _moe_combine_body(a8_ref, alo_ref, ahi_ref, *refs, tile0):
    ys_refs, tk_refs = refs[:N_EXPERTS], refs[N_EXPERTS:2 * N_EXPERTS]
    x1_ref, mod_ref, gpost_ref, o_ref = refs[2 * N_EXPERTS:]
    t = pl.program_id(0) + tile0
    tloc = t * CMB_TILE + lax.broadcasted_iota(jnp.int32, (SLAB, CMB_TILE), 1)
    srow = lax.broadcasted_iota(jnp.int32, (SLAB, 1), 0)
    f = jnp.zeros((CMB_TILE, D), F32)
    for e in range(N_EXPERTS):
        k = t * N_EXPERTS + e
        row = a8_ref[k] * SLAB_ALIGN + srow
        tok = jnp.where((row >= alo_ref[k]) & (row < ahi_ref[k]), tk_refs[e][...], -1)
        onehot = jnp.where(tok == tloc, 1.0, 0.0).astype(BF16)
        f = f + _dot_rhs2(onehot, ys_refs[e][...], TN)
    m = mod_ref[0]
    o_ref[...] = x1_ref[...] + m[5:6] * _rms(f, gpost_ref[...])


def _moe_combine_call(ys, plan, x1, mod3, gpost, n_ctx, lat_t, row0, nrows):
    midx = _mod_index(CMB_TILE, n_ctx, lat_t)
    tile0 = row0 // CMB_TILE

    def slab(w, e):
        return pl.BlockSpec((pl.Element(SLAB), pl.Element(w)),
                            lambda t, a8, lo, hi: (a8[(t + tile0) * N_EXPERTS + e] * SLAB_ALIGN, 0))

    experts = range(N_EXPERTS)
    grid_spec = pltpu.PrefetchScalarGridSpec(
        num_scalar_prefetch=3,
        grid=(nrows // CMB_TILE,),
        in_specs=[slab(D, e) for e in experts] + [slab(1, e) for e in experts] + [
            pl.BlockSpec((CMB_TILE, D), lambda t, a8, lo, hi: (t + tile0, 0)),
            pl.BlockSpec((1, 8, D), lambda t, a8, lo, hi: (midx(t + tile0), 0, 0)),
            pl.BlockSpec((1, D), lambda t, a8, lo, hi: (0, 0))],
        out_specs=pl.BlockSpec((CMB_TILE, D), lambda t, a8, lo, hi: (t, 0)))
    tok2d = plan['tok'].reshape(-1, 1)
    return pl.pallas_call(
        functools.partial(_moe_combine_body, tile0=tile0),
        out_shape=jax.ShapeDtypeStruct((nrows, D), F32),
        grid_spec=grid_spec,
        compiler_params=_params(("parallel",)),
        name="moe_combine",
    )(plan['a8'], plan['alo'], plan['ahi'], *([ys] * N_EXPERTS), *([tok2d] * N_EXPERTS), x1, mod3, gpost)


def _rope_tables(t):
    i = jnp.arange(M_HD)
    half, pair, f = i // 64, (i % 64) // 32, i % 32
    tt = jnp.arange(t)
    pos = jnp.stack([tt // GRID_W, tt % GRID_W], axis=-1).astype(F32)
    inv = ROPE_BASE ** (-f.astype(F32) / 32)
    ang = pos[:, half] * inv[None, :]
    sin = jnp.sin(ang)
    return (jnp.cos(ang), jnp.where(pair == 0, -sin, 0.0)[...], jnp.where(pair == 1, sin, 0.0))


def _pad_in_proj(w):
    off = {}
    o = 0
    for name, width in (('r', 512), ('k', 512), ('v', 512), ('wl', 128), ('al', 128), ('gl', 128),
                        ('mq', 512), ('mk', 512), ('mv', 512), ('mo', 512), ('mif', 16),
                        ('nq', 512), ('nk', 512), ('nv', 512), ('gate', 3072)):
        off[name] = (o, o + width)
        o += width
    col = lambda n: w[:, off[n][0]:off[n][1]].astype(BF16)
    parts = [col(n) for n in ('r', 'k', 'v', 'mq', 'mk', 'mv', 'mo', 'nq', 'nk', 'nv', 'gate', 'wl', 'al', 'gl', 'mif')]
    parts.append(jnp.zeros((w.shape[0], LANE - 16), BF16))
    return jnp.concatenate(parts, axis=1)


def _block_avg(width, group):
    i = jnp.arange(width) // group
    return jnp.where(i[:, None] == i[None, :], 1.0 / group, 0.0).astype(BF16)


def _zero_pad_rows(w2):
    z = jnp.zeros_like(w2[0])
    return jnp.stack([jnp.concatenate([w2[0], z], axis=0), jnp.concatenate([z, w2[1]], axis=0)])


def kernel(x_prompt, x_sample, state_rwkv, state_mlstm_C, state_mlstm_n, state_mlstm_m, cache_nat_k, cache_nat_v,
           c, c_ctx, w_mod, b_mod, g_pre_mix, g_post_mix, g_pre_ffn, g_post_ffn, w_in,
           rw_w0, rw_w2, rw_a0, rw_a2, rw_g2, rw_k_k, rw_k_a, rw_r_k, rw_ln_w, rw_ln_b,
           ml_b_i, ml_b_f, ml_norm_w, nat_rpb, w_o_rwkv, w_o_mlstm, w_o_nat, w_out,
           ff_w_gate, ff_w_up, ff_w_down, moe_w_router, moe_b_router, moe_w_gate, moe_w_up, moe_w_down):
    cb, ct = x_prompt.shape[:2]
    lb, lt = x_sample.shape[:2]
    n_ctx, n_lat = cb * ct, lb * lt
    x = jnp.concatenate([x_prompt.reshape(n_ctx, D), x_sample.reshape(n_lat, D)], axis=0)
    cvec = jnp.zeros((8, D), F32).at[0].set(c_ctx).at[1:1 + lb].set(c)
    rope = _rope_tables(lt)
    avg64, avg128 = _block_avg(HW, R_HD), _block_avg(HW, M_HD)
    s_rw = ml_st = kv_c = None
    for l in range(DEPTH):
        mod = _mod_call(cvec, w_mod[l], b_mod[l])
        mod3 = jnp.pad(mod[:1 + lb].reshape(1 + lb, 6, D), ((0, 0), (0, 2), (0, 0)))
        z = _proj_call(x, mod3, g_pre_mix[l], _pad_in_proj(w_in[l]), n_ctx, lt)

        rp = dict(w0=rw_w0[l].reshape(2, 1, HW), w2=_zero_pad_rows(rw_w2[l]),
                  a0=rw_a0[l].reshape(2, 1, HW), a2=_zero_pad_rows(rw_a2[l]),
                  kk=rw_k_k[l].reshape(1, HW), ka=rw_k_a[l].reshape(1, HW), rk=rw_r_k[l].reshape(1, HW))
        yr, bv, s_rw = _rwkv_call(z, rp, None, None, row0=0, nb=cb, t=ct, tb=RW_BLOCK, layer=l,
                                  prev_state=None if s_rw is None else (s_rw,))
        yr, bv = _rwkv_call(z, rp, state_rwkv[:, l], (yr, bv), row0=n_ctx, nb=lb, t=lt, tb=RW_BLOCK)

        gt = z[:, ZB_MIF * LANE:ZB_MIF * LANE + 16].reshape(-1, CHUNK, 16).transpose(0, 2, 1)
        hm, *ml_st = _mlstm_call(z, gt, None, ml_b_i[l], ml_b_f[l], None, None,
                                 row0=0, nb=cb, t=ct, tb=ML_BLOCK, layer=l, prev_state=ml_st)
        init = (state_mlstm_C[:, l], state_mlstm_n[:, l][:, :, :, None, :], state_mlstm_m[:, l])
        hm, = _mlstm_call(z, gt, rope, ml_b_i[l], ml_b_f[l], init, (hm,),
                          row0=n_ctx, nb=lb, t=lt, tb=ML_BLOCK)

        yn, *kv_c = _ctx_attn_call(z, kv_c, nb=cb, t=ct, layer=l)
        yn = _natten_call(z, cache_nat_k[:, l], cache_nat_v[:, l], _natten_table(nat_rpb[l]), yn,
                          row0=n_ctx, nb=lb, t=lt)

        mw = dict(avg64=avg64, avg128=avg128, rg2=rw_g2[l].astype(BF16), lnw=rw_ln_w[l].reshape(1, HW),
                  lnb=rw_ln_b[l].reshape(1, HW), mlw=ml_norm_w[l].reshape(1, HW),
                  wor=w_o_rwkv[l].astype(BF16), wom=w_o_mlstm[l].astype(BF16), won=w_o_nat[l].astype(BF16),
                  wout=w_out[l].astype(BF16), gpost=g_post_mix[l].reshape(1, D), gpre=g_pre_ffn[l].reshape(1, D))
        j = l // 2
        if l % 2 == 0:
            router = None
        else:
            wr = jnp.pad(moe_w_router[j], ((0, 0), (0, LANE - N_EXPERTS)))
            br = jnp.pad(moe_b_router[j], (0, LANE - N_EXPERTS), constant_values=NEG).reshape(1, LANE)
            router = (wr, br)
        merged = _merge_call(x, mod3, yr, bv, hm, yn, z, mw, n_ctx, lt, router)
        gpost = g_post_ffn[l].reshape(1, D)
        if l % 2 == 0:
            x = _ffn_call(merged[1], merged[0], mod3, gpost, ff_w_gate[j].astype(BF16),
                          ff_w_up[j].astype(BF16), ff_w_down[j].astype(BF16), n_ctx, lt)
        else:
            plan = _moe_plan(merged[2], n_ctx + n_lat)
            xa, xb = _moe_gather_call(merged[1][0], plan), _moe_gather_call(merged[1][1], plan)
            ys = _moe_group_call(xa, xb, plan, moe_w_gate[j].astype(BF16), moe_w_up[j].astype(BF16),
                                 moe_w_down[j].astype(BF16))
            groups = [_moe_combine_call(ys, plan, merged[0], mod3, gpost, n_ctx, lt, r0, nr)
                      for r0, nr in ((0, n_ctx), (n_ctx, n_lat))]
            x = jnp.concatenate(groups, axis=0) if l + 1 < DEPTH else None
        if x is not None:
            groups = [x[:n_ctx], x[n_ctx:]]

    c_m, n_m, m_m = ml_st
    return (groups[0].reshape(cb, ct, D), groups[1].reshape(lb, lt, D),
            s_rw, c_m, n_m[:, :, :, :, 0, :], m_m[:, :, :, :, 0, 0], kv_c[0], kv_c[1])
```

```python
import functools
import math

import jax
import jax.numpy as jnp
from jax import lax
from jax.experimental import pallas as pl
from jax.experimental.pallas import tpu as pltpu

F32 = jnp.float32
BF16 = jnp.bfloat16

D = 1024
DEPTH = 2
GRID_W = 64
R_HEADS, R_HD = 8, 64
M_HEADS, M_HD = 4, 128
N_HEADS, N_HD = 8, 64
HW = 512
WIN_ROWS, WIN_COLS = 8, 16
N_EXPERTS = 8
RWKV_GN_EPS = 64e-5
NORM_EPS = 1e-6
ROPE_BASE = 10000.0
CHUNK = 64
NEG = -1e30
LANE = 128
VMEM_LIMIT = 56 * 1024 * 1024

ZB_R, ZB_K, ZB_V = 0, 4, 8
ZB_MQ, ZB_MK, ZB_MV, ZB_MO = 12, 16, 20, 24
ZB_NQ, ZB_NK, ZB_NV = 28, 32, 36
ZB_GATE = 40
ZB_WL, ZB_AL, ZB_GL, ZB_MIF = 64, 65, 66, 67
P_PAD = 68 * LANE

NN = (((1,), (0,)), ((), ()))
NT = (((1,), (1,)), ((), ()))
TN = (((0,), (0,)), ((), ()))


def _dot(a, b, dims=NN):
    return lax.dot_general(a.astype(BF16), b.astype(BF16), dims, preferred_element_type=F32)


def _split(x):
    hi = x.astype(BF16)
    return hi, (x - hi.astype(F32)).astype(BF16)


def _dot3(a, b, dims=NN):
    ah, al = _split(a)
    bh, bl = _split(b)
    d = lambda u, w: lax.dot_general(u, w, dims, preferred_element_type=F32)
    return d(ah, bh) + (d(ah, bl) + d(al, bh))


def _dot_lhs2(a, b_exact, dims=NN):
    ah, al = _split(a)
    d = lambda u: lax.dot_general(u, b_exact, dims, preferred_element_type=F32)
    return d(ah) + d(al)


def _dot_rhs2(a_exact, b, dims=NN):
    bh, bl = _split(b)
    d = lambda w: lax.dot_general(a_exact, w, dims, preferred_element_type=F32)
    return d(bh) + d(bl)


def _log_sigmoid(x):
    return jnp.minimum(x, 0.0) - jnp.log1p(jnp.exp(-jnp.abs(x)))


def _rms(x, g):
    return x * lax.rsqrt(jnp.mean(x * x, axis=-1, keepdims=True) + NORM_EPS) * g


def _params(sem):
    return pltpu.CompilerParams(dimension_semantics=sem, vmem_limit_bytes=VMEM_LIMIT)


def _alias_prev(in_specs, args, prev, first_out=0):
    aliases = {}
    for k, a in enumerate(prev or ()):
        aliases[len(args)] = first_out + k
        in_specs.append(pl.BlockSpec(memory_space=pl.ANY))
        args.append(a)
    return aliases


def _mod_body(c_ref, w_ref, b_ref, o_ref):
    c = c_ref[...]
    o_ref[...] = _dot(c * jax.nn.sigmoid(c), w_ref[...]) + b_ref[...]


def _mod_call(cvec, w_mod, b_mod):
    tn = 1536
    return pl.pallas_call(
        _mod_body,
        out_shape=jax.ShapeDtypeStruct((8, 6 * D), F32),
        grid=(6 * D // tn,),
        in_specs=[pl.BlockSpec((8, D), lambda j: (0, 0)),
                  pl.BlockSpec((D, tn), lambda j: (0, j)),
                  pl.BlockSpec((1, tn), lambda j: (0, j))],
        out_specs=pl.BlockSpec((8, tn), lambda j: (0, j)),
        compiler_params=_params(("arbitrary",)),
        name="adaln_mod",
    )(cvec, w_mod, b_mod.reshape(1, 6 * D))


def _mod_index(tile_rows, n_ctx, lat_t):
    def f(i):
        start = i * tile_rows
        return jnp.where(start < n_ctx, 0, 1 + (start - n_ctx) // lat_t)
    return f


def _proj_body(x_ref, mod_ref, g_ref, w_ref, z_ref, hin_s):
    @pl.when(pl.program_id(1) == 0)
    def _():
        m = mod_ref[0]
        hin_s[...] = (_rms(x_ref[...], g_ref[...]) * (1.0 + m[1:2]) + m[0:1]).astype(BF16)
    z_ref[...] = jnp.dot(hin_s[...], w_ref[...], preferred_element_type=F32)


def _proj_call(x, mod3, g, w_bf, n_ctx, lat_t, tm=1024, tn=2176):
    ntok = x.shape[0]
    midx = _mod_index(tm, n_ctx, lat_t)
    return pl.pallas_call(
        _proj_body,
        out_shape=jax.ShapeDtypeStruct((ntok, P_PAD), F32),
        grid=(ntok // tm, P_PAD // tn),
        in_specs=[pl.BlockSpec((tm, D), lambda i, j: (i, 0)),
                  pl.BlockSpec((1, 8, D), lambda i, j: (midx(i), 0, 0)),
                  pl.BlockSpec((1, D), lambda i, j: (0, 0)),
                  pl.BlockSpec((D, tn), lambda i, j: (0, j))],
        out_specs=pl.BlockSpec((tm, tn), lambda i, j: (i, j)),
        scratch_shapes=[pltpu.VMEM((tm, D), BF16)],
        compiler_params=_params(("parallel", "arbitrary")),
        name="in_proj",
    )(x, mod3, g.reshape(1, D), w_bf)


RW_TILE = 2 * CHUNK
RW_BLOCK = 1024
ML_BLOCK = 512


def _rwkv_body(*refs, tb_rows, nseq, has_init, has_out, n_alias):
    (r_ref, k_ref, v_ref, wl_ref, al_ref, w0_ref, w2_ref, a0_ref, a2_ref,
     kkw_ref, kaw_ref, rkw_ref, hsum_ref) = refs[:13]
    rest = list(refs[13:])
    s0_ref = rest.pop(0) if has_init else None
    del rest[:n_alias]
    y_ref, bv_ref = rest.pop(0), rest.pop(0)
    sout_ref = rest.pop(0) if has_out else None
    r2_s, m_s, g_s, st_s = rest
    C = CHUNK
    TL = RW_TILE
    nchunk = tb_rows // nseq // C
    d = pl.program_id(2)
    tb = pl.program_id(3)
    n_tb = pl.num_programs(3)

    ti = lax.broadcasted_iota(jnp.int32, (TL, TL), 0)
    si = lax.broadcasted_iota(jnp.int32, (TL, TL), 1)
    head_blk = (ti >> 6) == (si >> 6)

    @pl.when(tb == 0)
    def _():
        for q in range(nseq):
            if has_init:
                s0 = s0_ref[q, 0]
                two = jnp.concatenate([jnp.concatenate([s0[0], s0[0]], axis=1),
                                       jnp.concatenate([s0[1], s0[1]], axis=1)], axis=0)
                st_s[q] = jnp.where(head_blk, two, 0.0)
            else:
                st_s[q] = jnp.zeros((LANE, LANE), F32)

    hsum = hsum_ref[...]
    r = r_ref[...]
    kf = k_ref[...]
    v = v_ref[...]
    w_pre = w0_ref[0] + _dot(jnp.tanh(wl_ref[...]), w2_ref[0])
    lw = -math.exp(-0.5) * jax.nn.sigmoid(w_pre)
    a = jax.nn.sigmoid(a0_ref[0] + _dot(al_ref[...], a2_ref[0]))
    kd = kf * (1.0 + (a - 1.0) * kaw_ref[...])
    kkf = kf * kkw_ref[...]
    kn = kkf * lax.rsqrt(_dot_lhs2(kkf * kkf, hsum) + 1e-12)
    bd = kn * a
    bv_ref[0] = _dot_lhs2(r * kd * rkw_ref[...], hsum) * v

    dlt = (ti - si) * (1 - 2 * d)
    strict = head_blk & (dlt > 0)
    incl = head_blk & (dlt >= 0)
    cum_ones = jnp.concatenate([jnp.where(incl, 1.0, 0.0), jnp.where(head_blk, 1.0, 0.0)], axis=0).astype(BF16)
    is_diag = ti == si
    eye = jnp.where(is_diag, 1.0, 0.0)
    pair = (ti >> 1) == (si >> 1)
    sibling = [((ti >> lg) ^ (si >> lg)) == 1 for lg in range(1, 6)]
    head0 = si < R_HD

    n_tiles = tb_rows // TL
    tiles = []
    for tile in range(n_tiles):
        rows = slice(TL * tile, TL * (tile + 1))
        cs = _dot_rhs2(cum_ones, lw[rows])
        cum, tot = cs[:TL], cs[TL:]
        e_neg = jnp.exp(-cum)
        e_hat = jnp.exp(tot - cum)
        rt = r[rows] * jnp.exp(cum)
        at = -kn[rows] * jnp.exp(cum - lw[rows])
        tiles.append(dict(rows=rows, tot=tot, rt=rt, at=at, bh=bd[rows] * e_hat, kh=kd[rows] * e_hat,
                          rhs=jnp.concatenate([bd[rows] * e_neg, kd[rows] * e_neg], axis=0).astype(BF16),
                          at_bf=at.astype(BF16), v_bf=v[rows].astype(BF16)))
    chains = [(tl, h) for tl in tiles for h in range(2)]
    a_all = []
    for tl, h in chains:
        hm = head0 if h == 0 else ~head0
        lhs = jnp.concatenate([jnp.where(hm, tl['at'], 0.0), jnp.where(hm, tl['rt'], 0.0)], axis=0)
        a_all.append(_dot(lhs, tl['rhs'], NT))
    n_mat = [jnp.where(strict, a[:TL, :TL], 0.0) for a in a_all]
    a_ak = [jnp.where(strict, a[:TL, TL:], 0.0).astype(BF16) for a in a_all]
    a_rb = [jnp.where(incl, a[TL:, :TL], 0.0).astype(BF16) for a in a_all]
    a_rk = [jnp.where(incl, a[TL:, TL:], 0.0).astype(BF16) for a in a_all]
    t_inv = [eye + jnp.where(pair, n, 0.0) for n in n_mat]
    for sib in sibling:
        x_mat = [_dot(jnp.where(sib, n, 0.0), t) for n, t in zip(n_mat, t_inv)]
        t_inv = [t + _dot(t, x) for t, x in zip(t_inv, x_mat)]
    p_h = [_dot(t, tl['at_bf']) for t, (tl, _) in zip(t_inv, chains)]
    av = [_dot(a, tl['v_bf']) for a, (tl, _) in zip(a_ak, chains)]
    q_h = [_dot(t, x) for t, x in zip(t_inv, av)]
    r2_h = [_dot(a, p) for a, p in zip(a_rb, p_h)]
    y0_h = [_dot(a, q) + _dot(k_, tl['v_bf']) for a, q, k_, (tl, _) in zip(a_rb, q_h, a_rk, chains)]
    p_m, q_m = [], []
    for i, tl in enumerate(tiles):
        rows = tl['rows']
        pick = lambda u: jnp.where(head0, u[2 * i], u[2 * i + 1])
        p_m.append(pick(p_h))
        q_m.append(pick(q_h))
        r2_s[rows, :] = tl['rt'] + pick(r2_h)
        y_ref[0, rows, :] = pick(y0_h)
        for c in range(TL // C):
            cr = slice(C * c, C * (c + 1))
            decay = jnp.where(is_diag, jnp.exp(tl['tot'][C * c:C * c + 1]), 0.0)
            m_s[i * (TL // C) + c] = decay + jnp.where(head_blk, _dot(p_m[i][cr], tl['bh'][cr], TN), 0.0)
            g_s[i * (TL // C) + c] = jnp.where(
                head_blk, _dot(q_m[i][cr], tl['bh'][cr], TN) + _dot(v[rows][cr], tl['kh'][cr], TN), 0.0)

    def step(i, states):
        ci = jnp.where(d == 0, i, nchunk - 1 - i)
        new = []
        for q, s in enumerate(states):
            cq = q * nchunk + ci
            rows = pl.ds(pl.multiple_of(cq * C, C), C)
            y_ref[0, rows, :] = y_ref[0, rows, :] + _dot(r2_s[rows, :], s, NT)
            new.append(_dot3(s, m_s[cq]) + g_s[cq])
        return tuple(new)

    s_fin = lax.fori_loop(0, nchunk, step, tuple(st_s[q] for q in range(nseq)))
    for q in range(nseq):
        st_s[q] = s_fin[q]

    if has_out:
        @pl.when(tb == n_tb - 1)
        def _():
            for q in range(nseq):
                sout_ref[q, 0, 0, 0] = s_fin[q][:R_HD, :R_HD]
                sout_ref[q, 0, 0, 1] = s_fin[q][R_HD:, R_HD:]


def _rwkv_call(z, P, s0, prev, *, row0, nb, t, tb, layer=None, prev_state=None):
    nseq = max(tb // t, 1)
    n_tb = max(t // tb, 1)
    blk0 = row0 // tb
    has_init = s0 is not None

    def rowblk(b, d, i):
        return blk0 + b * n_tb + jnp.where(d == 0, i, n_tb - 1 - i)

    def zspec(cb):
        return pl.BlockSpec((tb, LANE), lambda b, hp, d, i: (rowblk(b, d, i), cb(hp)))

    vec = lambda: pl.BlockSpec((1, LANE), lambda b, hp, d, i: (0, hp))
    dvec = lambda: pl.BlockSpec((1, 1, LANE), lambda b, hp, d, i: (d, 0, hp))
    dmat = lambda: pl.BlockSpec((1, LANE, LANE), lambda b, hp, d, i: (d, 0, hp))
    st_spec = pl.BlockSpec((nseq, 1, 2, R_HD, R_HD), lambda b, hp, d, i: (b, d, hp, 0, 0))
    in_specs = [zspec(lambda hp: ZB_R + hp), zspec(lambda hp: ZB_K + hp), zspec(lambda hp: ZB_V + hp),
                zspec(lambda hp: ZB_WL), zspec(lambda hp: ZB_AL),
                dvec(), dmat(), dvec(), dmat(), vec(), vec(), vec(),
                pl.BlockSpec((LANE, LANE), lambda b, hp, d, i: (0, 0))]
    args = [z, z, z, z, z, P['w0'], P['w2'], P['a0'], P['a2'], P['kk'], P['ka'], P['rk'],
            _block_avg(LANE, R_HD) * R_HD]
    if has_init:
        in_specs.append(st_spec)
        args.append(s0)
    has_out = layer is not None
    aliases = _alias_prev(in_specs, args, prev)
    aliases.update(_alias_prev(in_specs, args, prev_state, first_out=2))
    yspec = pl.BlockSpec((1, tb, LANE), lambda b, hp, d, i: (d, rowblk(b, d, i), hp))
    out_shape = [jax.ShapeDtypeStruct((2, z.shape[0], HW), F32)] * 2
    out_specs = [yspec, yspec]
    if has_out:
        out_shape.append(jax.ShapeDtypeStruct((nb, DEPTH, 2, R_HEADS, R_HD, R_HD), F32))
        out_specs.append(pl.BlockSpec((nseq, 1, 1, 2, R_HD, R_HD), lambda b, hp, d, i: (b, layer, d, hp, 0, 0)))
    return pl.pallas_call(
        functools.partial(_rwkv_body, tb_rows=tb, nseq=nseq, has_init=has_init, has_out=has_out,
                          n_alias=len(aliases)),
        out_shape=out_shape,
        grid=(nb // nseq, R_HEADS // 2, 2, n_tb),
        in_specs=in_specs,
        out_specs=out_specs,
        input_output_aliases=aliases,
        scratch_shapes=[pltpu.VMEM((tb, LANE), F32), pltpu.VMEM((tb // CHUNK, LANE, LANE), F32),
                        pltpu.VMEM((tb // CHUNK, LANE, LANE), F32), pltpu.VMEM((nseq, LANE, LANE), F32)],
        compiler_params=_params(("parallel", "parallel", "arbitrary", "arbitrary")),
        name="rwkv7_chunked",
    )(*args)


def _mlstm_body(*refs, tb_rows, nseq, use_rope, has_init, has_out, n_alias):
    refs = list(refs)
    q_ref, k_ref, v_ref, g_ref, gt_ref = [refs.pop(0) for _ in range(5)]
    cos_ref, sin_ref, perm_ref = [refs.pop(0) for _ in range(3)] if use_rope else (None,) * 3
    bi_ref, bf_ref = refs.pop(0), refs.pop(0)
    c0_ref, n0_ref, m0_ref = [refs.pop(0) for _ in range(3)] if has_init else (None,) * 3
    del refs[:n_alias]
    h_ref = refs.pop(0)
    cout_ref, nout_ref, mout_ref = [refs.pop(0) for _ in range(3)] if has_out else (None,) * 3
    sc_s, kv_s, nk_s, cprev_s, nprev_s, c_s, n_s, m_s = refs
    L = CHUNK
    nck = tb_rows // L
    ncs = nck // nseq
    b = pl.program_id(0)
    h = pl.program_id(1)
    d = pl.program_id(2)
    tb = pl.program_id(3)
    n_tb = pl.num_programs(3)
    bi = bi_ref[d, h]
    bf = bf_ref[d, h]
    ji = d * (2 * M_HEADS) + h
    jf = ji + M_HEADS

    @pl.when(tb == 0)
    def _():
        for q in range(nseq):
            if has_init:
                c_s[q] = c0_ref[q, 0, 0]
                n_s[q] = n0_ref[q, 0, 0]
                m_s[q] = jnp.full((1, M_HD), m0_ref[b * nseq + q, d, h], F32)
            else:
                c_s[q] = jnp.zeros((M_HD, M_HD), F32)
                n_s[q] = jnp.zeros((1, M_HD), F32)
                m_s[q] = jnp.zeros((1, M_HD), F32)

    g = g_ref[...]
    lane = lax.broadcasted_iota(jnp.int32, g.shape, 1)
    icol_all = jnp.sum(jnp.where(lane == ji, g, 0.0), axis=1, keepdims=True) + bi
    fcol_all = _log_sigmoid(jnp.sum(jnp.where(lane == jf, g, 0.0), axis=1, keepdims=True) + bf)

    ti = lax.broadcasted_iota(jnp.int32, (L, L), 0)
    si = lax.broadcasted_iota(jnp.int32, (L, L), 1)
    sgn = 1 - 2 * d
    incl = (ti - si) * sgn >= 0
    incl_t = (si - ti) * sgn >= 0
    row128 = lambda x: jnp.broadcast_to(x, (1, M_HD))

    ch = []
    for c in range(nck):
        rows = slice(L * c, L * (c + 1))
        irow = gt_ref[c, pl.ds(ji, 1), :] + bi
        frow = _log_sigmoid(gt_ref[c, pl.ds(jf, 1), :] + bf)
        bsum_col = jnp.sum(jnp.where(incl, frow, 0.0), axis=1, keepdims=True)
        bsum_row = jnp.sum(jnp.where(incl_t, fcol_all[rows], 0.0), axis=0, keepdims=True)
        blast = jnp.sum(frow, axis=1, keepdims=True)
        wlog = blast - bsum_col + icol_all[rows]
        dmat = jnp.where(incl, bsum_col - bsum_row + irow, NEG)
        sc_s[c, 0:1, :] = row128(blast)
        sc_s[c, 1:2, :] = row128(jnp.max(wlog, axis=0, keepdims=True))
        ch.append(dict(rows=rows, bsum_col=bsum_col, blast=blast, wlog=wlog, dmat=dmat,
                       dmax=jnp.max(dmat, axis=1, keepdims=True)))

    def m_step(i, ms):
        ci = jnp.where(d == 0, i, ncs - 1 - i)
        new = []
        for q, m in enumerate(ms):
            cq = q * ncs + ci
            m_new = jnp.maximum(sc_s[cq, 0:1, :] + m, sc_s[cq, 1:2, :])
            sc_s[cq, 2:3, :] = m
            sc_s[cq, 3:4, :] = m_new
            new.append(m_new)
        return tuple(new)

    m_fin = lax.fori_loop(0, ncs, m_step, tuple(m_s[q] for q in range(nseq)))

    for c, cd in enumerate(ch):
        rows = cd['rows']
        cd['q'], cd['k'] = q_ref[rows, :], k_ref[rows, :]
        cd['v'] = v_ref[rows, :]
        cd['m_prev'] = sc_s[c, 2:3, 0:1]
        cd['m_new'] = sc_s[c, 3:4, 0:1]
    if use_rope:
        perm = perm_ref[...]
        rot = [(_dot_lhs2(cd['q'], perm), _dot_lhs2(cd['k'], perm)) for cd in ch]
        for cd, (rq, rk) in zip(ch, rot):
            cos, sin = cos_ref[cd['rows'], :], sin_ref[cd['rows'], :]
            cd['q'] = cd['q'] * cos + rq * sin
            cd['k'] = cd['k'] * cos + rk * sin
    for cd in ch:
        cd['k'] = cd['k'] * (M_HD ** -0.5)
    for cd in ch:
        cd['qk'] = _dot(cd['q'], cd['k'], NT)
    for c, cd in enumerate(ch):
        kw = cd['k'] * jnp.exp(cd['wlog'] - cd['m_new'])
        kv_s[c] = _dot3(kw, cd['v'], TN)
        nk_s[c] = jnp.sum(kw, axis=0, keepdims=True)
        sc_s[c, 4:5, :] = row128(jnp.exp(cd['blast'] + cd['m_prev'] - cd['m_new']))
    for cd in ch:
        inter = cd['bsum_col'] + cd['m_prev']
        cd['mt'] = jnp.maximum(inter, cd['dmax'])
        cd['iw'] = jnp.exp(inter - cd['mt'])
        sm = cd['qk'] * jnp.exp(cd['dmat'] - cd['mt'])
        cd['ssum'] = jnp.sum(sm, axis=1, keepdims=True)
        cd['sv'] = _dot(sm, cd['v'])

    def c_step(i, carry):
        ci = jnp.where(d == 0, i, ncs - 1 - i)
        new = []
        for q, (c_st, n_st) in enumerate(carry):
            cq = q * ncs + ci
            dec = sc_s[cq, 4:5, 0:1]
            cprev_s[cq] = c_st
            nprev_s[cq] = n_st
            new.append((dec * c_st + kv_s[cq], dec * n_st + nk_s[cq]))
        return tuple(new)

    st_fin = lax.fori_loop(0, ncs, c_step, tuple((c_s[q], n_s[q]) for q in range(nseq)))
    for q in range(nseq):
        c_s[q], n_s[q] = st_fin[q]
        m_s[q] = m_fin[q]

    qc_prev = [_dot(cd['q'], cprev_s[c]) for c, cd in enumerate(ch)]
    for c, cd in enumerate(ch):
        num = cd['iw'] * qc_prev[c] + cd['sv']
        den = cd['iw'] * jnp.sum(cd['q'] * nprev_s[c], axis=1, keepdims=True) + cd['ssum']
        h_ref[0, cd['rows'], :] = num / jnp.maximum(jnp.abs(den), jnp.exp(-cd['mt']))

    if has_out:
        @pl.when(tb == n_tb - 1)
        def _():
            for q in range(nseq):
                cout_ref[q, 0, 0, 0] = st_fin[q][0]
                nout_ref[q, 0, 0, 0] = st_fin[q][1]
                mout_ref[q, 0, 0, 0] = m_fin[q]


def _mlstm_call(z, gt, rope, b_i, b_f, init, prev, *, row0, nb, t, tb, layer=None, prev_state=None):
    nseq = max(tb // t, 1)
    n_tb = max(t // tb, 1)
    blk0 = row0 // tb
    cpb = tb // CHUNK
    use_rope = rope is not None
    has_init = init is not None

    def rowblk(b, d, i):
        return blk0 + b * n_tb + jnp.where(d == 0, i, n_tb - 1 - i)

    def zspec(cb):
        return pl.BlockSpec((tb, LANE), lambda b, h, d, i: (rowblk(b, d, i), cb(h)))

    smem = pl.BlockSpec(memory_space=pltpu.SMEM)
    in_specs = [zspec(lambda h: ZB_MQ + h), zspec(lambda h: ZB_MK + h), zspec(lambda h: ZB_MV + h),
                zspec(lambda h: ZB_MIF),
                pl.BlockSpec((cpb, 16, CHUNK), lambda b, h, d, i: (rowblk(b, d, i), 0, 0))]
    args = [z, z, z, z, gt]
    if use_rope:
        tspec = pl.BlockSpec((tb, LANE), lambda b, h, d, i: (jnp.where(d == 0, i, n_tb - 1 - i), 0))
        in_specs += [tspec, tspec, pl.BlockSpec((M_HD, M_HD), lambda b, h, d, i: (0, 0))]
        args += list(rope)
    in_specs += [smem, smem]
    args += [b_i, b_f]
    cspec = pl.BlockSpec((nseq, 1, 1, M_HD, M_HD), lambda b, h, d, i: (b, d, h, 0, 0))
    nspec = pl.BlockSpec((nseq, 1, 1, 1, M_HD), lambda b, h, d, i: (b, d, h, 0, 0))
    if has_init:
        in_specs += [cspec, nspec, smem]
        args += list(init)
    has_out = layer is not None
    aliases = _alias_prev(in_specs, args, prev)
    aliases.update(_alias_prev(in_specs, args, prev_state, first_out=1))
    out_shape = [jax.ShapeDtypeStruct((2, z.shape[0], HW), F32)]
    out_specs = [pl.BlockSpec((1, tb, LANE), lambda b, h, d, i: (d, rowblk(b, d, i), h))]
    if has_out:
        vec_shape = jax.ShapeDtypeStruct((nb, DEPTH, 2, M_HEADS, 1, M_HD), F32)
        out_shape += [jax.ShapeDtypeStruct((nb, DEPTH, 2, M_HEADS, M_HD, M_HD), F32), vec_shape, vec_shape]
        vec_spec = pl.BlockSpec((nseq, 1, 1, 1, 1, M_HD), lambda b, h, d, i: (b, layer, d, h, 0, 0))
        out_specs += [pl.BlockSpec((nseq, 1, 1, 1, M_HD, M_HD), lambda b, h, d, i: (b, layer, d, h, 0, 0)),
                      vec_spec, vec_spec]
    return pl.pallas_call(
        functools.partial(_mlstm_body, tb_rows=tb, nseq=nseq, use_rope=use_rope, has_init=has_init,
                          has_out=has_out, n_alias=len(aliases)),
        out_shape=out_shape,
        grid=(nb // nseq, M_HEADS, 2, n_tb),
        in_specs=in_specs,
        out_specs=out_specs,
        input_output_aliases=aliases,
        scratch_shapes=[pltpu.VMEM((cpb, 8, M_HD), F32),
                        pltpu.VMEM((cpb, M_HD, M_HD), F32), pltpu.VMEM((cpb, 1, M_HD), F32),
                        pltpu.VMEM((cpb, M_HD, M_HD), F32), pltpu.VMEM((cpb, 1, M_HD), F32),
                        pltpu.VMEM((nseq, M_HD, M_HD), F32), pltpu.VMEM((nseq, 1, M_HD), F32),
                        pltpu.VMEM((nseq, 1, M_HD), F32)],
        compiler_params=_params(("parallel", "parallel", "arbitrary", "arbitrary")),
        name="mlstm_chunkwise",
    )(*args)


def _ctx_attn_body(q_ref, k_ref, v_ref, *refs):
    y_ref, ko_ref, vo_ref = refs[-3:]
    q = q_ref[...] * (N_HD ** -0.5)
    k = k_ref[...]
    v = v_ref[...]
    heads = [slice(N_HD * h, N_HD * (h + 1)) for h in range(2)]
    for h, sl in enumerate(heads):
        ko_ref[0, 0, h] = k[:, sl]
        vo_ref[0, 0, h] = v[:, sl]
    s = [_dot(q[:, sl], k[:, sl], NT) for sl in heads]
    p = [jnp.exp(s_ - jnp.max(s_, axis=-1, keepdims=True)) for s_ in s]
    o = [_dot(p_, v[:, sl]) for p_, sl in zip(p, heads)]
    for o_, p_, sl in zip(o, p, heads):
        y_ref[:, sl] = o_ / jnp.sum(p_, axis=-1, keepdims=True)


def _ctx_attn_call(z, prev_kv, *, nb, t, layer):
    zspec = lambda cb: pl.BlockSpec((t, LANE), lambda b, hp: (b, cb + hp))
    kv_spec = pl.BlockSpec((1, 1, 2, t, N_HD), lambda b, hp: (b, layer, hp, 0, 0))
    kv_shape = jax.ShapeDtypeStruct((nb, DEPTH, N_HEADS, t, N_HD), F32)
    in_specs = [zspec(ZB_NQ), zspec(ZB_NK), zspec(ZB_NV)]
    args = [z, z, z]
    aliases = _alias_prev(in_specs, args, prev_kv, first_out=1)
    return pl.pallas_call(
        _ctx_attn_body,
        out_shape=[jax.ShapeDtypeStruct((z.shape[0], HW), F32), kv_shape, kv_shape],
        grid=(nb, N_HEADS // 2),
        in_specs=in_specs,
        out_specs=[pl.BlockSpec((t, LANE), lambda b, hp: (b, hp)), kv_spec, kv_spec],
        input_output_aliases=aliases,
        compiler_params=_params(("parallel", "parallel")),
        name="context_attention",
    )(*args)


def _natten_body(q_ref, k_ref, v_ref, kc_ref, vc_ref, tab_ref, prev_ref, y_ref, *, rows_per_step, rows_n):
    del prev_ref
    rb = pl.program_id(2)
    n_loc = WIN_ROWS * GRID_W

    unroll = 4

    def rows(it, carry):
        work = []
        for u in range(unroll):
            i = it * unroll + u
            r = rb * rows_per_step + i
            rstart = jnp.clip(r - WIN_ROWS // 2, 0, rows_n - WIN_ROWS)
            qrows = pl.ds(pl.multiple_of(i * GRID_W, GRID_W), GRID_W)
            krows = pl.ds(pl.multiple_of(rstart * GRID_W, GRID_W), n_loc)
            work += [(qrows, krows, r - rstart, h, slice(N_HD * h, N_HD * (h + 1))) for h in range(2)]
        q = [q_ref[qr, sl] * (N_HD ** -0.5) for qr, _, _, _, sl in work]
        s_loc = [_dot(q_, k_ref[kr, sl], NT) + tab_ref[h, var] for q_, (_, kr, var, h, sl) in zip(q, work)]
        s_ctx = [_dot(q_, kc_ref[0, h], NT) for q_, (_, _, _, h, _) in zip(q, work)]
        m = [jnp.maximum(jnp.max(a, axis=-1, keepdims=True), jnp.max(c, axis=-1, keepdims=True))
             for a, c in zip(s_loc, s_ctx)]
        p_loc = [jnp.exp(a - m_) for a, m_ in zip(s_loc, m)]
        p_ctx = [jnp.exp(c - m_) for c, m_ in zip(s_ctx, m)]
        o = [_dot(a, v_ref[kr, sl]) + _dot(c, vc_ref[0, h])
             for a, c, (_, kr, _, h, sl) in zip(p_loc, p_ctx, work)]
        for o_, a, c, (qr, _, _, _, sl) in zip(o, p_loc, p_ctx, work):
            y_ref[qr, sl] = o_ / (jnp.sum(a, axis=-1, keepdims=True) + jnp.sum(c, axis=-1, keepdims=True))
        return carry

    lax.fori_loop(0, rows_per_step // unroll, rows, 0)


def _natten_call(z, k_ctx, v_ctx, tab, prev, *, row0, nb, t, rows_per_step=8):
    rows_n = t // GRID_W
    tq = rows_per_step * GRID_W
    qblk0 = row0 // tq
    sblk0 = row0 // t
    n_rb = rows_n // rows_per_step
    past = k_ctx.shape[2]
    kv_spec = lambda cb: pl.BlockSpec((t, LANE), lambda b, hp, rb: (sblk0 + b, cb + hp))
    cache_spec = pl.BlockSpec((1, 2, past, N_HD), lambda b, hp, rb: (b, hp, 0, 0))
    return pl.pallas_call(
        functools.partial(_natten_body, rows_per_step=rows_per_step, rows_n=rows_n),
        out_shape=jax.ShapeDtypeStruct((z.shape[0], HW), F32),
        grid=(nb, N_HEADS // 2, n_rb),
        in_specs=[pl.BlockSpec((tq, LANE), lambda b, hp, rb: (qblk0 + b * n_rb + rb, ZB_NQ + hp)),
                  kv_spec(ZB_NK), kv_spec(ZB_NV), cache_spec, cache_spec,
                  pl.BlockSpec((2, WIN_ROWS, GRID_W, WIN_ROWS * GRID_W), lambda b, hp, rb: (hp, 0, 0, 0)),
                  pl.BlockSpec(memory_space=pl.ANY)],
        out_specs=pl.BlockSpec((tq, LANE), lambda b, hp, rb: (qblk0 + b * n_rb + rb, hp)),
        input_output_aliases={6: 0},
        compiler_params=_params(("parallel", "parallel", "arbitrary")),
        name="neighbourhood_attention",
    )(z, z, z, k_ctx, v_ctx, tab, prev)


def _natten_table(rpb):
    cq = jnp.arange(GRID_W)[:, None]
    ck = jnp.arange(GRID_W)[None, :]
    dc = jnp.clip(ck - cq, -(WIN_COLS - 1), WIN_COLS - 1) + (WIN_COLS - 1)
    cstart = jnp.clip(cq - WIN_COLS // 2, 0, GRID_W - WIN_COLS)
    in_band = (ck >= cstart) & (ck < cstart + WIN_COLS)
    onehot = ((dc[None] == jnp.arange(2 * WIN_COLS - 1)[:, None, None]) & in_band[None]).astype(F32)
    cols = jnp.einsum('hrd,dqk->hrqk', rpb, onehot, precision=lax.Precision.HIGHEST)
    cols = cols + jnp.where(in_band, 0.0, NEG)
    tab = jnp.stack([cols[:, WIN_ROWS - 1 - var:2 * WIN_ROWS - 1 - var] for var in range(WIN_ROWS)], axis=1)
    return tab.transpose(0, 1, 3, 2, 4).reshape(rpb.shape[0], WIN_ROWS, GRID_W, WIN_ROWS * GRID_W)


def _group_norm(y, avg_bf, eps):
    mu = _dot_lhs2(y, avg_bf)
    yc = y - mu
    var = _dot_lhs2(yc * yc, avg_bf)
    return yc * lax.rsqrt(var + eps)


def _merge_body(*refs, with_router):
    refs = list(refs)
    (x_ref, mod_ref, yr_ref, bv_ref, hm_ref, yn_ref, gl_ref, mo_ref, g0_ref, g1_ref, g2_ref,
     avg64_ref, avg128_ref, rg2_ref, lnw_ref, lnb_ref, mlw_ref, wor_ref, wom_ref, won_ref, wout_ref,
     gpost_ref, gpre_ref) = [refs.pop(0) for _ in range(23)]
    wr_ref, br_ref = (refs.pop(0), refs.pop(0)) if with_router else (None, None)
    x1_ref, hin_ref = refs.pop(0), refs.pop(0)
    hin_hi_ref, sel_ref = (refs.pop(0), refs.pop(0)) if with_router else (None, None)
    m = mod_ref[0]
    yr = yr_ref[0] + yr_ref[1]
    yn_r = _group_norm(yr, avg64_ref[...], RWKV_GN_EPS) * lnw_ref[...] + lnb_ref[...]
    g = _dot(jax.nn.sigmoid(gl_ref[...]), rg2_ref[...])
    out_r = (yn_r + bv_ref[0] + bv_ref[1]) * g
    hn = _group_norm(hm_ref[0] + hm_ref[1], avg128_ref[...], NORM_EPS) * mlw_ref[...]
    out_m = jax.nn.sigmoid(mo_ref[...]) * hn
    merged = (jax.nn.sigmoid(g0_ref[...]) * _dot(out_r, wor_ref[...])
              + jax.nn.sigmoid(g1_ref[...]) * _dot(out_m, wom_ref[...])
              + jax.nn.sigmoid(g2_ref[...]) * _dot(yn_ref[...], won_ref[...]))
    o = _dot(merged, wout_ref[...])
    x1 = x_ref[...] + m[2:3] * _rms(o, gpost_ref[...])
    x1_ref[...] = x1
    hin = _rms(x1, gpre_ref[...]) * (1.0 + m[4:5]) + m[3:4]
    if with_router:
        sel_ref[...] = _top2_select(_dot3(hin, wr_ref[...]) + br_ref[...])
        hin_ref[...] = hin[:, :D // 2]
        hin_hi_ref[...] = hin[:, D // 2:]
    else:
        hin_ref[...] = hin.astype(BF16)


def _merge_call(x, mod3, yr, bv, hm, yn, z, W, n_ctx, lat_t, router, tm=256):
    ntok = x.shape[0]
    midx = _mod_index(tm, n_ctx, lat_t)
    with_router = router is not None
    row = lambda w: pl.BlockSpec((tm, w), lambda i: (i, 0))
    dirs = pl.BlockSpec((2, tm, HW), lambda i: (0, i, 0))
    zspec = lambda w, cb: pl.BlockSpec((tm, w), lambda i: (i, cb))
    full = lambda a: pl.BlockSpec(a.shape, lambda i: (0,) * a.ndim)
    consts = [W['avg64'], W['avg128'], W['rg2'], W['lnw'], W['lnb'], W['mlw'], W['wor'], W['wom'], W['won'],
              W['wout'], W['gpost'], W['gpre']]
    in_specs = [row(D), pl.BlockSpec((1, 8, D), lambda i: (midx(i), 0, 0)), dirs, dirs, dirs, row(HW),
                zspec(LANE, ZB_GL), zspec(HW, ZB_MO // 4),
                zspec(D, ZB_GATE // 8), zspec(D, ZB_GATE // 8 + 1), zspec(D, ZB_GATE // 8 + 2)]
    in_specs += [full(a) for a in consts]
    args = [x, mod3, yr, bv, hm, yn, z, z, z, z, z] + consts
    if with_router:
        half = jax.ShapeDtypeStruct((ntok, D // 2), F32)
        out_shape = [jax.ShapeDtypeStruct((ntok, D), F32), half, half]
        out_specs = [row(D), row(D // 2), row(D // 2)]
    else:
        out_shape = [jax.ShapeDtypeStruct((ntok, D), F32), jax.ShapeDtypeStruct((ntok, D), BF16)]
        out_specs = [row(D), row(D)]
    if with_router:
        in_specs += [full(router[0]), full(router[1])]
        args += list(router)
        out_shape.append(jax.ShapeDtypeStruct((ntok, LANE), F32))
        out_specs.append(row(LANE))
    return pl.pallas_call(
        functools.partial(_merge_body, with_router=with_router),
        out_shape=out_shape,
        grid=(ntok // tm,),
        in_specs=in_specs,
        out_specs=out_specs,
        compiler_params=_params(("parallel",)),
        name="branch_merge",
    )(*args)


def _top2_select(lg):
    lane = lax.broadcasted_iota(jnp.int32, lg.shape, 1)
    m1 = jnp.max(lg, axis=1, keepdims=True)
    i1 = jnp.min(jnp.where(lg == m1, lane, LANE), axis=1, keepdims=True)
    lg2 = jnp.where(lane == i1, -jnp.inf, lg)
    m2 = jnp.max(lg2, axis=1, keepdims=True)
    i2 = jnp.min(jnp.where(lg2 == m2, lane, LANE), axis=1, keepdims=True)
    e2 = jnp.exp(m2 - m1)
    den = 1.0 + e2
    return jnp.where(lane == 0, i1.astype(F32),
                     jnp.where(lane == 1, i2.astype(F32),
                               jnp.where(lane == 2, 1.0 / den, jnp.where(lane == 3, e2 / den, 0.0))))


def _ffn_body(h_ref, x1_ref, mod_ref, gpost_ref, wg_ref, wu_ref, wd_ref, o_ref, acc_s):
    j = pl.program_id(1)

    @pl.when(j == 0)
    def _():
        acc_s[...] = jnp.zeros_like(acc_s)

    hin = h_ref[...]
    hg = jnp.dot(hin, wg_ref[...], preferred_element_type=F32)
    hu = jnp.dot(hin, wu_ref[...], preferred_element_type=F32)
    acc_s[...] += _dot(hg * jax.nn.sigmoid(hg) * hu, wd_ref[...])

    @pl.when(j == pl.num_programs(1) - 1)
    def _():
        m = mod_ref[0]
        o_ref[...] = x1_ref[...] + m[5:6] * _rms(acc_s[...], gpost_ref[...])


def _ffn_call(hin, x1, mod3, gpost, wg, wu, wd, n_ctx, lat_t, tm=512, tf=1408):
    ntok = hin.shape[0]
    dff = wg.shape[1]
    midx = _mod_index(tm, n_ctx, lat_t)
    return pl.pallas_call(
        _ffn_body,
        out_shape=jax.ShapeDtypeStruct((ntok, D), F32),
        grid=(ntok // tm, dff // tf),
        in_specs=[pl.BlockSpec((tm, D), lambda i, j: (i, 0)),
                  pl.BlockSpec((tm, D), lambda i, j: (i, 0)),
                  pl.BlockSpec((1, 8, D), lambda i, j: (midx(i), 0, 0)),
                  pl.BlockSpec((1, D), lambda i, j: (0, 0)),
                  pl.BlockSpec((D, tf), lambda i, j: (0, j)),
                  pl.BlockSpec((D, tf), lambda i, j: (0, j)),
                  pl.BlockSpec((tf, D), lambda i, j: (j, 0))],
        out_specs=pl.BlockSpec((tm, D), lambda i, j: (i, 0)),
        scratch_shapes=[pltpu.VMEM((tm, D), F32)],
        compiler_params=_params(("parallel", "arbitrary")),
        name="dense_swiglu",
    )(hin, x1, mod3, gpost, wg, wu, wd)


MOE_TILE = 512
MOE_TF = 896
CMB_TILE = 128
SLAB_ALIGN = 8
SLAB = CMB_TILE + SLAB_ALIGN


def _moe_plan(sel, ntok):
    i32 = jnp.int32
    e1, e2 = sel[:, 0].astype(i32), sel[:, 1].astype(i32)
    ar = jnp.arange(ntok, dtype=i32)
    ex = jnp.arange(N_EXPERTS, dtype=i32)
    oh = ((e1[:, None] == ex) | (e2[:, None] == ex)).astype(i32)
    csum = jnp.cumsum(oh, axis=0)
    total = csum[-1]
    gsz = (total + MOE_TILE - 1) // MOE_TILE * MOE_TILE
    gend = jnp.cumsum(gsz)
    gstart = gend - gsz
    dstart = jnp.cumsum(total) - total
    key, wts = lax.sort((jnp.concatenate([e1 * ntok + ar, e2 * ntok + ar]),
                         jnp.concatenate([sel[:, 2], sel[:, 3]])), num_keys=1)
    tok_d = key % ntok
    n_pad = 2 * ntok + (N_EXPERTS + 1) * MOE_TILE
    p = jnp.arange(n_pad, dtype=i32)
    ge = jnp.minimum(jnp.sum((p[:, None] >= gend[None, :]).astype(i32), axis=1), N_EXPERTS - 1)
    pick = lambda tab: jnp.sum(jnp.where(ge[:, None] == ex, tab[None, :], 0), axis=1)
    q = p - pick(gstart)
    live = (q < pick(total)) & (p < gend[-1])
    src = jnp.clip(pick(dstart) + q, 0, 2 * ntok - 1)
    tok_p = jnp.where(live, tok_d[src], 0)
    w_p = jnp.where(live, wts[src], 0.0)
    tstart = jnp.arange(n_pad // MOE_TILE, dtype=i32) * MOE_TILE
    texp = jnp.minimum(jnp.sum((tstart[:, None] >= gend[None, :]).astype(i32), axis=1), N_EXPERTS - 1)
    tval = (tstart < gend[-1]).astype(i32)
    r0 = (csum - oh)[::CMB_TILE]
    r1 = jnp.concatenate([r0[1:], total[None]], axis=0)
    alo = (gstart[None, :] + r0).reshape(-1)
    ahi = (gstart[None, :] + r1).reshape(-1)
    return dict(tok=tok_p, w=w_p.reshape(n_pad, 1), texp=texp, tval=tval,
                a8=alo // SLAB_ALIGN, alo=alo, ahi=ahi, n_pad=n_pad)


def _moe_gather_body(tok_ref, src_ref, o_ref, buf_s):
    base = pl.program_id(0) * MOE_TILE

    def gather(r, carry):
        buf_s[pl.ds(r, 1), :] = src_ref[pl.ds(tok_ref[base + r], 1), :]
        return carry

    lax.fori_loop(0, MOE_TILE, gather, 0, unroll=8)
    o_ref[...] = buf_s[...].astype(BF16)


def _moe_gather_call(src, plan):
    n_pad = plan['n_pad']
    width = src.shape[1]
    grid_spec = pltpu.PrefetchScalarGridSpec(
        num_scalar_prefetch=1,
        grid=(n_pad // MOE_TILE,),
        in_specs=[pl.BlockSpec(memory_space=pltpu.VMEM)],
        out_specs=pl.BlockSpec((MOE_TILE, width), lambda i, tk: (i, 0)),
        scratch_shapes=[pltpu.VMEM((MOE_TILE, width), F32)])
    return pl.pallas_call(
        _moe_gather_body,
        out_shape=jax.ShapeDtypeStruct((n_pad, width), BF16),
        grid_spec=grid_spec,
        compiler_params=_params(("arbitrary",)),
        name="moe_row_gather",
    )(plan['tok'], src)


def _moe_group_body(texp_ref, tval_ref, xa_ref, xb_ref, ws_ref, wg_ref, wu_ref, wd_ref, ys_ref, acc_s):
    del texp_ref
    i = pl.program_id(0)
    j = pl.program_id(1)
    half = D // 2

    @pl.when(j == 0)
    def _():
        acc_s[...] = jnp.zeros_like(acc_s)

    @pl.when(tval_ref[i] > 0)
    def _():
        xa, xb = xa_ref[...], xb_ref[...]
        mm = lambda u, ref, rows: jnp.dot(u, ref[0, rows, :], preferred_element_type=F32)
        hg = mm(xa, wg_ref, slice(0, half)) + mm(xb, wg_ref, slice(half, D))
        hu = mm(xa, wu_ref, slice(0, half)) + mm(xb, wu_ref, slice(half, D))
        acc_s[...] += _dot(hg * jax.nn.sigmoid(hg) * hu, wd_ref[0])

    @pl.when(j == pl.num_programs(1) - 1)
    def _():
        ys_ref[...] = acc_s[...] * ws_ref[...]


def _moe_group_call(xa, xb, plan, wg, wu, wd):
    n_pad = plan['n_pad']
    dff = wg.shape[2]
    xspec = pl.BlockSpec((MOE_TILE, D // 2), lambda i, j, te, tv: (i, 0))
    grid_spec = pltpu.PrefetchScalarGridSpec(
        num_scalar_prefetch=2,
        grid=(n_pad // MOE_TILE, dff // MOE_TF),
        in_specs=[xspec, xspec,
                  pl.BlockSpec((MOE_TILE, 1), lambda i, j, te, tv: (i, 0)),
                  pl.BlockSpec((1, D, MOE_TF), lambda i, j, te, tv: (te[i], 0, j)),
                  pl.BlockSpec((1, D, MOE_TF), lambda i, j, te, tv: (te[i], 0, j)),
                  pl.BlockSpec((1, MOE_TF, D), lambda i, j, te, tv: (te[i], j, 0))],
        out_specs=pl.BlockSpec((MOE_TILE, D), lambda i, j, te, tv: (i, 0)),
        scratch_shapes=[pltpu.VMEM((MOE_TILE, D), F32)])
    return pl.pallas_call(
        _moe_group_body,
        out_shape=jax.ShapeDtypeStruct((n_pad, D), F32),
        grid_spec=grid_spec,
        compiler_params=_params(("parallel", "arbitrary")),
        name="moe_expert_swiglu",
    )(plan['texp'], plan['tval'], xa, xb, plan['w'], wg, wu, wd)


def _moe_combine_body(a8_ref, alo_ref, ahi_ref, *refs, tile0):
    ys_refs, tk_refs = refs[:N_EXPERTS], refs[N_EXPERTS:2 * N_EXPERTS]
    x1_ref, mod_ref, gpost_ref, o_ref = refs[2 * N_EXPERTS:]
    t = pl.program_id(0) + tile0
    tloc = t * CMB_TILE + lax.broadcasted_iota(jnp.int32, (SLAB, CMB_TILE), 1)
    srow = lax.broadcasted_iota(jnp.int32, (SLAB, 1), 0)
    f = jnp.zeros((CMB_TILE, D), F32)
    for e in range(N_EXPERTS):
        k = t * N_EXPERTS + e
        row = a8_ref[k] * SLAB_ALIGN + srow
        tok = jnp.where((row >= alo_ref[k]) & (row < ahi_ref[k]), tk_refs[e][...], -1)
        onehot = jnp.where(tok == tloc, 1.0, 0.0).astype(BF16)
        f = f + _dot_rhs2(onehot, ys_refs[e][...], TN)
    m = mod_ref[0]
    o_ref[...] = x1_ref[...] + m[5:6] * _rms(f, gpost_ref[...])


def _moe_combine_call(ys, plan, x1, mod3, gpost, n_ctx, lat_t, row0, nrows):
    midx = _mod_index(CMB_TILE, n_ctx, lat_t)
    tile0 = row0 // CMB_TILE

    def slab(w, e):
        return pl.BlockSpec((pl.Element(SLAB), pl.Element(w)),
                            lambda t, a8, lo, hi: (a8[(t + tile0) * N_EXPERTS + e] * SLAB_ALIGN, 0))

    experts = range(N_EXPERTS)
    grid_spec = pltpu.PrefetchScalarGridSpec(
        num_scalar_prefetch=3,
        grid=(nrows // CMB_TILE,),
        in_specs=[slab(D, e) for e in experts] + [slab(1, e) for e in experts] + [
            pl.BlockSpec((CMB_TILE, D), lambda t, a8, lo, hi: (t + tile0, 0)),
            pl.BlockSpec((1, 8, D), lambda t, a8, lo, hi: (midx(t + tile0), 0, 0)),
            pl.BlockSpec((1, D), lambda t, a8, lo, hi: (0, 0))],
        out_specs=pl.BlockSpec((CMB_TILE, D), lambda t, a8, lo, hi: (t, 0)))
    tok2d = plan['tok'].reshape(-1, 1)
    return pl.pallas_call(
        functools.partial(_moe_combine_body, tile0=tile0),
        out_shape=jax.ShapeDtypeStruct((nrows, D), F32),
        grid_spec=grid_spec,
        compiler_params=_params(("parallel",)),
        name="moe_combine",
    )(plan['a8'], plan['alo'], plan['ahi'], *([ys] * N_EXPERTS), *([tok2d] * N_EXPERTS), x1, mod3, gpost)


def _rope_tables(t):
    i = jnp.arange(M_HD)
    half, pair, f = i // 64, (i % 64) // 32, i % 32
    tt = jnp.arange(t)
    pos = jnp.stack([tt // GRID_W, tt % GRID_W], axis=-1).astype(F32)
    inv = ROPE_BASE ** (-f.astype(F32) / 32)
    ang = pos[:, half] * inv[None, :]
    src = jnp.where(pair == 0, i + 32, i - 32)
    perm = jnp.where(i[:, None] == src[None, :], jnp.where(pair == 0, -1.0, 1.0)[None, :], 0.0)
    return jnp.cos(ang), jnp.sin(ang), perm.astype(BF16)


def _pad_in_proj(w):
    off = {}
    o = 0
    for name, width in (('r', 512), ('k', 512), ('v', 512), ('wl', 128), ('al', 128), ('gl', 128),
                        ('mq', 512), ('mk', 512), ('mv', 512), ('mo', 512), ('mif', 16),
                        ('nq', 512), ('nk', 512), ('nv', 512), ('gate', 3072)):
        off[name] = (o, o + width)
        o += width
    col = lambda n: w[:, off[n][0]:off[n][1]].astype(BF16)
    parts = [col(n) for n in ('r', 'k', 'v', 'mq', 'mk', 'mv', 'mo', 'nq', 'nk', 'nv', 'gate', 'wl', 'al', 'gl', 'mif')]
    parts.append(jnp.zeros((w.shape[0], LANE - 16), BF16))
    return jnp.concatenate(parts, axis=1)


def _block_avg(width, group):
    i = jnp.arange(width) // group
    return jnp.where(i[:, None] == i[None, :], 1.0 / group, 0.0).astype(BF16)


def _zero_pad_rows(w2):
    z = jnp.zeros_like(w2[0])
    return jnp.stack([jnp.concatenate([w2[0], z], axis=0), jnp.concatenate([z, w2[1]], axis=0)])


def kernel(x_prompt, x_sample, state_rwkv, state_mlstm_C, state_mlstm_n, state_mlstm_m, cache_nat_k, cache_nat_v,
           c, c_ctx, w_mod, b_mod, g_pre_mix, g_post_mix, g_pre_ffn, g_post_ffn, w_in,
           rw_w0, rw_w2, rw_a0, rw_a2, rw_g2, rw_k_k, rw_k_a, rw_r_k, rw_ln_w, rw_ln_b,
           ml_b_i, ml_b_f, ml_norm_w, nat_rpb, w_o_rwkv, w_o_mlstm, w_o_nat, w_out,
           ff_w_gate, ff_w_up, ff_w_down, moe_w_router, moe_b_router, moe_w_gate, moe_w_up, moe_w_down):
    cb, ct = x_prompt.shape[:2]
    lb, lt = x_sample.shape[:2]
    n_ctx, n_lat = cb * ct, lb * lt
    x = jnp.concatenate([x_prompt.reshape(n_ctx, D), x_sample.reshape(n_lat, D)], axis=0)
    cvec = jnp.zeros((8, D), F32).at[0].set(c_ctx).at[1:1 + lb].set(c)
    rope = _rope_tables(lt)
    avg64, avg128 = _block_avg(HW, R_HD), _block_avg(HW, M_HD)
    s_rw = ml_st = kv_c = None
    for l in range(DEPTH):
        mod = _mod_call(cvec, w_mod[l], b_mod[l])
        mod3 = jnp.pad(mod[:1 + lb].reshape(1 + lb, 6, D), ((0, 0), (0, 2), (0, 0)))
        z = _proj_call(x, mod3, g_pre_mix[l], _pad_in_proj(w_in[l]), n_ctx, lt)

        rp = dict(w0=rw_w0[l].reshape(2, 1, HW), w2=_zero_pad_rows(rw_w2[l]),
                  a0=rw_a0[l].reshape(2, 1, HW), a2=_zero_pad_rows(rw_a2[l]),
                  kk=rw_k_k[l].reshape(1, HW), ka=rw_k_a[l].reshape(1, HW), rk=rw_r_k[l].reshape(1, HW))
        yr, bv, s_rw = _rwkv_call(z, rp, None, None, row0=0, nb=cb, t=ct, tb=RW_BLOCK, layer=l,
                                  prev_state=None if s_rw is None else (s_rw,))
        yr, bv = _rwkv_call(z, rp, state_rwkv[:, l], (yr, bv), row0=n_ctx, nb=lb, t=lt, tb=RW_BLOCK)

        gt = z[:, ZB_MIF * LANE:ZB_MIF * LANE + 16].reshape(-1, CHUNK, 16).transpose(0, 2, 1)
        hm, *ml_st = _mlstm_call(z, gt, None, ml_b_i[l], ml_b_f[l], None, None,
                                 row0=0, nb=cb, t=ct, tb=ML_BLOCK, layer=l, prev_state=ml_st)
        init = (state_mlstm_C[:, l], state_mlstm_n[:, l][:, :, :, None, :], state_mlstm_m[:, l])
        hm, = _mlstm_call(z, gt, rope, ml_b_i[l], ml_b_f[l], init, (hm,),
                          row0=n_ctx, nb=lb, t=lt, tb=ML_BLOCK)

        yn, *kv_c = _ctx_attn_call(z, kv_c, nb=cb, t=ct, layer=l)
        yn = _natten_call(z, cache_nat_k[:, l], cache_nat_v[:, l], _natten_table(nat_rpb[l]), yn,
                          row0=n_ctx, nb=lb, t=lt)

        mw = dict(avg64=avg64, avg128=avg128, rg2=rw_g2[l].astype(BF16), lnw=rw_ln_w[l].reshape(1, HW),
                  lnb=rw_ln_b[l].reshape(1, HW), mlw=ml_norm_w[l].reshape(1, HW),
                  wor=w_o_rwkv[l].astype(BF16), wom=w_o_mlstm[l].astype(BF16), won=w_o_nat[l].astype(BF16),
                  wout=w_out[l].astype(BF16), gpost=g_post_mix[l].reshape(1, D), gpre=g_pre_ffn[l].reshape(1, D))
        j = l // 2
        if l % 2 == 0:
            router = None
        else:
            wr = jnp.pad(moe_w_router[j], ((0, 0), (0, LANE - N_EXPERTS)))
            br = jnp.pad(moe_b_router[j], (0, LANE - N_EXPERTS), constant_values=NEG).reshape(1, LANE)
            router = (wr, br)
        merged = _merge_call(x, mod3, yr, bv, hm, yn, z, mw, n_ctx, lt, router)
        gpost = g_post_ffn[l].reshape(1, D)
        if l % 2 == 0:
            x = _ffn_call(merged[1], merged[0], mod3, gpost, ff_w_gate[j].astype(BF16),
                          ff_w_up[j].astype(BF16), ff_w_down[j].astype(BF16), n_ctx, lt)
        else:
            plan = _moe_plan(merged[3], n_ctx + n_lat)
            xa, xb = _moe_gather_call(merged[1], plan), _moe_gather_call(merged[2], plan)
            ys = _moe_group_call(xa, xb, plan, moe_w_gate[j].astype(BF16), moe_w_up[j].astype(BF16),
                                 moe_w_down[j].astype(BF16))
            groups = [_moe_combine_call(ys, plan, merged[0], mod3, gpost, n_ctx, lt, r0, nr)
                      for r0, nr in ((0, n_ctx), (n_ctx, n_lat))]
            x = jnp.concatenate(groups, axis=0) if l + 1 < DEPTH else None
        if x is not None:
            groups = [x[:n_ctx], x[n_ctx:]]

    c_m, n_m, m_m = ml_st
    return (groups[0].reshape(cb, ct, D), groups[1].reshape(lb, lt, D),
            s_rw, c_m, n_m[:, :, :, :, 0, :], m_m[:, :, :, :, 0, 0], kv_c[0], kv_c[1])
```

```python
import functools
import math

import jax
import jax.numpy as jnp
from jax import lax
from jax.experimental import pallas as pl
from jax.experimental.pallas import tpu as pltpu

F32 = jnp.float32
BF16 = jnp.bfloat16

D = 1024
DEPTH = 2
GRID_W = 64
R_HEADS, R_HD = 8, 64
M_HEADS, M_HD = 4, 128
N_HEADS, N_HD = 8, 64
HW = 512
WIN_ROWS, WIN_COLS = 8, 16
N_EXPERTS = 8
RWKV_GN_EPS = 64e-5
NORM_EPS = 1e-6
ROPE_BASE = 10000.0
CHUNK = 64
NEG = -1e30
LANE = 128
VMEM_LIMIT = 56 * 1024 * 1024

ZB_R, ZB_K, ZB_V = 0, 4, 8
ZB_MQ, ZB_MK, ZB_MV, ZB_MO = 12, 16, 20, 24
ZB_NQ, ZB_NK, ZB_NV = 28, 32, 36
ZB_GATE = 40
ZB_WL, ZB_AL, ZB_GL, ZB_MIF = 64, 65, 66, 67
P_PAD = 68 * LANE

NN = (((1,), (0,)), ((), ()))
NT = (((1,), (1,)), ((), ()))
TN = (((0,), (0,)), ((), ()))


def _dot(a, b, dims=NN):
    return lax.dot_general(a.astype(BF16), b.astype(BF16), dims, preferred_element_type=F32)


def _split(x):
    hi = x.astype(BF16)
    return hi, (x - hi.astype(F32)).astype(BF16)


def _dot3(a, b, dims=NN):
    ah, al = _split(a)
    bh, bl = _split(b)
    d = lambda u, w: lax.dot_general(u, w, dims, preferred_element_type=F32)
    return d(ah, bh) + (d(ah, bl) + d(al, bh))


def _dot_lhs2(a, b_exact, dims=NN):
    ah, al = _split(a)
    d = lambda u: lax.dot_general(u, b_exact, dims, preferred_element_type=F32)
    return d(ah) + d(al)


def _dot_rhs2(a_exact, b, dims=NN):
    bh, bl = _split(b)
    d = lambda w: lax.dot_general(a_exact, w, dims, preferred_element_type=F32)
    return d(bh) + d(bl)


def _log_sigmoid(x):
    return jnp.minimum(x, 0.0) - jnp.log1p(jnp.exp(-jnp.abs(x)))


def _rms(x, g):
    return x * lax.rsqrt(jnp.mean(x * x, axis=-1, keepdims=True) + NORM_EPS) * g


def _params(sem):
    return pltpu.CompilerParams(dimension_semantics=sem, vmem_limit_bytes=VMEM_LIMIT)


def _alias_prev(in_specs, args, prev, first_out=0):
    aliases = {}
    for k, a in enumerate(prev or ()):
        aliases[len(args)] = first_out + k
        in_specs.append(pl.BlockSpec(memory_space=pl.ANY))
        args.append(a)
    return aliases


def _mod_body(c_ref, w_ref, b_ref, o_ref):
    c = c_ref[...]
    o_ref[...] = _dot(c * jax.nn.sigmoid(c), w_ref[...]) + b_ref[...]


def _mod_call(cvec, w_mod, b_mod):
    tn = 1536
    return pl.pallas_call(
        _mod_body,
        out_shape=jax.ShapeDtypeStruct((8, 6 * D), F32),
        grid=(6 * D // tn,),
        in_specs=[pl.BlockSpec((8, D), lambda j: (0, 0)),
                  pl.BlockSpec((D, tn), lambda j: (0, j)),
                  pl.BlockSpec((1, tn), lambda j: (0, j))],
        out_specs=pl.BlockSpec((8, tn), lambda j: (0, j)),
        compiler_params=_params(("arbitrary",)),
        name="adaln_mod",
    )(cvec, w_mod, b_mod.reshape(1, 6 * D))


def _mod_index(tile_rows, n_ctx, lat_t):
    def f(i):
        start = i * tile_rows
        return jnp.where(start < n_ctx, 0, 1 + (start - n_ctx) // lat_t)
    return f


def _proj_body(x_ref, mod_ref, g_ref, w_ref, z_ref, hin_s):
    @pl.when(pl.program_id(1) == 0)
    def _():
        m = mod_ref[0]
        hin_s[...] = (_rms(x_ref[...], g_ref[...]) * (1.0 + m[1:2]) + m[0:1]).astype(BF16)
    z_ref[...] = jnp.dot(hin_s[...], w_ref[...], preferred_element_type=F32)


def _proj_call(x, mod3, g, w_bf, n_ctx, lat_t, tm=1024, tn=2176):
    ntok = x.shape[0]
    midx = _mod_index(tm, n_ctx, lat_t)
    return pl.pallas_call(
        _proj_body,
        out_shape=jax.ShapeDtypeStruct((ntok, P_PAD), F32),
        grid=(ntok // tm, P_PAD // tn),
        in_specs=[pl.BlockSpec((tm, D), lambda i, j: (i, 0)),
                  pl.BlockSpec((1, 8, D), lambda i, j: (midx(i), 0, 0)),
                  pl.BlockSpec((1, D), lambda i, j: (0, 0)),
                  pl.BlockSpec((D, tn), lambda i, j: (0, j))],
        out_specs=pl.BlockSpec((tm, tn), lambda i, j: (i, j)),
        scratch_shapes=[pltpu.VMEM((tm, D), BF16)],
        compiler_params=_params(("parallel", "arbitrary")),
        name="in_proj",
    )(x, mod3, g.reshape(1, D), w_bf)


RW_TILE = 2 * CHUNK
RW_BLOCK = 1024
ML_BLOCK = 512


def _rwkv_body(*refs, tb_rows, nseq, has_init, has_out, n_alias):
    (r_ref, k_ref, v_ref, wl_ref, al_ref, w0_ref, w2_ref, a0_ref, a2_ref,
     kkw_ref, kaw_ref, rkw_ref, hsum_ref) = refs[:13]
    rest = list(refs[13:])
    s0_ref = rest.pop(0) if has_init else None
    del rest[:n_alias]
    y_ref, bv_ref = rest.pop(0), rest.pop(0)
    sout_ref = rest.pop(0) if has_out else None
    r2_s, m_s, g_s, st_s = rest
    C = CHUNK
    TL = RW_TILE
    nchunk = tb_rows // nseq // C
    d = pl.program_id(2)
    tb = pl.program_id(3)
    n_tb = pl.num_programs(3)

    ti = lax.broadcasted_iota(jnp.int32, (TL, TL), 0)
    si = lax.broadcasted_iota(jnp.int32, (TL, TL), 1)
    head_blk = (ti >> 6) == (si >> 6)

    @pl.when(tb == 0)
    def _():
        for q in range(nseq):
            if has_init:
                s0 = s0_ref[q, 0]
                two = jnp.concatenate([jnp.concatenate([s0[0], s0[0]], axis=1),
                                       jnp.concatenate([s0[1], s0[1]], axis=1)], axis=0)
                st_s[q] = jnp.where(head_blk, two, 0.0)
            else:
                st_s[q] = jnp.zeros((LANE, LANE), F32)

    hsum = hsum_ref[...]
    r = r_ref[...]
    kf = k_ref[...]
    v = v_ref[...]
    w_pre = w0_ref[0] + _dot(jnp.tanh(wl_ref[...]), w2_ref[0])
    lw = -math.exp(-0.5) * jax.nn.sigmoid(w_pre)
    a = jax.nn.sigmoid(a0_ref[0] + _dot(al_ref[...], a2_ref[0]))
    kd = kf * (1.0 + (a - 1.0) * kaw_ref[...])
    kkf = kf * kkw_ref[...]
    kn = kkf * lax.rsqrt(_dot_lhs2(kkf * kkf, hsum) + 1e-12)
    bd = kn * a
    bv_ref[0] = _dot_lhs2(r * kd * rkw_ref[...], hsum) * v

    dlt = (ti - si) * (1 - 2 * d)
    strict = head_blk & (dlt > 0)
    incl = head_blk & (dlt >= 0)
    cum_ones = jnp.concatenate([jnp.where(incl, 1.0, 0.0), jnp.where(head_blk, 1.0, 0.0)], axis=0).astype(BF16)
    is_diag = ti == si
    eye = jnp.where(is_diag, 1.0, 0.0)
    pair = (ti >> 1) == (si >> 1)
    sibling = [((ti >> lg) ^ (si >> lg)) == 1 for lg in range(1, 6)]
    head0 = si < R_HD

    n_tiles = tb_rows // TL
    tiles = []
    for tile in range(n_tiles):
        rows = slice(TL * tile, TL * (tile + 1))
        cs = _dot_rhs2(cum_ones, lw[rows])
        cum, tot = cs[:TL], cs[TL:]
        e_neg = jnp.exp(-cum)
        e_hat = jnp.exp(tot - cum)
        rt = r[rows] * jnp.exp(cum)
        at = -kn[rows] * jnp.exp(cum - lw[rows])
        tiles.append(dict(rows=rows, tot=tot, rt=rt, at=at, bh=bd[rows] * e_hat, kh=kd[rows] * e_hat,
                          rhs=jnp.concatenate([bd[rows] * e_neg, kd[rows] * e_neg], axis=0).astype(BF16),
                          at_bf=at.astype(BF16), v_bf=v[rows].astype(BF16)))
    chains = [(tl, h) for tl in tiles for h in range(2)]
    a_all = []
    for tl, h in chains:
        hm = head0 if h == 0 else ~head0
        lhs = jnp.concatenate([jnp.where(hm, tl['at'], 0.0), jnp.where(hm, tl['rt'], 0.0)], axis=0)
        a_all.append(_dot(lhs, tl['rhs'], NT))
    n_mat = [jnp.where(strict, a[:TL, :TL], 0.0) for a in a_all]
    a_ak = [jnp.where(strict, a[:TL, TL:], 0.0).astype(BF16) for a in a_all]
    a_rb = [jnp.where(incl, a[TL:, :TL], 0.0).astype(BF16) for a in a_all]
    a_rk = [jnp.where(incl, a[TL:, TL:], 0.0).astype(BF16) for a in a_all]
    t_inv = [eye + jnp.where(pair, n, 0.0) for n in n_mat]
    for sib in sibling:
        x_mat = [_dot(jnp.where(sib, n, 0.0), t) for n, t in zip(n_mat, t_inv)]
        t_inv = [t + _dot(t, x) for t, x in zip(t_inv, x_mat)]
    p_h = [_dot(t, tl['at_bf']) for t, (tl, _) in zip(t_inv, chains)]
    av = [_dot(a, tl['v_bf']) for a, (tl, _) in zip(a_ak, chains)]
    q_h = [_dot(t, x) for t, x in zip(t_inv, av)]
    r2_h = [_dot(a, p) for a, p in zip(a_rb, p_h)]
    y0_h = [_dot(a, q) + _dot(k_, tl['v_bf']) for a, q, k_, (tl, _) in zip(a_rb, q_h, a_rk, chains)]
    p_m, q_m = [], []
    for i, tl in enumerate(tiles):
        rows = tl['rows']
        pick = lambda u: jnp.where(head0, u[2 * i], u[2 * i + 1])
        p_m.append(pick(p_h))
        q_m.append(pick(q_h))
        r2_s[rows, :] = tl['rt'] + pick(r2_h)
        y_ref[0, rows, :] = pick(y0_h)
        for c in range(TL // C):
            cr = slice(C * c, C * (c + 1))
            decay = jnp.where(is_diag, jnp.exp(tl['tot'][C * c:C * c + 1]), 0.0)
            m_s[i * (TL // C) + c] = decay + jnp.where(head_blk, _dot(p_m[i][cr], tl['bh'][cr], TN), 0.0)
            g_s[i * (TL // C) + c] = jnp.where(
                head_blk, _dot(q_m[i][cr], tl['bh'][cr], TN) + _dot(v[rows][cr], tl['kh'][cr], TN), 0.0)

    def step(i, states):
        ci = jnp.where(d == 0, i, nchunk - 1 - i)
        new = []
        for q, s in enumerate(states):
            cq = q * nchunk + ci
            rows = pl.ds(pl.multiple_of(cq * C, C), C)
            y_ref[0, rows, :] = y_ref[0, rows, :] + _dot(r2_s[rows, :], s, NT)
            new.append(_dot3(s, m_s[cq]) + g_s[cq])
        return tuple(new)

    s_fin = lax.fori_loop(0, nchunk, step, tuple(st_s[q] for q in range(nseq)))
    for q in range(nseq):
        st_s[q] = s_fin[q]

    if has_out:
        @pl.when(tb == n_tb - 1)
        def _():
            for q in range(nseq):
                sout_ref[q, 0, 0, 0] = s_fin[q][:R_HD, :R_HD]
                sout_ref[q, 0, 0, 1] = s_fin[q][R_HD:, R_HD:]


def _rwkv_call(z, P, s0, prev, *, row0, nb, t, tb, layer=None, prev_state=None):
    nseq = max(tb // t, 1)
    n_tb = max(t // tb, 1)
    blk0 = row0 // tb
    has_init = s0 is not None

    def rowblk(b, d, i):
        return blk0 + b * n_tb + jnp.where(d == 0, i, n_tb - 1 - i)

    def zspec(cb):
        return pl.BlockSpec((tb, LANE), lambda b, hp, d, i: (rowblk(b, d, i), cb(hp)))

    vec = lambda: pl.BlockSpec((1, LANE), lambda b, hp, d, i: (0, hp))
    dvec = lambda: pl.BlockSpec((1, 1, LANE), lambda b, hp, d, i: (d, 0, hp))
    dmat = lambda: pl.BlockSpec((1, LANE, LANE), lambda b, hp, d, i: (d, 0, hp))
    st_spec = pl.BlockSpec((nseq, 1, 2, R_HD, R_HD), lambda b, hp, d, i: (b, d, hp, 0, 0))
    in_specs = [zspec(lambda hp: ZB_R + hp), zspec(lambda hp: ZB_K + hp), zspec(lambda hp: ZB_V + hp),
                zspec(lambda hp: ZB_WL), zspec(lambda hp: ZB_AL),
                dvec(), dmat(), dvec(), dmat(), vec(), vec(), vec(),
                pl.BlockSpec((LANE, LANE), lambda b, hp, d, i: (0, 0))]
    args = [z, z, z, z, z, P['w0'], P['w2'], P['a0'], P['a2'], P['kk'], P['ka'], P['rk'],
            _block_avg(LANE, R_HD) * R_HD]
    if has_init:
        in_specs.append(st_spec)
        args.append(s0)
    has_out = layer is not None
    aliases = _alias_prev(in_specs, args, prev)
    aliases.update(_alias_prev(in_specs, args, prev_state, first_out=2))
    yspec = pl.BlockSpec((1, tb, LANE), lambda b, hp, d, i: (d, rowblk(b, d, i), hp))
    out_shape = [jax.ShapeDtypeStruct((2, z.shape[0], HW), F32)] * 2
    out_specs = [yspec, yspec]
    if has_out:
        out_shape.append(jax.ShapeDtypeStruct((nb, DEPTH, 2, R_HEADS, R_HD, R_HD), F32))
        out_specs.append(pl.BlockSpec((nseq, 1, 1, 2, R_HD, R_HD), lambda b, hp, d, i: (b, layer, d, hp, 0, 0)))
    return pl.pallas_call(
        functools.partial(_rwkv_body, tb_rows=tb, nseq=nseq, has_init=has_init, has_out=has_out,
                          n_alias=len(aliases)),
        out_shape=out_shape,
        grid=(nb // nseq, R_HEADS // 2, 2, n_tb),
        in_specs=in_specs,
        out_specs=out_specs,
        input_output_aliases=aliases,
        scratch_shapes=[pltpu.VMEM((tb, LANE), F32), pltpu.VMEM((tb // CHUNK, LANE, LANE), F32),
                        pltpu.VMEM((tb // CHUNK, LANE, LANE), F32), pltpu.VMEM((nseq, LANE, LANE), F32)],
        compiler_params=_params(("parallel", "parallel", "arbitrary", "arbitrary")),
        name="rwkv7_chunked",
    )(*args)


def _mlstm_body(*refs, tb_rows, nseq, use_rope, has_init, has_out, n_alias):
    refs = list(refs)
    q_ref, k_ref, v_ref, g_ref, gt_ref = [refs.pop(0) for _ in range(5)]
    cos_ref, sin_ref, perm_ref = [refs.pop(0) for _ in range(3)] if use_rope else (None,) * 3
    bi_ref, bf_ref = refs.pop(0), refs.pop(0)
    c0_ref, n0_ref, m0_ref = [refs.pop(0) for _ in range(3)] if has_init else (None,) * 3
    del refs[:n_alias]
    h_ref = refs.pop(0)
    cout_ref, nout_ref, mout_ref = [refs.pop(0) for _ in range(3)] if has_out else (None,) * 3
    sc_s, kv_s, nk_s, cprev_s, nprev_s, c_s, n_s, m_s = refs
    L = CHUNK
    nck = tb_rows // L
    ncs = nck // nseq
    b = pl.program_id(0)
    h = pl.program_id(1)
    d = pl.program_id(2)
    tb = pl.program_id(3)
    n_tb = pl.num_programs(3)
    bi = bi_ref[d, h]
    bf = bf_ref[d, h]
    ji = d * (2 * M_HEADS) + h
    jf = ji + M_HEADS

    @pl.when(tb == 0)
    def _():
        for q in range(nseq):
            if has_init:
                c_s[q] = c0_ref[q, 0, 0]
                n_s[q] = n0_ref[q, 0, 0]
                m_s[q] = jnp.full((1, M_HD), m0_ref[b * nseq + q, d, h], F32)
            else:
                c_s[q] = jnp.zeros((M_HD, M_HD), F32)
                n_s[q] = jnp.zeros((1, M_HD), F32)
                m_s[q] = jnp.zeros((1, M_HD), F32)

    g = g_ref[...]
    lane = lax.broadcasted_iota(jnp.int32, g.shape, 1)
    icol_all = jnp.sum(jnp.where(lane == ji, g, 0.0), axis=1, keepdims=True) + bi
    fcol_all = _log_sigmoid(jnp.sum(jnp.where(lane == jf, g, 0.0), axis=1, keepdims=True) + bf)

    ti = lax.broadcasted_iota(jnp.int32, (L, L), 0)
    si = lax.broadcasted_iota(jnp.int32, (L, L), 1)
    sgn = 1 - 2 * d
    incl = (ti - si) * sgn >= 0
    incl_t = (si - ti) * sgn >= 0
    row128 = lambda x: jnp.broadcast_to(x, (1, M_HD))

    ch = []
    for c in range(nck):
        rows = slice(L * c, L * (c + 1))
        irow = gt_ref[c, pl.ds(ji, 1), :] + bi
        frow = _log_sigmoid(gt_ref[c, pl.ds(jf, 1), :] + bf)
        bsum_col = jnp.sum(jnp.where(incl, frow, 0.0), axis=1, keepdims=True)
        bsum_row = jnp.sum(jnp.where(incl_t, fcol_all[rows], 0.0), axis=0, keepdims=True)
        blast = jnp.sum(frow, axis=1, keepdims=True)
        wlog = blast - bsum_col + icol_all[rows]
        dmat = jnp.where(incl, bsum_col - bsum_row + irow, NEG)
        sc_s[c, 0:1, :] = row128(blast)
        sc_s[c, 1:2, :] = row128(jnp.max(wlog, axis=0, keepdims=True))
        ch.append(dict(rows=rows, bsum_col=bsum_col, blast=blast, wlog=wlog, dmat=dmat,
                       dmax=jnp.max(dmat, axis=1, keepdims=True)))

    def m_step(i, ms):
        ci = jnp.where(d == 0, i, ncs - 1 - i)
        new = []
        for q, m in enumerate(ms):
            cq = q * ncs + ci
            m_new = jnp.maximum(sc_s[cq, 0:1, :] + m, sc_s[cq, 1:2, :])
            sc_s[cq, 2:3, :] = m
            sc_s[cq, 3:4, :] = m_new
            new.append(m_new)
        return tuple(new)

    m_fin = lax.fori_loop(0, ncs, m_step, tuple(m_s[q] for q in range(nseq)))

    for c, cd in enumerate(ch):
        rows = cd['rows']
        cd['q'], cd['k'] = q_ref[rows, :], k_ref[rows, :]
        cd['v'] = v_ref[rows, :]
        cd['m_prev'] = sc_s[c, 2:3, 0:1]
        cd['m_new'] = sc_s[c, 3:4, 0:1]
    if use_rope:
        perm = perm_ref[...]
        rot = [(_dot_lhs2(cd['q'], perm), _dot_lhs2(cd['k'], perm)) for cd in ch]
        for cd, (rq, rk) in zip(ch, rot):
            cos, sin = cos_ref[cd['rows'], :], sin_ref[cd['rows'], :]
            cd['q'] = cd['q'] * cos + rq * sin
            cd['k'] = cd['k'] * cos + rk * sin
    for cd in ch:
        cd['k'] = cd['k'] * (M_HD ** -0.5)
    for cd in ch:
        cd['qk'] = _dot(cd['q'], cd['k'], NT)
    for c, cd in enumerate(ch):
        kw = cd['k'] * jnp.exp(cd['wlog'] - cd['m_new'])
        kv_s[c] = _dot3(kw, cd['v'], TN)
        nk_s[c] = jnp.sum(kw, axis=0, keepdims=True)
        sc_s[c, 4:5, :] = row128(jnp.exp(cd['blast'] + cd['m_prev'] - cd['m_new']))
    for cd in ch:
        inter = cd['bsum_col'] + cd['m_prev']
        cd['mt'] = jnp.maximum(inter, cd['dmax'])
        cd['iw'] = jnp.exp(inter - cd['mt'])
        sm = cd['qk'] * jnp.exp(cd['dmat'] - cd['mt'])
        cd['ssum'] = jnp.sum(sm, axis=1, keepdims=True)
        cd['sv'] = _dot(sm, cd['v'])

    def c_step(i, carry):
        ci = jnp.where(d == 0, i, ncs - 1 - i)
        new = []
        for q, (c_st, n_st) in enumerate(carry):
            cq = q * ncs + ci
            dec = sc_s[cq, 4:5, 0:1]
            cprev_s[cq] = c_st
            nprev_s[cq] = n_st
            new.append((dec * c_st + kv_s[cq], dec * n_st + nk_s[cq]))
        return tuple(new)

    st_fin = lax.fori_loop(0, ncs, c_step, tuple((c_s[q], n_s[q]) for q in range(nseq)))
    for q in range(nseq):
        c_s[q], n_s[q] = st_fin[q]
        m_s[q] = m_fin[q]

    qc_prev = [_dot(cd['q'], cprev_s[c]) for c, cd in enumerate(ch)]
    for c, cd in enumerate(ch):
        num = cd['iw'] * qc_prev[c] + cd['sv']
        den = cd['iw'] * jnp.sum(cd['q'] * nprev_s[c], axis=1, keepdims=True) + cd['ssum']
        h_ref[0, cd['rows'], :] = num / jnp.maximum(jnp.abs(den), jnp.exp(-cd['mt']))

    if has_out:
        @pl.when(tb == n_tb - 1)
        def _():
            for q in range(nseq):
                cout_ref[q, 0, 0, 0] = st_fin[q][0]
                nout_ref[q, 0, 0, 0] = st_fin[q][1]
                mout_ref[q, 0, 0, 0] = m_fin[q]


def _mlstm_call(z, gt, rope, b_i, b_f, init, prev, *, row0, nb, t, tb, layer=None, prev_state=None):
    nseq = max(tb // t, 1)
    n_tb = max(t // tb, 1)
    blk0 = row0 // tb
    cpb = tb // CHUNK
    use_rope = rope is not None
    has_init = init is not None

    def rowblk(b, d, i):
        return blk0 + b * n_tb + jnp.where(d == 0, i, n_tb - 1 - i)

    def zspec(cb):
        return pl.BlockSpec((tb, LANE), lambda b, h, d, i: (rowblk(b, d, i), cb(h)))

    smem = pl.BlockSpec(memory_space=pltpu.SMEM)
    in_specs = [zspec(lambda h: ZB_MQ + h), zspec(lambda h: ZB_MK + h), zspec(lambda h: ZB_MV + h),
                zspec(lambda h: ZB_MIF),
                pl.BlockSpec((cpb, 16, CHUNK), lambda b, h, d, i: (rowblk(b, d, i), 0, 0))]
    args = [z, z, z, z, gt]
    if use_rope:
        tspec = pl.BlockSpec((tb, LANE), lambda b, h, d, i: (jnp.where(d == 0, i, n_tb - 1 - i), 0))
        in_specs += [tspec, tspec, pl.BlockSpec((M_HD, M_HD), lambda b, h, d, i: (0, 0))]
        args += list(rope)
    in_specs += [smem, smem]
    args += [b_i, b_f]
    cspec = pl.BlockSpec((nseq, 1, 1, M_HD, M_HD), lambda b, h, d, i: (b, d, h, 0, 0))
    nspec = pl.BlockSpec((nseq, 1, 1, 1, M_HD), lambda b, h, d, i: (b, d, h, 0, 0))
    if has_init:
        in_specs += [cspec, nspec, smem]
        args += list(init)
    has_out = layer is not None
    aliases = _alias_prev(in_specs, args, prev)
    aliases.update(_alias_prev(in_specs, args, prev_state, first_out=1))
    out_shape = [jax.ShapeDtypeStruct((2, z.shape[0], HW), F32)]
    out_specs = [pl.BlockSpec((1, tb, LANE), lambda b, h, d, i: (d, rowblk(b, d, i), h))]
    if has_out:
        vec_shape = jax.ShapeDtypeStruct((nb, DEPTH, 2, M_HEADS, 1, M_HD), F32)
        out_shape += [jax.ShapeDtypeStruct((nb, DEPTH, 2, M_HEADS, M_HD, M_HD), F32), vec_shape, vec_shape]
        vec_spec = pl.BlockSpec((nseq, 1, 1, 1, 1, M_HD), lambda b, h, d, i: (b, layer, d, h, 0, 0))
        out_specs += [pl.BlockSpec((nseq, 1, 1, 1, M_HD, M_HD), lambda b, h, d, i: (b, layer, d, h, 0, 0)),
                      vec_spec, vec_spec]
    return pl.pallas_call(
        functools.partial(_mlstm_body, tb_rows=tb, nseq=nseq, use_rope=use_rope, has_init=has_init,
                          has_out=has_out, n_alias=len(aliases)),
        out_shape=out_shape,
        grid=(nb // nseq, M_HEADS, 2, n_tb),
        in_specs=in_specs,
        out_specs=out_specs,
        input_output_aliases=aliases,
        scratch_shapes=[pltpu.VMEM((cpb, 8, M_HD), F32),
                        pltpu.VMEM((cpb, M_HD, M_HD), F32), pltpu.VMEM((cpb, 1, M_HD), F32),
                        pltpu.VMEM((cpb, M_HD, M_HD), F32), pltpu.VMEM((cpb, 1, M_HD), F32),
                        pltpu.VMEM((nseq, M_HD, M_HD), F32), pltpu.VMEM((nseq, 1, M_HD), F32),
                        pltpu.VMEM((nseq, 1, M_HD), F32)],
        compiler_params=_params(("parallel", "parallel", "arbitrary", "arbitrary")),
        name="mlstm_chunkwise",
    )(*args)


def _ctx_attn_body(q_ref, k_ref, v_ref, *refs):
    y_ref, ko_ref, vo_ref = refs[-3:]
    q = q_ref[...] * (N_HD ** -0.5)
    k = k_ref[...]
    v = v_ref[...]
    heads = [slice(N_HD * h, N_HD * (h + 1)) for h in range(2)]
    for h, sl in enumerate(heads):
        ko_ref[0, 0, h] = k[:, sl]
        vo_ref[0, 0, h] = v[:, sl]
    s = [_dot(q[:, sl], k[:, sl], NT) for sl in heads]
    p = [jnp.exp(s_ - jnp.max(s_, axis=-1, keepdims=True)) for s_ in s]
    o = [_dot(p_, v[:, sl]) for p_, sl in zip(p, heads)]
    for o_, p_, sl in zip(o, p, heads):
        y_ref[:, sl] = o_ / jnp.sum(p_, axis=-1, keepdims=True)


def _ctx_attn_call(z, prev_kv, *, nb, t, layer):
    zspec = lambda cb: pl.BlockSpec((t, LANE), lambda b, hp: (b, cb + hp))
    kv_spec = pl.BlockSpec((1, 1, 2, t, N_HD), lambda b, hp: (b, layer, hp, 0, 0))
    kv_shape = jax.ShapeDtypeStruct((nb, DEPTH, N_HEADS, t, N_HD), F32)
    in_specs = [zspec(ZB_NQ), zspec(ZB_NK), zspec(ZB_NV)]
    args = [z, z, z]
    aliases = _alias_prev(in_specs, args, prev_kv, first_out=1)
    return pl.pallas_call(
        _ctx_attn_body,
        out_shape=[jax.ShapeDtypeStruct((z.shape[0], HW), F32), kv_shape, kv_shape],
        grid=(nb, N_HEADS // 2),
        in_specs=in_specs,
        out_specs=[pl.BlockSpec((t, LANE), lambda b, hp: (b, hp)), kv_spec, kv_spec],
        input_output_aliases=aliases,
        compiler_params=_params(("parallel", "parallel")),
        name="context_attention",
    )(*args)


def _natten_body(q_ref, k_ref, v_ref, kc_ref, vc_ref, tab_ref, prev_ref, y_ref, *, rows_per_step, rows_n):
    del prev_ref
    rb = pl.program_id(2)
    n_loc = WIN_ROWS * GRID_W

    unroll = 4

    def rows(it, carry):
        work = []
        for u in range(unroll):
            i = it * unroll + u
            r = rb * rows_per_step + i
            rstart = jnp.clip(r - WIN_ROWS // 2, 0, rows_n - WIN_ROWS)
            qrows = pl.ds(pl.multiple_of(i * GRID_W, GRID_W), GRID_W)
            krows = pl.ds(pl.multiple_of(rstart * GRID_W, GRID_W), n_loc)
            work += [(qrows, krows, r - rstart, h, slice(N_HD * h, N_HD * (h + 1))) for h in range(2)]
        q = [q_ref[qr, sl] * (N_HD ** -0.5) for qr, _, _, _, sl in work]
        s_loc = [_dot(q_, k_ref[kr, sl], NT) + tab_ref[h, var] for q_, (_, kr, var, h, sl) in zip(q, work)]
        s_ctx = [_dot(q_, kc_ref[0, h], NT) for q_, (_, _, _, h, _) in zip(q, work)]
        m = [jnp.maximum(jnp.max(a, axis=-1, keepdims=True), jnp.max(c, axis=-1, keepdims=True))
             for a, c in zip(s_loc, s_ctx)]
        p_loc = [jnp.exp(a - m_) for a, m_ in zip(s_loc, m)]
        p_ctx = [jnp.exp(c - m_) for c, m_ in zip(s_ctx, m)]
        o = [_dot(a, v_ref[kr, sl]) + _dot(c, vc_ref[0, h])
             for a, c, (_, kr, _, h, sl) in zip(p_loc, p_ctx, work)]
        for o_, a, c, (qr, _, _, _, sl) in zip(o, p_loc, p_ctx, work):
            y_ref[qr, sl] = o_ / (jnp.sum(a, axis=-1, keepdims=True) + jnp.sum(c, axis=-1, keepdims=True))
        return carry

    lax.fori_loop(0, rows_per_step // unroll, rows, 0)


def _natten_call(z, k_ctx, v_ctx, tab, prev, *, row0, nb, t, rows_per_step=8):
    rows_n = t // GRID_W
    tq = rows_per_step * GRID_W
    qblk0 = row0 // tq
    sblk0 = row0 // t
    n_rb = rows_n // rows_per_step
    past = k_ctx.shape[2]
    kv_spec = lambda cb: pl.BlockSpec((t, LANE), lambda b, hp, rb: (sblk0 + b, cb + hp))
    cache_spec = pl.BlockSpec((1, 2, past, N_HD), lambda b, hp, rb: (b, hp, 0, 0))
    return pl.pallas_call(
        functools.partial(_natten_body, rows_per_step=rows_per_step, rows_n=rows_n),
        out_shape=jax.ShapeDtypeStruct((z.shape[0], HW), F32),
        grid=(nb, N_HEADS // 2, n_rb),
        in_specs=[pl.BlockSpec((tq, LANE), lambda b, hp, rb: (qblk0 + b * n_rb + rb, ZB_NQ + hp)),
                  kv_spec(ZB_NK), kv_spec(ZB_NV), cache_spec, cache_spec,
                  pl.BlockSpec((2, WIN_ROWS, GRID_W, WIN_ROWS * GRID_W), lambda b, hp, rb: (hp, 0, 0, 0)),
                  pl.BlockSpec(memory_space=pl.ANY)],
        out_specs=pl.BlockSpec((tq, LANE), lambda b, hp, rb: (qblk0 + b * n_rb + rb, hp)),
        input_output_aliases={6: 0},
        compiler_params=_params(("parallel", "parallel", "arbitrary")),
        name="neighbourhood_attention",
    )(z, z, z, k_ctx, v_ctx, tab, prev)


def _natten_table(rpb):
    cq = jnp.arange(GRID_W)[:, None]
    ck = jnp.arange(GRID_W)[None, :]
    dc = jnp.clip(ck - cq, -(WIN_COLS - 1), WIN_COLS - 1) + (WIN_COLS - 1)
    cstart = jnp.clip(cq - WIN_COLS // 2, 0, GRID_W - WIN_COLS)
    in_band = (ck >= cstart) & (ck < cstart + WIN_COLS)
    onehot = ((dc[None] == jnp.arange(2 * WIN_COLS - 1)[:, None, None]) & in_band[None]).astype(F32)
    cols = jnp.einsum('hrd,dqk->hrqk', rpb, onehot, precision=lax.Precision.HIGHEST)
    cols = cols + jnp.where(in_band, 0.0, NEG)
    tab = jnp.stack([cols[:, WIN_ROWS - 1 - var:2 * WIN_ROWS - 1 - var] for var in range(WIN_ROWS)], axis=1)
    return tab.transpose(0, 1, 3, 2, 4).reshape(rpb.shape[0], WIN_ROWS, GRID_W, WIN_ROWS * GRID_W)


def _group_norm(y, avg_bf, eps):
    mu = _dot_lhs2(y, avg_bf)
    yc = y - mu
    var = _dot_lhs2(yc * yc, avg_bf)
    return yc * lax.rsqrt(var + eps)


def _merge_body(*refs, with_router):
    refs = list(refs)
    (x_ref, mod_ref, yr_ref, bv_ref, hm_ref, yn_ref, gl_ref, mo_ref, g0_ref, g1_ref, g2_ref,
     avg64_ref, avg128_ref, rg2_ref, lnw_ref, lnb_ref, mlw_ref, wor_ref, wom_ref, won_ref, wout_ref,
     gpost_ref, gpre_ref) = [refs.pop(0) for _ in range(23)]
    wr_ref, br_ref = (refs.pop(0), refs.pop(0)) if with_router else (None, None)
    x1_ref, hin_ref = refs.pop(0), refs.pop(0)
    hin_hi_ref, sel_ref = (refs.pop(0), refs.pop(0)) if with_router else (None, None)
    m = mod_ref[0]
    yr = yr_ref[0] + yr_ref[1]
    yn_r = _group_norm(yr, avg64_ref[...], RWKV_GN_EPS) * lnw_ref[...] + lnb_ref[...]
    g = _dot(jax.nn.sigmoid(gl_ref[...]), rg2_ref[...])
    out_r = (yn_r + bv_ref[0] + bv_ref[1]) * g
    hn = _group_norm(hm_ref[0] + hm_ref[1], avg128_ref[...], NORM_EPS) * mlw_ref[...]
    out_m = jax.nn.sigmoid(mo_ref[...]) * hn
    merged = (jax.nn.sigmoid(g0_ref[...]) * _dot(out_r, wor_ref[...])
              + jax.nn.sigmoid(g1_ref[...]) * _dot(out_m, wom_ref[...])
              + jax.nn.sigmoid(g2_ref[...]) * _dot(yn_ref[...], won_ref[...]))
    o = _dot(merged, wout_ref[...])
    x1 = x_ref[...] + m[2:3] * _rms(o, gpost_ref[...])
    x1_ref[...] = x1
    hin = _rms(x1, gpre_ref[...]) * (1.0 + m[4:5]) + m[3:4]
    if with_router:
        sel_ref[...] = _top2_select(_dot3(hin, wr_ref[...]) + br_ref[...])
        hin_ref[...] = hin[:, :D // 2]
        hin_hi_ref[...] = hin[:, D // 2:]
    else:
        hin_ref[...] = hin.astype(BF16)


def _merge_call(x, mod3, yr, bv, hm, yn, z, W, n_ctx, lat_t, router, tm=256):
    ntok = x.shape[0]
    midx = _mod_index(tm, n_ctx, lat_t)
    with_router = router is not None
    row = lambda w: pl.BlockSpec((tm, w), lambda i: (i, 0))
    dirs = pl.BlockSpec((2, tm, HW), lambda i: (0, i, 0))
    zspec = lambda w, cb: pl.BlockSpec((tm, w), lambda i: (i, cb))
    full = lambda a: pl.BlockSpec(a.shape, lambda i: (0,) * a.ndim)
    consts = [W['avg64'], W['avg128'], W['rg2'], W['lnw'], W['lnb'], W['mlw'], W['wor'], W['wom'], W['won'],
              W['wout'], W['gpost'], W['gpre']]
    in_specs = [row(D), pl.BlockSpec((1, 8, D), lambda i: (midx(i), 0, 0)), dirs, dirs, dirs, row(HW),
                zspec(LANE, ZB_GL), zspec(HW, ZB_MO // 4),
                zspec(D, ZB_GATE // 8), zspec(D, ZB_GATE // 8 + 1), zspec(D, ZB_GATE // 8 + 2)]
    in_specs += [full(a) for a in consts]
    args = [x, mod3, yr, bv, hm, yn, z, z, z, z, z] + consts
    if with_router:
        half = jax.ShapeDtypeStruct((ntok, D // 2), F32)
        out_shape = [jax.ShapeDtypeStruct((ntok, D), F32), half, half]
        out_specs = [row(D), row(D // 2), row(D // 2)]
    else:
        out_shape = [jax.ShapeDtypeStruct((ntok, D), F32), jax.ShapeDtypeStruct((ntok, D), BF16)]
        out_specs = [row(D), row(D)]
    if with_router:
        in_specs += [full(router[0]), full(router[1])]
        args += list(router)
        out_shape.append(jax.ShapeDtypeStruct((ntok, LANE), F32))
        out_specs.append(row(LANE))
    return pl.pallas_call(
        functools.partial(_merge_body, with_router=with_router),
        out_shape=out_shape,
        grid=(ntok // tm,),
        in_specs=in_specs,
        out_specs=out_specs,
        compiler_params=_params(("parallel",)),
        name="branch_merge",
    )(*args)


def _top2_select(lg):
    lane = lax.broadcasted_iota(jnp.int32, lg.shape, 1)
    m1 = jnp.max(lg, axis=1, keepdims=True)
    i1 = jnp.min(jnp.where(lg == m1, lane, LANE), axis=1, keepdims=True)
    lg2 = jnp.where(lane == i1, -jnp.inf, lg)
    m2 = jnp.max(lg2, axis=1, keepdims=True)
    i2 = jnp.min(jnp.where(lg2 == m2, lane, LANE), axis=1, keepdims=True)
    e2 = jnp.exp(m2 - m1)
    den = 1.0 + e2
    return jnp.where(lane == 0, i1.astype(F32),
                     jnp.where(lane == 1, i2.astype(F32),
                               jnp.where(lane == 2, 1.0 / den, jnp.where(lane == 3, e2 / den, 0.0))))


def _ffn_body(h_ref, x1_ref, mod_ref, gpost_ref, wg_ref, wu_ref, wd_ref, o_ref, acc_s):
    j = pl.program_id(1)

    @pl.when(j == 0)
    def _():
        acc_s[...] = jnp.zeros_like(acc_s)

    hin = h_ref[...]
    hg = jnp.dot(hin, wg_ref[...], preferred_element_type=F32)
    hu = jnp.dot(hin, wu_ref[...], preferred_element_type=F32)
    acc_s[...] += _dot(hg * jax.nn.sigmoid(hg) * hu, wd_ref[...])

    @pl.when(j == pl.num_programs(1) - 1)
    def _():
        m = mod_ref[0]
        o_ref[...] = x1_ref[...] + m[5:6] * _rms(acc_s[...], gpost_ref[...])


def _ffn_call(hin, x1, mod3, gpost, wg, wu, wd, n_ctx, lat_t, tm=512, tf=2816):
    ntok = hin.shape[0]
    dff = wg.shape[1]
    midx = _mod_index(tm, n_ctx, lat_t)
    return pl.pallas_call(
        _ffn_body,
        out_shape=jax.ShapeDtypeStruct((ntok, D), F32),
        grid=(ntok // tm, dff // tf),
        in_specs=[pl.BlockSpec((tm, D), lambda i, j: (i, 0)),
                  pl.BlockSpec((tm, D), lambda i, j: (i, 0)),
                  pl.BlockSpec((1, 8, D), lambda i, j: (midx(i), 0, 0)),
                  pl.BlockSpec((1, D), lambda i, j: (0, 0)),
                  pl.BlockSpec((D, tf), lambda i, j: (0, j)),
                  pl.BlockSpec((D, tf), lambda i, j: (0, j)),
                  pl.BlockSpec((tf, D), lambda i, j: (j, 0))],
        out_specs=pl.BlockSpec((tm, D), lambda i, j: (i, 0)),
        scratch_shapes=[pltpu.VMEM((tm, D), F32)],
        compiler_params=_params(("parallel", "arbitrary")),
        name="dense_swiglu",
    )(hin, x1, mod3, gpost, wg, wu, wd)


MOE_TILE = 512
MOE_TF = 1792
CMB_TILE = 128
SLAB_ALIGN = 8
SLAB = CMB_TILE + SLAB_ALIGN


def _moe_plan(sel, ntok):
    i32 = jnp.int32
    e1, e2 = sel[:, 0].astype(i32), sel[:, 1].astype(i32)
    ar = jnp.arange(ntok, dtype=i32)
    ex = jnp.arange(N_EXPERTS, dtype=i32)
    oh = ((e1[:, None] == ex) | (e2[:, None] == ex)).astype(i32)
    csum = jnp.cumsum(oh, axis=0)
    total = csum[-1]
    gsz = (total + MOE_TILE - 1) // MOE_TILE * MOE_TILE
    gend = jnp.cumsum(gsz)
    gstart = gend - gsz
    dstart = jnp.cumsum(total) - total
    key, wts = lax.sort((jnp.concatenate([e1 * ntok + ar, e2 * ntok + ar]),
                         jnp.concatenate([sel[:, 2], sel[:, 3]])), num_keys=1)
    tok_d = key % ntok
    n_pad = 2 * ntok + (N_EXPERTS + 1) * MOE_TILE
    p = jnp.arange(n_pad, dtype=i32)
    ge = jnp.minimum(jnp.sum((p[:, None] >= gend[None, :]).astype(i32), axis=1), N_EXPERTS - 1)
    pick = lambda tab: jnp.sum(jnp.where(ge[:, None] == ex, tab[None, :], 0), axis=1)
    q = p - pick(gstart)
    live = (q < pick(total)) & (p < gend[-1])
    src = jnp.clip(pick(dstart) + q, 0, 2 * ntok - 1)
    tok_p = jnp.where(live, tok_d[src], 0)
    w_p = jnp.where(live, wts[src], 0.0)
    tstart = jnp.arange(n_pad // MOE_TILE, dtype=i32) * MOE_TILE
    texp = jnp.minimum(jnp.sum((tstart[:, None] >= gend[None, :]).astype(i32), axis=1), N_EXPERTS - 1)
    tval = (tstart < gend[-1]).astype(i32)
    r0 = (csum - oh)[::CMB_TILE]
    r1 = jnp.concatenate([r0[1:], total[None]], axis=0)
    alo = (gstart[None, :] + r0).reshape(-1)
    ahi = (gstart[None, :] + r1).reshape(-1)
    return dict(tok=tok_p, w=w_p.reshape(n_pad, 1), texp=texp, tval=tval,
                a8=alo // SLAB_ALIGN, alo=alo, ahi=ahi, n_pad=n_pad)


def _moe_gather_body(tok_ref, src_ref, o_ref, buf_s):
    base = pl.program_id(0) * MOE_TILE

    def gather(r, carry):
        buf_s[pl.ds(r, 1), :] = src_ref[pl.ds(tok_ref[base + r], 1), :]
        return carry

    lax.fori_loop(0, MOE_TILE, gather, 0, unroll=8)
    o_ref[...] = buf_s[...].astype(BF16)


def _moe_gather_call(src, plan):
    n_pad = plan['n_pad']
    width = src.shape[1]
    grid_spec = pltpu.PrefetchScalarGridSpec(
        num_scalar_prefetch=1,
        grid=(n_pad // MOE_TILE,),
        in_specs=[pl.BlockSpec(memory_space=pltpu.VMEM)],
        out_specs=pl.BlockSpec((MOE_TILE, width), lambda i, tk: (i, 0)),
        scratch_shapes=[pltpu.VMEM((MOE_TILE, width), F32)])
    return pl.pallas_call(
        _moe_gather_body,
        out_shape=jax.ShapeDtypeStruct((n_pad, width), BF16),
        grid_spec=grid_spec,
        compiler_params=_params(("arbitrary",)),
        name="moe_row_gather",
    )(plan['tok'], src)


def _moe_group_body(texp_ref, tval_ref, xa_ref, xb_ref, ws_ref, wg_ref, wu_ref, wd_ref, ys_ref, acc_s):
    del texp_ref
    i = pl.program_id(0)
    j = pl.program_id(1)
    half = D // 2

    @pl.when(j == 0)
    def _():
        acc_s[...] = jnp.zeros_like(acc_s)

    @pl.when(tval_ref[i] > 0)
    def _():
        xa, xb = xa_ref[...], xb_ref[...]
        mm = lambda u, ref, rows: jnp.dot(u, ref[0, rows, :], preferred_element_type=F32)
        hg = mm(xa, wg_ref, slice(0, half)) + mm(xb, wg_ref, slice(half, D))
        hu = mm(xa, wu_ref, slice(0, half)) + mm(xb, wu_ref, slice(half, D))
        acc_s[...] += _dot(hg * jax.nn.sigmoid(hg) * hu, wd_ref[0])

    @pl.when(j == pl.num_programs(1) - 1)
    def _():
        ys_ref[...] = acc_s[...] * ws_ref[...]


def _moe_group_call(xa, xb, plan, wg, wu, wd):
    n_pad = plan['n_pad']
    dff = wg.shape[2]
    xspec = pl.BlockSpec((MOE_TILE, D // 2), lambda i, j, te, tv: (i, 0))
    grid_spec = pltpu.PrefetchScalarGridSpec(
        num_scalar_prefetch=2,
        grid=(n_pad // MOE_TILE, dff // MOE_TF),
        in_specs=[xspec, xspec,
                  pl.BlockSpec((MOE_TILE, 1), lambda i, j, te, tv: (i, 0)),
                  pl.BlockSpec((1, D, MOE_TF), lambda i, j, te, tv: (te[i], 0, j)),
                  pl.BlockSpec((1, D, MOE_TF), lambda i, j, te, tv: (te[i], 0, j)),
                  pl.BlockSpec((1, MOE_TF, D), lambda i, j, te, tv: (te[i], j, 0))],
        out_specs=pl.BlockSpec((MOE_TILE, D), lambda i, j, te, tv: (i, 0)),
        scratch_shapes=[pltpu.VMEM((MOE_TILE, D), F32)])
    return pl.pallas_call(
        _moe_group_body,
        out_shape=jax.ShapeDtypeStruct((n_pad, D), F32),
        grid_spec=grid_spec,
        compiler_params=_params(("parallel", "arbitrary")),
        name="moe_expert_swiglu",
    )(plan['texp'], plan['tval'], xa, xb, plan['w'], wg, wu, wd)


def _moe_combine_body(a8_ref, alo_ref, ahi_ref, *refs, tile0):
    ys_refs, tk_refs = refs[:N_EXPERTS], refs[N_EXPERTS:2 * N_EXPERTS]
    x1_ref, mod_ref, gpost_ref, o_ref = refs[2 * N_EXPERTS:]
    t = pl.program_id(0) + tile0
    tloc = t * CMB_TILE + lax.broadcasted_iota(jnp.int32, (SLAB, CMB_TILE), 1)
    srow = lax.broadcasted_iota(jnp.int32, (SLAB, 1), 0)
    f = jnp.zeros((CMB_TILE, D), F32)
    for e in range(N_EXPERTS):
        k = t * N_EXPERTS + e
        row = a8_ref[k] * SLAB_ALIGN + srow
        tok = jnp.where((row >= alo_ref[k]) & (row < ahi_ref[k]), tk_refs[e][...], -1)
        onehot = jnp.where(tok == tloc, 1.0, 0.0).astype(BF16)
        f = f + _dot_rhs2(onehot, ys_refs[e][...], TN)
    m = mod_ref[0]
    o_ref[...] = x1_ref[...] + m[5:6] * _rms(f, gpost_ref[...])


def _moe_combine_call(ys, plan, x1, mod3, gpost, n_ctx, lat_t, row0, nrows):
    midx = _mod_index(CMB_TILE, n_ctx, lat_t)
    tile0 = row0 // CMB_TILE

    def slab(w, e):
        return pl.BlockSpec((pl.Element(SLAB), pl.Element(w)),
                            lambda t, a8, lo, hi: (a8[(t + tile0) * N_EXPERTS + e] * SLAB_ALIGN, 0))

    experts = range(N_EXPERTS)
    grid_spec = pltpu.PrefetchScalarGridSpec(
        num_scalar_prefetch=3,
        grid=(nrows // CMB_TILE,),
        in_specs=[slab(D, e) for e in experts] + [slab(1, e) for e in experts] + [
            pl.BlockSpec((CMB_TILE, D), lambda t, a8, lo, hi: (t + tile0, 0)),
            pl.BlockSpec((1, 8, D), lambda t, a8, lo, hi: (midx(t + tile0), 0, 0)),
            pl.BlockSpec((1, D), lambda t, a8, lo, hi: (0, 0))],
        out_specs=pl.BlockSpec((CMB_TILE, D), lambda t, a8, lo, hi: (t, 0)))
    tok2d = plan['tok'].reshape(-1, 1)
    return pl.pallas_call(
        functools.partial(_moe_combine_body, tile0=tile0),
        out_shape=jax.ShapeDtypeStruct((nrows, D), F32),
        grid_spec=grid_spec,
        compiler_params=_params(("parallel",)),
        name="moe_combine",
    )(plan['a8'], plan['alo'], plan['ahi'], *([ys] * N_EXPERTS), *([tok2d] * N_EXPERTS), x1, mod3, gpost)


def _rope_tables(t):
    i = jnp.arange(M_HD)
    half, pair, f = i // 64, (i % 64) // 32, i % 32
    tt = jnp.arange(t)
    pos = jnp.stack([tt // GRID_W, tt % GRID_W], axis=-1).astype(F32)
    inv = ROPE_BASE ** (-f.astype(F32) / 32)
    ang = pos[:, half] * inv[None, :]
    src = jnp.where(pair == 0, i + 32, i - 32)
    perm = jnp.where(i[:, None] == src[None, :], jnp.where(pair == 0, -1.0, 1.0)[None, :], 0.0)
    return jnp.cos(ang), jnp.sin(ang), perm.astype(BF16)


def _pad_in_proj(w):
    off = {}
    o = 0
    for name, width in (('r', 512), ('k', 512), ('v', 512), ('wl', 128), ('al', 128), ('gl', 128),
                        ('mq', 512), ('mk', 512), ('mv', 512), ('mo', 512), ('mif', 16),
                        ('nq', 512), ('nk', 512), ('nv', 512), ('gate', 3072)):
        off[name] = (o, o + width)
        o += width
    col = lambda n: w[:, off[n][0]:off[n][1]].astype(BF16)
    parts = [col(n) for n in ('r', 'k', 'v', 'mq', 'mk', 'mv', 'mo', 'nq', 'nk', 'nv', 'gate', 'wl', 'al', 'gl', 'mif')]
    parts.append(jnp.zeros((w.shape[0], LANE - 16), BF16))
    return jnp.concatenate(parts, axis=1)


def _block_avg(width, group):
    i = jnp.arange(width) // group
    return jnp.where(i[:, None] == i[None, :], 1.0 / group, 0.0).astype(BF16)


def _zero_pad_rows(w2):
    z = jnp.zeros_like(w2[0])
    return jnp.stack([jnp.concatenate([w2[0], z], axis=0), jnp.concatenate([z, w2[1]], axis=0)])


def kernel(x_prompt, x_sample, state_rwkv, state_mlstm_C, state_mlstm_n, state_mlstm_m, cache_nat_k, cache_nat_v,
           c, c_ctx, w_mod, b_mod, g_pre_mix, g_post_mix, g_pre_ffn, g_post_ffn, w_in,
           rw_w0, rw_w2, rw_a0, rw_a2, rw_g2, rw_k_k, rw_k_a, rw_r_k, rw_ln_w, rw_ln_b,
           ml_b_i, ml_b_f, ml_norm_w, nat_rpb, w_o_rwkv, w_o_mlstm, w_o_nat, w_out,
           ff_w_gate, ff_w_up, ff_w_down, moe_w_router, moe_b_router, moe_w_gate, moe_w_up, moe_w_down):
    cb, ct = x_prompt.shape[:2]
    lb, lt = x_sample.shape[:2]
    n_ctx, n_lat = cb * ct, lb * lt
    x = jnp.concatenate([x_prompt.reshape(n_ctx, D), x_sample.reshape(n_lat, D)], axis=0)
    cvec = jnp.zeros((8, D), F32).at[0].set(c_ctx).at[1:1 + lb].set(c)
    rope = _rope_tables(lt)
    avg64, avg128 = _block_avg(HW, R_HD), _block_avg(HW, M_HD)
    s_rw = ml_st = kv_c = None
    for l in range(DEPTH):
        mod = _mod_call(cvec, w_mod[l], b_mod[l])
        mod3 = jnp.pad(mod[:1 + lb].reshape(1 + lb, 6, D), ((0, 0), (0, 2), (0, 0)))
        z = _proj_call(x, mod3, g_pre_mix[l], _pad_in_proj(w_in[l]), n_ctx, lt)

        rp = dict(w0=rw_w0[l].reshape(2, 1, HW), w2=_zero_pad_rows(rw_w2[l]),
                  a0=rw_a0[l].reshape(2, 1, HW), a2=_zero_pad_rows(rw_a2[l]),
                  kk=rw_k_k[l].reshape(1, HW), ka=rw_k_a[l].reshape(1, HW), rk=rw_r_k[l].reshape(1, HW))
        yr, bv, s_rw = _rwkv_call(z, rp, None, None, row0=0, nb=cb, t=ct, tb=RW_BLOCK, layer=l,
                                  prev_state=None if s_rw is None else (s_rw,))
        yr, bv = _rwkv_call(z, rp, state_rwkv[:, l], (yr, bv), row0=n_ctx, nb=lb, t=lt, tb=RW_BLOCK)

        gt = z[:, ZB_MIF * LANE:ZB_MIF * LANE + 16].reshape(-1, CHUNK, 16).transpose(0, 2, 1)
        hm, *ml_st = _mlstm_call(z, gt, None, ml_b_i[l], ml_b_f[l], None, None,
                                 row0=0, nb=cb, t=ct, tb=ML_BLOCK, layer=l, prev_state=ml_st)
        init = (state_mlstm_C[:, l], state_mlstm_n[:, l][:, :, :, None, :], state_mlstm_m[:, l])
        hm, = _mlstm_call(z, gt, rope, ml_b_i[l], ml_b_f[l], init, (hm,),
                          row0=n_ctx, nb=lb, t=lt, tb=ML_BLOCK)

        yn, *kv_c = _ctx_attn_call(z, kv_c, nb=cb, t=ct, layer=l)
        yn = _natten_call(z, cache_nat_k[:, l], cache_nat_v[:, l], _natten_table(nat_rpb[l]), yn,
                          row0=n_ctx, nb=lb, t=lt)

        mw = dict(avg64=avg64, avg128=avg128, rg2=rw_g2[l].astype(BF16), lnw=rw_ln_w[l].reshape(1, HW),
                  lnb=rw_ln_b[l].reshape(1, HW), mlw=ml_norm_w[l].reshape(1, HW),
                  wor=w_o_rwkv[l].astype(BF16), wom=w_o_mlstm[l].astype(BF16), won=w_o_nat[l].astype(BF16),
                  wout=w_out[l].astype(BF16), gpost=g_post_mix[l].reshape(1, D), gpre=g_pre_ffn[l].reshape(1, D))
        j = l // 2
        if l % 2 == 0:
            router = None
        else:
            wr = jnp.pad(moe_w_router[j], ((0, 0), (0, LANE - N_EXPERTS)))
            br = jnp.pad(moe_b_router[j], (0, LANE - N_EXPERTS), constant_values=NEG).reshape(1, LANE)
            router = (wr, br)
        merged = _merge_call(x, mod3, yr, bv, hm, yn, z, mw, n_ctx, lt, router)
        gpost = g_post_ffn[l].reshape(1, D)
        if l % 2 == 0:
            x = _ffn_call(merged[1], merged[0], mod3, gpost, ff_w_gate[j].astype(BF16),
                          ff_w_up[j].astype(BF16), ff_w_down[j].astype(BF16), n_ctx, lt)
        else:
            plan = _moe_plan(merged[3], n_ctx + n_lat)
            xa, xb = _moe_gather_call(merged[1], plan), _moe_gather_call(merged[2], plan)
            ys = _moe_group_call(xa, xb, plan, moe_w_gate[j].astype(BF16), moe_w_up[j].astype(BF16),
                                 moe_w_down[j].astype(BF16))
            groups = [_moe_combine_call(ys, plan, merged[0], mod3, gpost, n_ctx, lt, r0, nr)
                      for r0, nr in ((0, n_ctx), (n_ctx, n_lat))]
            x = jnp.concatenate(groups, axis=0) if l + 1 < DEPTH else None
        if x is not None:
            groups = [x[:n_ctx], x[n_ctx:]]

    c_m, n_m, m_m = ml_st
    return (groups[0].reshape(cb, ct, D), groups[1].reshape(lb, lt, D),
            s_rw, c_m, n_m[:, :, :, :, 0, :], m_m[:, :, :, :, 0, 0], kv_c[0], kv_c[1])
```

```python
import functools
import math

import jax
import jax.numpy as jnp
from jax import lax
from jax.experimental import pallas as pl
from jax.experimental.pallas import tpu as pltpu

F32 = jnp.float32
BF16 = jnp.bfloat16

D = 1024
DEPTH = 2
GRID_W = 64
R_HEADS, R_HD = 8, 64
M_HEADS, M_HD = 4, 128
N_HEADS, N_HD = 8, 64
HW = 512
WIN_ROWS, WIN_COLS = 8, 16
N_EXPERTS = 8
RWKV_GN_EPS = 64e-5
NORM_EPS = 1e-6
ROPE_BASE = 10000.0
CHUNK = 64
NEG = -1e30
LANE = 128
VMEM_LIMIT = 56 * 1024 * 1024

ZB_R, ZB_K, ZB_V = 0, 4, 8
ZB_MQ, ZB_MK, ZB_MV, ZB_MO = 12, 16, 20, 24
ZB_NQ, ZB_NK, ZB_NV = 28, 32, 36
ZB_GATE = 40
ZB_WL, ZB_AL, ZB_GL, ZB_MIF = 64, 65, 66, 67
P_PAD = 68 * LANE

NN = (((1,), (0,)), ((), ()))
NT = (((1,), (1,)), ((), ()))
TN = (((0,), (0,)), ((), ()))


def _dot(a, b, dims=NN):
    return lax.dot_general(a.astype(BF16), b.astype(BF16), dims, preferred_element_type=F32)


def _split(x):
    hi = x.astype(BF16)
    return hi, (x - hi.astype(F32)).astype(BF16)


def _dot3(a, b, dims=NN):
    ah, al = _split(a)
    bh, bl = _split(b)
    d = lambda u, w: lax.dot_general(u, w, dims, preferred_element_type=F32)
    return d(ah, bh) + (d(ah, bl) + d(al, bh))


def _dot_lhs2(a, b_exact, dims=NN):
    ah, al = _split(a)
    d = lambda u: lax.dot_general(u, b_exact, dims, preferred_element_type=F32)
    return d(ah) + d(al)


def _dot_rhs2(a_exact, b, dims=NN):
    bh, bl = _split(b)
    d = lambda w: lax.dot_general(a_exact, w, dims, preferred_element_type=F32)
    return d(bh) + d(bl)


def _log_sigmoid(x):
    return jnp.minimum(x, 0.0) - jnp.log1p(jnp.exp(-jnp.abs(x)))


def _rms(x, g):
    return x * lax.rsqrt(jnp.mean(x * x, axis=-1, keepdims=True) + NORM_EPS) * g


def _params(sem):
    return pltpu.CompilerParams(dimension_semantics=sem, vmem_limit_bytes=VMEM_LIMIT)


def _alias_prev(in_specs, args, prev, first_out=0):
    aliases = {}
    for k, a in enumerate(prev or ()):
        aliases[len(args)] = first_out + k
        in_specs.append(pl.BlockSpec(memory_space=pl.ANY))
        args.append(a)
    return aliases


def _mod_body(c_ref, w_ref, b_ref, o_ref):
    c = c_ref[...]
    o_ref[...] = _dot(c * jax.nn.sigmoid(c), w_ref[...]) + b_ref[...]


def _mod_call(cvec, w_mod, b_mod):
    tn = 1536
    return pl.pallas_call(
        _mod_body,
        out_shape=jax.ShapeDtypeStruct((8, 6 * D), F32),
        grid=(6 * D // tn,),
        in_specs=[pl.BlockSpec((8, D), lambda j: (0, 0)),
                  pl.BlockSpec((D, tn), lambda j: (0, j)),
                  pl.BlockSpec((1, tn), lambda j: (0, j))],
        out_specs=pl.BlockSpec((8, tn), lambda j: (0, j)),
        compiler_params=_params(("arbitrary",)),
        name="adaln_mod",
    )(cvec, w_mod, b_mod.reshape(1, 6 * D))


def _mod_index(tile_rows, n_ctx, lat_t):
    def f(i):
        start = i * tile_rows
        return jnp.where(start < n_ctx, 0, 1 + (start - n_ctx) // lat_t)
    return f


def _proj_body(x_ref, mod_ref, g_ref, w_ref, z_ref, hin_s):
    @pl.when(pl.program_id(1) == 0)
    def _():
        m = mod_ref[0]
        hin_s[...] = (_rms(x_ref[...], g_ref[...]) * (1.0 + m[1:2]) + m[0:1]).astype(BF16)
    z_ref[...] = jnp.dot(hin_s[...], w_ref[...], preferred_element_type=F32)


def _proj_call(x, mod3, g, w_bf, n_ctx, lat_t, tm=1024, tn=2176):
    ntok = x.shape[0]
    midx = _mod_index(tm, n_ctx, lat_t)
    return pl.pallas_call(
        _proj_body,
        out_shape=jax.ShapeDtypeStruct((ntok, P_PAD), F32),
        grid=(ntok // tm, P_PAD // tn),
        in_specs=[pl.BlockSpec((tm, D), lambda i, j: (i, 0)),
                  pl.BlockSpec((1, 8, D), lambda i, j: (midx(i), 0, 0)),
                  pl.BlockSpec((1, D), lambda i, j: (0, 0)),
                  pl.BlockSpec((D, tn), lambda i, j: (0, j))],
        out_specs=pl.BlockSpec((tm, tn), lambda i, j: (i, j)),
        scratch_shapes=[pltpu.VMEM((tm, D), BF16)],
        compiler_params=_params(("parallel", "arbitrary")),
        name="in_proj",
    )(x, mod3, g.reshape(1, D), w_bf)


RW_TILE = 2 * CHUNK
RW_BLOCK = 1024
ML_BLOCK = 512


def _rwkv_body(*refs, tb_rows, nseq, has_init, has_out, n_alias):
    (r_ref, k_ref, v_ref, wl_ref, al_ref, w0_ref, w2_ref, a0_ref, a2_ref,
     kkw_ref, kaw_ref, rkw_ref, hsum_ref) = refs[:13]
    rest = list(refs[13:])
    s0_ref = rest.pop(0) if has_init else None
    del rest[:n_alias]
    y_ref, bv_ref = rest.pop(0), rest.pop(0)
    sout_ref = rest.pop(0) if has_out else None
    r2_s, m_s, g_s, st_s = rest
    C = CHUNK
    TL = RW_TILE
    nchunk = tb_rows // nseq // C
    d = pl.program_id(2)
    tb = pl.program_id(3)
    n_tb = pl.num_programs(3)

    ti = lax.broadcasted_iota(jnp.int32, (TL, TL), 0)
    si = lax.broadcasted_iota(jnp.int32, (TL, TL), 1)
    head_blk = (ti >> 6) == (si >> 6)

    @pl.when(tb == 0)
    def _():
        for q in range(nseq):
            if has_init:
                s0 = s0_ref[q, 0]
                two = jnp.concatenate([jnp.concatenate([s0[0], s0[0]], axis=1),
                                       jnp.concatenate([s0[1], s0[1]], axis=1)], axis=0)
                st_s[q] = jnp.where(head_blk, two, 0.0)
            else:
                st_s[q] = jnp.zeros((LANE, LANE), F32)

    hsum = hsum_ref[...]
    r = r_ref[...]
    kf = k_ref[...]
    v = v_ref[...]
    w_pre = w0_ref[0] + _dot(jnp.tanh(wl_ref[...]), w2_ref[0])
    lw = -math.exp(-0.5) * jax.nn.sigmoid(w_pre)
    a = jax.nn.sigmoid(a0_ref[0] + _dot(al_ref[...], a2_ref[0]))
    kd = kf * (1.0 + (a - 1.0) * kaw_ref[...])
    kkf = kf * kkw_ref[...]
    kn = kkf * lax.rsqrt(_dot_lhs2(kkf * kkf, hsum) + 1e-12)
    bd = kn * a
    bv_ref[0] = _dot_lhs2(r * kd * rkw_ref[...], hsum) * v

    dlt = (ti - si) * (1 - 2 * d)
    strict = head_blk & (dlt > 0)
    incl = head_blk & (dlt >= 0)
    cum_ones = jnp.concatenate([jnp.where(incl, 1.0, 0.0), jnp.where(head_blk, 1.0, 0.0)], axis=0).astype(BF16)
    is_diag = ti == si
    eye = jnp.where(is_diag, 1.0, 0.0)
    pair = (ti >> 1) == (si >> 1)
    sibling = [((ti >> lg) ^ (si >> lg)) == 1 for lg in range(1, 6)]
    head0 = si < R_HD

    n_tiles = tb_rows // TL
    tiles = []
    for tile in range(n_tiles):
        rows = slice(TL * tile, TL * (tile + 1))
        cs = _dot_rhs2(cum_ones, lw[rows])
        cum, tot = cs[:TL], cs[TL:]
        e_neg = jnp.exp(-cum)
        e_hat = jnp.exp(tot - cum)
        rt = r[rows] * jnp.exp(cum)
        at = -kn[rows] * jnp.exp(cum - lw[rows])
        tiles.append(dict(rows=rows, tot=tot, rt=rt, at=at, bh=bd[rows] * e_hat, kh=kd[rows] * e_hat,
                          rhs=jnp.concatenate([bd[rows] * e_neg, kd[rows] * e_neg], axis=0).astype(BF16),
                          at_bf=at.astype(BF16), v_bf=v[rows].astype(BF16)))
    chains = [(tl, h) for tl in tiles for h in range(2)]
    a_all = []
    for tl, h in chains:
        hm = head0 if h == 0 else ~head0
        lhs = jnp.concatenate([jnp.where(hm, tl['at'], 0.0), jnp.where(hm, tl['rt'], 0.0)], axis=0)
        a_all.append(_dot(lhs, tl['rhs'], NT))
    n_mat = [jnp.where(strict, a[:TL, :TL], 0.0) for a in a_all]
    a_ak = [jnp.where(strict, a[:TL, TL:], 0.0).astype(BF16) for a in a_all]
    a_rb = [jnp.where(incl, a[TL:, :TL], 0.0).astype(BF16) for a in a_all]
    a_rk = [jnp.where(incl, a[TL:, TL:], 0.0).astype(BF16) for a in a_all]
    t_inv = [eye + jnp.where(pair, n, 0.0) for n in n_mat]
    for sib in sibling:
        x_mat = [_dot(jnp.where(sib, n, 0.0), t) for n, t in zip(n_mat, t_inv)]
        t_inv = [t + _dot(t, x) for t, x in zip(t_inv, x_mat)]
    p_h = [_dot(t, tl['at_bf']) for t, (tl, _) in zip(t_inv, chains)]
    av = [_dot(a, tl['v_bf']) for a, (tl, _) in zip(a_ak, chains)]
    q_h = [_dot(t, x) for t, x in zip(t_inv, av)]
    r2_h = [_dot(a, p) for a, p in zip(a_rb, p_h)]
    y0_h = [_dot(a, q) + _dot(k_, tl['v_bf']) for a, q, k_, (tl, _) in zip(a_rb, q_h, a_rk, chains)]
    p_m, q_m = [], []
    for i, tl in enumerate(tiles):
        rows = tl['rows']
        pick = lambda u: jnp.where(head0, u[2 * i], u[2 * i + 1])
        p_m.append(pick(p_h))
        q_m.append(pick(q_h))
        r2_s[rows, :] = tl['rt'] + pick(r2_h)
        y_ref[0, rows, :] = pick(y0_h)
        for c in range(TL // C):
            cr = slice(C * c, C * (c + 1))
            decay = jnp.where(is_diag, jnp.exp(tl['tot'][C * c:C * c + 1]), 0.0)
            m_s[i * (TL // C) + c] = decay + jnp.where(head_blk, _dot(p_m[i][cr], tl['bh'][cr], TN), 0.0)
            g_s[i * (TL // C) + c] = jnp.where(
                head_blk, _dot(q_m[i][cr], tl['bh'][cr], TN) + _dot(v[rows][cr], tl['kh'][cr], TN), 0.0)

    def step(i, states):
        ci = jnp.where(d == 0, i, nchunk - 1 - i)
        new = []
        for q, s in enumerate(states):
            cq = q * nchunk + ci
            rows = pl.ds(pl.multiple_of(cq * C, C), C)
            y_ref[0, rows, :] = y_ref[0, rows, :] + _dot(r2_s[rows, :], s, NT)
            new.append(_dot3(s, m_s[cq]) + g_s[cq])
        return tuple(new)

    s_fin = lax.fori_loop(0, nchunk, step, tuple(st_s[q] for q in range(nseq)))
    for q in range(nseq):
        st_s[q] = s_fin[q]

    if has_out:
        @pl.when(tb == n_tb - 1)
        def _():
            for q in range(nseq):
                sout_ref[q, 0, 0, 0] = s_fin[q][:R_HD, :R_HD]
                sout_ref[q, 0, 0, 1] = s_fin[q][R_HD:, R_HD:]


def _rwkv_call(z, P, s0, prev, *, row0, nb, t, tb, layer=None, prev_state=None):
    nseq = max(tb // t, 1)
    n_tb = max(t // tb, 1)
    blk0 = row0 // tb
    has_init = s0 is not None

    def rowblk(b, d, i):
        return blk0 + b * n_tb + jnp.where(d == 0, i, n_tb - 1 - i)

    def zspec(cb):
        return pl.BlockSpec((tb, LANE), lambda b, hp, d, i: (rowblk(b, d, i), cb(hp)))

    vec = lambda: pl.BlockSpec((1, LANE), lambda b, hp, d, i: (0, hp))
    dvec = lambda: pl.BlockSpec((1, 1, LANE), lambda b, hp, d, i: (d, 0, hp))
    dmat = lambda: pl.BlockSpec((1, LANE, LANE), lambda b, hp, d, i: (d, 0, hp))
    st_spec = pl.BlockSpec((nseq, 1, 2, R_HD, R_HD), lambda b, hp, d, i: (b, d, hp, 0, 0))
    in_specs = [zspec(lambda hp: ZB_R + hp), zspec(lambda hp: ZB_K + hp), zspec(lambda hp: ZB_V + hp),
                zspec(lambda hp: ZB_WL), zspec(lambda hp: ZB_AL),
                dvec(), dmat(), dvec(), dmat(), vec(), vec(), vec(),
                pl.BlockSpec((LANE, LANE), lambda b, hp, d, i: (0, 0))]
    args = [z, z, z, z, z, P['w0'], P['w2'], P['a0'], P['a2'], P['kk'], P['ka'], P['rk'],
            _block_avg(LANE, R_HD) * R_HD]
    if has_init:
        in_specs.append(st_spec)
        args.append(s0)
    has_out = layer is not None
    aliases = _alias_prev(in_specs, args, prev)
    aliases.update(_alias_prev(in_specs, args, prev_state, first_out=2))
    yspec = pl.BlockSpec((1, tb, LANE), lambda b, hp, d, i: (d, rowblk(b, d, i), hp))
    out_shape = [jax.ShapeDtypeStruct((2, z.shape[0], HW), F32)] * 2
    out_specs = [yspec, yspec]
    if has_out:
        out_shape.append(jax.ShapeDtypeStruct((nb, DEPTH, 2, R_HEADS, R_HD, R_HD), F32))
        out_specs.append(pl.BlockSpec((nseq, 1, 1, 2, R_HD, R_HD), lambda b, hp, d, i: (b, layer, d, hp, 0, 0)))
    return pl.pallas_call(
        functools.partial(_rwkv_body, tb_rows=tb, nseq=nseq, has_init=has_init, has_out=has_out,
                          n_alias=len(aliases)),
        out_shape=out_shape,
        grid=(nb // nseq, R_HEADS // 2, 2, n_tb),
        in_specs=in_specs,
        out_specs=out_specs,
        input_output_aliases=aliases,
        scratch_shapes=[pltpu.VMEM((tb, LANE), F32), pltpu.VMEM((tb // CHUNK, LANE, LANE), F32),
                        pltpu.VMEM((tb // CHUNK, LANE, LANE), F32), pltpu.VMEM((nseq, LANE, LANE), F32)],
        compiler_params=_params(("parallel", "parallel", "arbitrary", "arbitrary")),
        name="rwkv7_chunked",
    )(*args)


def _mlstm_body(*refs, tb_rows, nseq, use_rope, has_init, has_out, n_alias):
    refs = list(refs)
    q_ref, k_ref, v_ref, g_ref, gt_ref = [refs.pop(0) for _ in range(5)]
    cos_ref, sin_ref, perm_ref = [refs.pop(0) for _ in range(3)] if use_rope else (None,) * 3
    bi_ref, bf_ref = refs.pop(0), refs.pop(0)
    c0_ref, n0_ref, m0_ref = [refs.pop(0) for _ in range(3)] if has_init else (None,) * 3
    del refs[:n_alias]
    h_ref = refs.pop(0)
    cout_ref, nout_ref, mout_ref = [refs.pop(0) for _ in range(3)] if has_out else (None,) * 3
    sc_s, kv_s, nk_s, cprev_s, nprev_s, c_s, n_s, m_s = refs
    L = CHUNK
    nck = tb_rows // L
    ncs = nck // nseq
    b = pl.program_id(0)
    h = pl.program_id(1)
    d = pl.program_id(2)
    tb = pl.program_id(3)
    n_tb = pl.num_programs(3)
    bi = bi_ref[d, h]
    bf = bf_ref[d, h]
    ji = d * (2 * M_HEADS) + h
    jf = ji + M_HEADS

    @pl.when(tb == 0)
    def _():
        for q in range(nseq):
            if has_init:
                c_s[q] = c0_ref[q, 0, 0]
                n_s[q] = n0_ref[q, 0, 0]
                m_s[q] = jnp.full((1, M_HD), m0_ref[b * nseq + q, d, h], F32)
            else:
                c_s[q] = jnp.zeros((M_HD, M_HD), F32)
                n_s[q] = jnp.zeros((1, M_HD), F32)
                m_s[q] = jnp.zeros((1, M_HD), F32)

    g = g_ref[...]
    lane = lax.broadcasted_iota(jnp.int32, g.shape, 1)
    icol_all = jnp.sum(jnp.where(lane == ji, g, 0.0), axis=1, keepdims=True) + bi
    fcol_all = _log_sigmoid(jnp.sum(jnp.where(lane == jf, g, 0.0), axis=1, keepdims=True) + bf)

    ti = lax.broadcasted_iota(jnp.int32, (L, L), 0)
    si = lax.broadcasted_iota(jnp.int32, (L, L), 1)
    sgn = 1 - 2 * d
    incl = (ti - si) * sgn >= 0
    incl_t = (si - ti) * sgn >= 0
    row128 = lambda x: jnp.broadcast_to(x, (1, M_HD))

    ch = []
    for c in range(nck):
        rows = slice(L * c, L * (c + 1))
        irow = gt_ref[c, pl.ds(ji, 1), :] + bi
        frow = _log_sigmoid(gt_ref[c, pl.ds(jf, 1), :] + bf)
        bsum_col = jnp.sum(jnp.where(incl, frow, 0.0), axis=1, keepdims=True)
        bsum_row = jnp.sum(jnp.where(incl_t, fcol_all[rows], 0.0), axis=0, keepdims=True)
        blast = jnp.sum(frow, axis=1, keepdims=True)
        wlog = blast - bsum_col + icol_all[rows]
        dmat = jnp.where(incl, bsum_col - bsum_row + irow, NEG)
        sc_s[c, 0:1, :] = row128(blast)
        sc_s[c, 1:2, :] = row128(jnp.max(wlog, axis=0, keepdims=True))
        ch.append(dict(rows=rows, bsum_col=bsum_col, blast=blast, wlog=wlog, dmat=dmat,
                       dmax=jnp.max(dmat, axis=1, keepdims=True)))

    def m_step(i, ms):
        ci = jnp.where(d == 0, i, ncs - 1 - i)
        new = []
        for q, m in enumerate(ms):
            cq = q * ncs + ci
            m_new = jnp.maximum(sc_s[cq, 0:1, :] + m, sc_s[cq, 1:2, :])
            sc_s[cq, 2:3, :] = m
            sc_s[cq, 3:4, :] = m_new
            new.append(m_new)
        return tuple(new)

    m_fin = lax.fori_loop(0, ncs, m_step, tuple(m_s[q] for q in range(nseq)))

    for c, cd in enumerate(ch):
        rows = cd['rows']
        cd['q'], cd['k'] = q_ref[rows, :], k_ref[rows, :]
        cd['v'] = v_ref[rows, :]
        cd['m_prev'] = sc_s[c, 2:3, 0:1]
        cd['m_new'] = sc_s[c, 3:4, 0:1]
    if use_rope:
        perm = perm_ref[...]
        rot = [(_dot_lhs2(cd['q'], perm), _dot_lhs2(cd['k'], perm)) for cd in ch]
        for cd, (rq, rk) in zip(ch, rot):
            cos, sin = cos_ref[cd['rows'], :], sin_ref[cd['rows'], :]
            cd['q'] = cd['q'] * cos + rq * sin
            cd['k'] = cd['k'] * cos + rk * sin
    for cd in ch:
        cd['k'] = cd['k'] * (M_HD ** -0.5)
    for cd in ch:
        cd['qk'] = _dot(cd['q'], cd['k'], NT)
    for c, cd in enumerate(ch):
        kw = cd['k'] * jnp.exp(cd['wlog'] - cd['m_new'])
        kv_s[c] = _dot3(kw, cd['v'], TN)
        nk_s[c] = jnp.sum(kw, axis=0, keepdims=True)
        sc_s[c, 4:5, :] = row128(jnp.exp(cd['blast'] + cd['m_prev'] - cd['m_new']))
    for cd in ch:
        inter = cd['bsum_col'] + cd['m_prev']
        cd['mt'] = jnp.maximum(inter, cd['dmax'])
        cd['iw'] = jnp.exp(inter - cd['mt'])
        sm = cd['qk'] * jnp.exp(cd['dmat'] - cd['mt'])
        cd['ssum'] = jnp.sum(sm, axis=1, keepdims=True)
        cd['sv'] = _dot(sm, cd['v'])

    def c_step(i, carry):
        ci = jnp.where(d == 0, i, ncs - 1 - i)
        new = []
        for q, (c_st, n_st) in enumerate(carry):
            cq = q * ncs + ci
            dec = sc_s[cq, 4:5, 0:1]
            cprev_s[cq] = c_st
            nprev_s[cq] = n_st
            new.append((dec * c_st + kv_s[cq], dec * n_st + nk_s[cq]))
        return tuple(new)

    st_fin = lax.fori_loop(0, ncs, c_step, tuple((c_s[q], n_s[q]) for q in range(nseq)))
    for q in range(nseq):
        c_s[q], n_s[q] = st_fin[q]
        m_s[q] = m_fin[q]

    qc_prev = [_dot(cd['q'], cprev_s[c]) for c, cd in enumerate(ch)]
    for c, cd in enumerate(ch):
        num = cd['iw'] * qc_prev[c] + cd['sv']
        den = cd['iw'] * jnp.sum(cd['q'] * nprev_s[c], axis=1, keepdims=True) + cd['ssum']
        h_ref[0, cd['rows'], :] = num / jnp.maximum(jnp.abs(den), jnp.exp(-cd['mt']))

    if has_out:
        @pl.when(tb == n_tb - 1)
        def _():
            for q in range(nseq):
                cout_ref[q, 0, 0, 0] = st_fin[q][0]
                nout_ref[q, 0, 0, 0] = st_fin[q][1]
                mout_ref[q, 0, 0, 0] = m_fin[q]


def _mlstm_call(z, gt, rope, b_i, b_f, init, prev, *, row0, nb, t, tb, layer=None, prev_state=None):
    nseq = max(tb // t, 1)
    n_tb = max(t // tb, 1)
    blk0 = row0 // tb
    cpb = tb // CHUNK
    use_rope = rope is not None
    has_init = init is not None

    def rowblk(b, d, i):
        return blk0 + b * n_tb + jnp.where(d == 0, i, n_tb - 1 - i)

    def zspec(cb):
        return pl.BlockSpec((tb, LANE), lambda b, h, d, i: (rowblk(b, d, i), cb(h)))

    smem = pl.BlockSpec(memory_space=pltpu.SMEM)
    in_specs = [zspec(lambda h: ZB_MQ + h), zspec(lambda h: ZB_MK + h), zspec(lambda h: ZB_MV + h),
                zspec(lambda h: ZB_MIF),
                pl.BlockSpec((cpb, 16, CHUNK), lambda b, h, d, i: (rowblk(b, d, i), 0, 0))]
    args = [z, z, z, z, gt]
    if use_rope:
        tspec = pl.BlockSpec((tb, LANE), lambda b, h, d, i: (jnp.where(d == 0, i, n_tb - 1 - i), 0))
        in_specs += [tspec, tspec, pl.BlockSpec((M_HD, M_HD), lambda b, h, d, i: (0, 0))]
        args += list(rope)
    in_specs += [smem, smem]
    args += [b_i, b_f]
    cspec = pl.BlockSpec((nseq, 1, 1, M_HD, M_HD), lambda b, h, d, i: (b, d, h, 0, 0))
    nspec = pl.BlockSpec((nseq, 1, 1, 1, M_HD), lambda b, h, d, i: (b, d, h, 0, 0))
    if has_init:
        in_specs += [cspec, nspec, smem]
        args += list(init)
    has_out = layer is not None
    aliases = _alias_prev(in_specs, args, prev)
    aliases.update(_alias_prev(in_specs, args, prev_state, first_out=1))
    out_shape = [jax.ShapeDtypeStruct((2, z.shape[0], HW), F32)]
    out_specs = [pl.BlockSpec((1, tb, LANE), lambda b, h, d, i: (d, rowblk(b, d, i), h))]
    if has_out:
        vec_shape = jax.ShapeDtypeStruct((nb, DEPTH, 2, M_HEADS, 1, M_HD), F32)
        out_shape += [jax.ShapeDtypeStruct((nb, DEPTH, 2, M_HEADS, M_HD, M_HD), F32), vec_shape, vec_shape]
        vec_spec = pl.BlockSpec((nseq, 1, 1, 1, 1, M_HD), lambda b, h, d, i: (b, layer, d, h, 0, 0))
        out_specs += [pl.BlockSpec((nseq, 1, 1, 1, M_HD, M_HD), lambda b, h, d, i: (b, layer, d, h, 0, 0)),
                      vec_spec, vec_spec]
    return pl.pallas_call(
        functools.partial(_mlstm_body, tb_rows=tb, nseq=nseq, use_rope=use_rope, has_init=has_init,
                          has_out=has_out, n_alias=len(aliases)),
        out_shape=out_shape,
        grid=(nb // nseq, M_HEADS, 2, n_tb),
        in_specs=in_specs,
        out_specs=out_specs,
        input_output_aliases=aliases,
        scratch_shapes=[pltpu.VMEM((cpb, 8, M_HD), F32),
                        pltpu.VMEM((cpb, M_HD, M_HD), F32), pltpu.VMEM((cpb, 1, M_HD), F32),
                        pltpu.VMEM((cpb, M_HD, M_HD), F32), pltpu.VMEM((cpb, 1, M_HD), F32),
                        pltpu.VMEM((nseq, M_HD, M_HD), F32), pltpu.VMEM((nseq, 1, M_HD), F32),
                        pltpu.VMEM((nseq, 1, M_HD), F32)],
        compiler_params=_params(("parallel", "parallel", "arbitrary", "arbitrary")),
        name="mlstm_chunkwise",
    )(*args)


def _ctx_attn_body(q_ref, k_ref, v_ref, *refs):
    y_ref, ko_ref, vo_ref = refs[-3:]
    q = q_ref[...] * (N_HD ** -0.5)
    k = k_ref[...]
    v = v_ref[...]
    heads = [slice(N_HD * h, N_HD * (h + 1)) for h in range(N_HEADS)]
    for h, sl in enumerate(heads):
        ko_ref[0, 0, h] = k[:, sl]
        vo_ref[0, 0, h] = v[:, sl]
    s = [_dot(q[:, sl], k[:, sl], NT) for sl in heads]
    p = [jnp.exp(s_ - jnp.max(s_, axis=-1, keepdims=True)) for s_ in s]
    o = [_dot(p_, v[:, sl]) for p_, sl in zip(p, heads)]
    for o_, p_, sl in zip(o, p, heads):
        y_ref[:, sl] = o_ / jnp.sum(p_, axis=-1, keepdims=True)


def _ctx_attn_call(z, prev_kv, *, nb, t, layer):
    zspec = lambda cb: pl.BlockSpec((t, HW), lambda b: (b, cb // (HW // LANE)))
    kv_spec = pl.BlockSpec((1, 1, N_HEADS, t, N_HD), lambda b: (b, layer, 0, 0, 0))
    kv_shape = jax.ShapeDtypeStruct((nb, DEPTH, N_HEADS, t, N_HD), F32)
    in_specs = [zspec(ZB_NQ), zspec(ZB_NK), zspec(ZB_NV)]
    args = [z, z, z]
    aliases = _alias_prev(in_specs, args, prev_kv, first_out=1)
    return pl.pallas_call(
        _ctx_attn_body,
        out_shape=[jax.ShapeDtypeStruct((z.shape[0], HW), F32), kv_shape, kv_shape],
        grid=(nb,),
        in_specs=in_specs,
        out_specs=[pl.BlockSpec((t, HW), lambda b: (b, 0)), kv_spec, kv_spec],
        input_output_aliases=aliases,
        compiler_params=_params(("parallel",)),
        name="context_attention",
    )(*args)


def _natten_body(q_ref, k_ref, v_ref, kc_ref, vc_ref, tab_ref, prev_ref, y_ref, *, rows_per_step, rows_n):
    del prev_ref
    rb = pl.program_id(2)
    n_loc = WIN_ROWS * GRID_W

    unroll = 4

    def rows(it, carry):
        work = []
        for u in range(unroll):
            i = it * unroll + u
            r = rb * rows_per_step + i
            rstart = jnp.clip(r - WIN_ROWS // 2, 0, rows_n - WIN_ROWS)
            qrows = pl.ds(pl.multiple_of(i * GRID_W, GRID_W), GRID_W)
            krows = pl.ds(pl.multiple_of(rstart * GRID_W, GRID_W), n_loc)
            work += [(qrows, krows, r - rstart, h, slice(N_HD * h, N_HD * (h + 1))) for h in range(2)]
        q = [q_ref[qr, sl] * (N_HD ** -0.5) for qr, _, _, _, sl in work]
        s_loc = [_dot(q_, k_ref[kr, sl], NT) + tab_ref[h, var] for q_, (_, kr, var, h, sl) in zip(q, work)]
        s_ctx = [_dot(q_, kc_ref[0, h], NT) for q_, (_, _, _, h, _) in zip(q, work)]
        m = [jnp.maximum(jnp.max(a, axis=-1, keepdims=True), jnp.max(c, axis=-1, keepdims=True))
             for a, c in zip(s_loc, s_ctx)]
        p_loc = [jnp.exp(a - m_) for a, m_ in zip(s_loc, m)]
        p_ctx = [jnp.exp(c - m_) for c, m_ in zip(s_ctx, m)]
        o = [_dot(a, v_ref[kr, sl]) + _dot(c, vc_ref[0, h])
             for a, c, (_, kr, _, h, sl) in zip(p_loc, p_ctx, work)]
        for o_, a, c, (qr, _, _, _, sl) in zip(o, p_loc, p_ctx, work):
            y_ref[qr, sl] = o_ / (jnp.sum(a, axis=-1, keepdims=True) + jnp.sum(c, axis=-1, keepdims=True))
        return carry

    lax.fori_loop(0, rows_per_step // unroll, rows, 0)


def _natten_call(z, k_ctx, v_ctx, tab, prev, *, row0, nb, t, rows_per_step=8):
    rows_n = t // GRID_W
    tq = rows_per_step * GRID_W
    qblk0 = row0 // tq
    sblk0 = row0 // t
    n_rb = rows_n // rows_per_step
    past = k_ctx.shape[2]
    kv_spec = lambda cb: pl.BlockSpec((t, LANE), lambda b, hp, rb: (sblk0 + b, cb + hp))
    cache_spec = pl.BlockSpec((1, 2, past, N_HD), lambda b, hp, rb: (b, hp, 0, 0))
    return pl.pallas_call(
        functools.partial(_natten_body, rows_per_step=rows_per_step, rows_n=rows_n),
        out_shape=jax.ShapeDtypeStruct((z.shape[0], HW), F32),
        grid=(nb, N_HEADS // 2, n_rb),
        in_specs=[pl.BlockSpec((tq, LANE), lambda b, hp, rb: (qblk0 + b * n_rb + rb, ZB_NQ + hp)),
                  kv_spec(ZB_NK), kv_spec(ZB_NV), cache_spec, cache_spec,
                  pl.BlockSpec((2, WIN_ROWS, GRID_W, WIN_ROWS * GRID_W), lambda b, hp, rb: (hp, 0, 0, 0)),
                  pl.BlockSpec(memory_space=pl.ANY)],
        out_specs=pl.BlockSpec((tq, LANE), lambda b, hp, rb: (qblk0 + b * n_rb + rb, hp)),
        input_output_aliases={6: 0},
        compiler_params=_params(("parallel", "parallel", "arbitrary")),
        name="neighbourhood_attention",
    )(z, z, z, k_ctx, v_ctx, tab, prev)


def _natten_table(rpb):
    cq = jnp.arange(GRID_W)[:, None]
    ck = jnp.arange(GRID_W)[None, :]
    dc = jnp.clip(ck - cq, -(WIN_COLS - 1), WIN_COLS - 1) + (WIN_COLS - 1)
    cstart = jnp.clip(cq - WIN_COLS // 2, 0, GRID_W - WIN_COLS)
    in_band = (ck >= cstart) & (ck < cstart + WIN_COLS)
    onehot = ((dc[None] == jnp.arange(2 * WIN_COLS - 1)[:, None, None]) & in_band[None]).astype(F32)
    cols = jnp.einsum('hrd,dqk->hrqk', rpb, onehot, precision=lax.Precision.HIGHEST)
    cols = cols + jnp.where(in_band, 0.0, NEG)
    tab = jnp.stack([cols[:, WIN_ROWS - 1 - var:2 * WIN_ROWS - 1 - var] for var in range(WIN_ROWS)], axis=1)
    return tab.transpose(0, 1, 3, 2, 4).reshape(rpb.shape[0], WIN_ROWS, GRID_W, WIN_ROWS * GRID_W)


def _group_norm(y, avg_bf, eps):
    mu = _dot_lhs2(y, avg_bf)
    yc = y - mu
    var = _dot_lhs2(yc * yc, avg_bf)
    return yc * lax.rsqrt(var + eps)


def _merge_body(*refs, with_router):
    refs = list(refs)
    (x_ref, mod_ref, yr_ref, bv_ref, hm_ref, yn_ref, gl_ref, mo_ref, g0_ref, g1_ref, g2_ref,
     avg64_ref, avg128_ref, rg2_ref, lnw_ref, lnb_ref, mlw_ref, wor_ref, wom_ref, won_ref, wout_ref,
     gpost_ref, gpre_ref) = [refs.pop(0) for _ in range(23)]
    wr_ref, br_ref = (refs.pop(0), refs.pop(0)) if with_router else (None, None)
    x1_ref, hin_ref = refs.pop(0), refs.pop(0)
    hin_hi_ref, sel_ref = (refs.pop(0), refs.pop(0)) if with_router else (None, None)
    m = mod_ref[0]
    yr = yr_ref[0] + yr_ref[1]
    yn_r = _group_norm(yr, avg64_ref[...], RWKV_GN_EPS) * lnw_ref[...] + lnb_ref[...]
    g = _dot(jax.nn.sigmoid(gl_ref[...]), rg2_ref[...])
    out_r = (yn_r + bv_ref[0] + bv_ref[1]) * g
    hn = _group_norm(hm_ref[0] + hm_ref[1], avg128_ref[...], NORM_EPS) * mlw_ref[...]
    out_m = jax.nn.sigmoid(mo_ref[...]) * hn
    merged = (jax.nn.sigmoid(g0_ref[...]) * _dot(out_r, wor_ref[...])
              + jax.nn.sigmoid(g1_ref[...]) * _dot(out_m, wom_ref[...])
              + jax.nn.sigmoid(g2_ref[...]) * _dot(yn_ref[...], won_ref[...]))
    o = _dot(merged, wout_ref[...])
    x1 = x_ref[...] + m[2:3] * _rms(o, gpost_ref[...])
    x1_ref[...] = x1
    hin = _rms(x1, gpre_ref[...]) * (1.0 + m[4:5]) + m[3:4]
    if with_router:
        sel_ref[...] = _top2_select(_dot3(hin, wr_ref[...]) + br_ref[...])
        hin_ref[...] = hin[:, :D // 2]
        hin_hi_ref[...] = hin[:, D // 2:]
    else:
        hin_ref[...] = hin.astype(BF16)


def _merge_call(x, mod3, yr, bv, hm, yn, z, W, n_ctx, lat_t, router, tm=256):
    ntok = x.shape[0]
    midx = _mod_index(tm, n_ctx, lat_t)
    with_router = router is not None
    row = lambda w: pl.BlockSpec((tm, w), lambda i: (i, 0))
    dirs = pl.BlockSpec((2, tm, HW), lambda i: (0, i, 0))
    zspec = lambda w, cb: pl.BlockSpec((tm, w), lambda i: (i, cb))
    full = lambda a: pl.BlockSpec(a.shape, lambda i: (0,) * a.ndim)
    consts = [W['avg64'], W['avg128'], W['rg2'], W['lnw'], W['lnb'], W['mlw'], W['wor'], W['wom'], W['won'],
              W['wout'], W['gpost'], W['gpre']]
    in_specs = [row(D), pl.BlockSpec((1, 8, D), lambda i: (midx(i), 0, 0)), dirs, dirs, dirs, row(HW),
                zspec(LANE, ZB_GL), zspec(HW, ZB_MO // 4),
                zspec(D, ZB_GATE // 8), zspec(D, ZB_GATE // 8 + 1), zspec(D, ZB_GATE // 8 + 2)]
    in_specs += [full(a) for a in consts]
    args = [x, mod3, yr, bv, hm, yn, z, z, z, z, z] + consts
    if with_router:
        half = jax.ShapeDtypeStruct((ntok, D // 2), F32)
        out_shape = [jax.ShapeDtypeStruct((ntok, D), F32), half, half]
        out_specs = [row(D), row(D // 2), row(D // 2)]
    else:
        out_shape = [jax.ShapeDtypeStruct((ntok, D), F32), jax.ShapeDtypeStruct((ntok, D), BF16)]
        out_specs = [row(D), row(D)]
    if with_router:
        in_specs += [full(router[0]), full(router[1])]
        args += list(router)
        out_shape.append(jax.ShapeDtypeStruct((ntok, LANE), F32))
        out_specs.append(row(LANE))
    return pl.pallas_call(
        functools.partial(_merge_body, with_router=with_router),
        out_shape=out_shape,
        grid=(ntok // tm,),
        in_specs=in_specs,
        out_specs=out_specs,
        compiler_params=_params(("parallel",)),
        name="branch_merge",
    )(*args)


def _top2_select(lg):
    lane = lax.broadcasted_iota(jnp.int32, lg.shape, 1)
    m1 = jnp.max(lg, axis=1, keepdims=True)
    i1 = jnp.min(jnp.where(lg == m1, lane, LANE), axis=1, keepdims=True)
    lg2 = jnp.where(lane == i1, -jnp.inf, lg)
    m2 = jnp.max(lg2, axis=1, keepdims=True)
    i2 = jnp.min(jnp.where(lg2 == m2, lane, LANE), axis=1, keepdims=True)
    e2 = jnp.exp(m2 - m1)
    den = 1.0 + e2
    return jnp.where(lane == 0, i1.astype(F32),
                     jnp.where(lane == 1, i2.astype(F32),
                               jnp.where(lane == 2, 1.0 / den, jnp.where(lane == 3, e2 / den, 0.0))))


def _ffn_body(h_ref, x1_ref, mod_ref, gpost_ref, wg_ref, wu_ref, wd_ref, o_ref, acc_s):
    j = pl.program_id(1)

    @pl.when(j == 0)
    def _():
        acc_s[...] = jnp.zeros_like(acc_s)

    hin = h_ref[...]
    hg = jnp.dot(hin, wg_ref[...], preferred_element_type=F32)
    hu = jnp.dot(hin, wu_ref[...], preferred_element_type=F32)
    acc_s[...] += _dot(hg * jax.nn.sigmoid(hg) * hu, wd_ref[...])

    @pl.when(j == pl.num_programs(1) - 1)
    def _():
        m = mod_ref[0]
        o_ref[...] = x1_ref[...] + m[5:6] * _rms(acc_s[...], gpost_ref[...])


def _ffn_call(hin, x1, mod3, gpost, wg, wu, wd, n_ctx, lat_t, tm=512, tf=2816):
    ntok = hin.shape[0]
    dff = wg.shape[1]
    midx = _mod_index(tm, n_ctx, lat_t)
    return pl.pallas_call(
        _ffn_body,
        out_shape=jax.ShapeDtypeStruct((ntok, D), F32),
        grid=(ntok // tm, dff // tf),
        in_specs=[pl.BlockSpec((tm, D), lambda i, j: (i, 0)),
                  pl.BlockSpec((tm, D), lambda i, j: (i, 0)),
                  pl.BlockSpec((1, 8, D), lambda i, j: (midx(i), 0, 0)),
                  pl.BlockSpec((1, D), lambda i, j: (0, 0)),
                  pl.BlockSpec((D, tf), lambda i, j: (0, j)),
                  pl.BlockSpec((D, tf), lambda i, j: (0, j)),
                  pl.BlockSpec((tf, D), lambda i, j: (j, 0))],
        out_specs=pl.BlockSpec((tm, D), lambda i, j: (i, 0)),
        scratch_shapes=[pltpu.VMEM((tm, D), F32)],
        compiler_params=_params(("parallel", "arbitrary")),
        name="dense_swiglu",
    )(hin, x1, mod3, gpost, wg, wu, wd)


MOE_TILE = 512
MOE_TF = 1792
CMB_TILE = 128
SLAB_ALIGN = 8
SLAB = CMB_TILE + SLAB_ALIGN


def _moe_plan(sel, ntok):
    i32 = jnp.int32
    e1, e2 = sel[:, 0].astype(i32), sel[:, 1].astype(i32)
    ar = jnp.arange(ntok, dtype=i32)
    ex = jnp.arange(N_EXPERTS, dtype=i32)
    oh = ((e1[:, None] == ex) | (e2[:, None] == ex)).astype(i32)
    csum = jnp.cumsum(oh, axis=0)
    total = csum[-1]
    gsz = (total + MOE_TILE - 1) // MOE_TILE * MOE_TILE
    gend = jnp.cumsum(gsz)
    gstart = gend - gsz
    dstart = jnp.cumsum(total) - total
    key, wts = lax.sort((jnp.concatenate([e1 * ntok + ar, e2 * ntok + ar]),
                         jnp.concatenate([sel[:, 2], sel[:, 3]])), num_keys=1)
    tok_d = key % ntok
    n_pad = 2 * ntok + (N_EXPERTS + 1) * MOE_TILE
    p = jnp.arange(n_pad, dtype=i32)
    ge = jnp.minimum(jnp.sum((p[:, None] >= gend[None, :]).astype(i32), axis=1), N_EXPERTS - 1)
    pick = lambda tab: jnp.sum(jnp.where(ge[:, None] == ex, tab[None, :], 0), axis=1)
    q = p - pick(gstart)
    live = (q < pick(total)) & (p < gend[-1])
    src = jnp.clip(pick(dstart) + q, 0, 2 * ntok - 1)
    tok_p = jnp.where(live, tok_d[src], 0)
    w_p = jnp.where(live, wts[src], 0.0)
    tstart = jnp.arange(n_pad // MOE_TILE, dtype=i32) * MOE_TILE
    texp = jnp.minimum(jnp.sum((tstart[:, None] >= gend[None, :]).astype(i32), axis=1), N_EXPERTS - 1)
    tval = (tstart < gend[-1]).astype(i32)
    r0 = (csum - oh)[::CMB_TILE]
    r1 = jnp.concatenate([r0[1:], total[None]], axis=0)
    alo = (gstart[None, :] + r0).reshape(-1)
    ahi = (gstart[None, :] + r1).reshape(-1)
    return dict(tok=tok_p, w=w_p.reshape(n_pad, 1), texp=texp, tval=tval,
                a8=alo // SLAB_ALIGN, alo=alo, ahi=ahi, n_pad=n_pad)


def _moe_gather_body(tok_ref, src_ref, o_ref, buf_s):
    base = pl.program_id(0) * MOE_TILE

    def gather(r, carry):
        buf_s[pl.ds(r, 1), :] = src_ref[pl.ds(tok_ref[base + r], 1), :]
        return carry

    lax.fori_loop(0, MOE_TILE, gather, 0, unroll=8)
    o_ref[...] = buf_s[...].astype(BF16)


def _moe_gather_call(src, plan):
    n_pad = plan['n_pad']
    width = src.shape[1]
    grid_spec = pltpu.PrefetchScalarGridSpec(
        num_scalar_prefetch=1,
        grid=(n_pad // MOE_TILE,),
        in_specs=[pl.BlockSpec(memory_space=pltpu.VMEM)],
        out_specs=pl.BlockSpec((MOE_TILE, width), lambda i, tk: (i, 0)),
        scratch_shapes=[pltpu.VMEM((MOE_TILE, width), F32)])
    return pl.pallas_call(
        _moe_gather_body,
        out_shape=jax.ShapeDtypeStruct((n_pad, width), BF16),
        grid_spec=grid_spec,
        compiler_params=_params(("arbitrary",)),
        name="moe_row_gather",
    )(plan['tok'], src)


def _moe_group_body(texp_ref, tval_ref, xa_ref, xb_ref, ws_ref, wg_ref, wu_ref, wd_ref, ys_ref, acc_s):
    del texp_ref
    i = pl.program_id(0)
    j = pl.program_id(1)
    half = D // 2

    @pl.when(j == 0)
    def _():
        acc_s[...] = jnp.zeros_like(acc_s)

    @pl.when(tval_ref[i] > 0)
    def _():
        xa, xb = xa_ref[...], xb_ref[...]
        mm = lambda u, ref, rows: jnp.dot(u, ref[0, rows, :], preferred_element_type=F32)
        hg = mm(xa, wg_ref, slice(0, half)) + mm(xb, wg_ref, slice(half, D))
        hu = mm(xa, wu_ref, slice(0, half)) + mm(xb, wu_ref, slice(half, D))
        acc_s[...] += _dot(hg * jax.nn.sigmoid(hg) * hu, wd_ref[0])

    @pl.when(j == pl.num_programs(1) - 1)
    def _():
        ys_ref[...] = acc_s[...] * ws_ref[...]


def _moe_group_call(xa, xb, plan, wg, wu, wd):
    n_pad = plan['n_pad']
    dff = wg.shape[2]
    xspec = pl.BlockSpec((MOE_TILE, D // 2), lambda i, j, te, tv: (i, 0))
    grid_spec = pltpu.PrefetchScalarGridSpec(
        num_scalar_prefetch=2,
        grid=(n_pad // MOE_TILE, dff // MOE_TF),
        in_specs=[xspec, xspec,
                  pl.BlockSpec((MOE_TILE, 1), lambda i, j, te, tv: (i, 0)),
                  pl.BlockSpec((1, D, MOE_TF), lambda i, j, te, tv: (te[i], 0, j)),
                  pl.BlockSpec((1, D, MOE_TF), lambda i, j, te, tv: (te[i], 0, j)),
                  pl.BlockSpec((1, MOE_TF, D), lambda i, j, te, tv: (te[i], j, 0))],
        out_specs=pl.BlockSpec((MOE_TILE, D), lambda i, j, te, tv: (i, 0)),
        scratch_shapes=[pltpu.VMEM((MOE_TILE, D), F32)])
    return pl.pallas_call(
        _moe_group_body,
        out_shape=jax.ShapeDtypeStruct((n_pad, D), F32),
        grid_spec=grid_spec,
        compiler_params=_params(("parallel", "arbitrary")),
        name="moe_expert_swiglu",
    )(plan['texp'], plan['tval'], xa, xb, plan['w'], wg, wu, wd)


def _moe_combine_body(a8_ref, alo_ref, ahi_ref, *refs, tile0):
    ys_refs, tk_refs = refs[:N_EXPERTS], refs[N_EXPERTS:2 * N_EXPERTS]
    x1_ref, mod_ref, gpost_ref, o_ref = refs[2 * N_EXPERTS:]
    t = pl.program_id(0) + tile0
    tloc = t * CMB_TILE + lax.broadcasted_iota(jnp.int32, (SLAB, CMB_TILE), 1)
    srow = lax.broadcasted_iota(jnp.int32, (SLAB, 1), 0)
    f = jnp.zeros((CMB_TILE, D), F32)
    for e in range(N_EXPERTS):
        k = t * N_EXPERTS + e
        row = a8_ref[k] * SLAB_ALIGN + srow
        tok = jnp.where((row >= alo_ref[k]) & (row < ahi_ref[k]), tk_refs[e][...], -1)
        onehot = jnp.where(tok == tloc, 1.0, 0.0).astype(BF16)
        f = f + _dot_rhs2(onehot, ys_refs[e][...], TN)
    m = mod_ref[0]
    o_ref[...] = x1_ref[...] + m[5:6] * _rms(f, gpost_ref[...])


def _moe_combine_call(ys, plan, x1, mod3, gpost, n_ctx, lat_t, row0, nrows):
    midx = _mod_index(CMB_TILE, n_ctx, lat_t)
    tile0 = row0 // CMB_TILE

    def slab(w, e):
        return pl.BlockSpec((pl.Element(SLAB), pl.Element(w)),
                            lambda t, a8, lo, hi: (a8[(t + tile0) * N_EXPERTS + e] * SLAB_ALIGN, 0))

    experts = range(N_EXPERTS)
    grid_spec = pltpu.PrefetchScalarGridSpec(
        num_scalar_prefetch=3,
        grid=(nrows // CMB_TILE,),
        in_specs=[slab(D, e) for e in experts] + [slab(1, e) for e in experts] + [
            pl.BlockSpec((CMB_TILE, D), lambda t, a8, lo, hi: (t + tile0, 0)),
            pl.BlockSpec((1, 8, D), lambda t, a8, lo, hi: (midx(t + tile0), 0, 0)),
            pl.BlockSpec((1, D), lambda t, a8, lo, hi: (0, 0))],
        out_specs=pl.BlockSpec((CMB_TILE, D), lambda t, a8, lo, hi: (t, 0)))
    tok2d = plan['tok'].reshape(-1, 1)
    return pl.pallas_call(
        functools.partial(_moe_combine_body, tile0=tile0),
        out_shape=jax.ShapeDtypeStruct((nrows, D), F32),
        grid_spec=grid_spec,
        compiler_params=_params(("parallel",)),
        name="moe_combine",
    )(plan['a8'], plan['alo'], plan['ahi'], *([ys] * N_EXPERTS), *([tok2d] * N_EXPERTS), x1, mod3, gpost)


def _rope_tables(t):
    i = jnp.arange(M_HD)
    half, pair, f = i // 64, (i % 64) // 32, i % 32
    tt = jnp.arange(t)
    pos = jnp.stack([tt // GRID_W, tt % GRID_W], axis=-1).astype(F32)
    inv = ROPE_BASE ** (-f.astype(F32) / 32)
    ang = pos[:, half] * inv[None, :]
    src = jnp.where(pair == 0, i + 32, i - 32)
    perm = jnp.where(i[:, None] == src[None, :], jnp.where(pair == 0, -1.0, 1.0)[None, :], 0.0)
    return jnp.cos(ang), jnp.sin(ang), perm.astype(BF16)


def _pad_in_proj(w):
    off = {}
    o = 0
    for name, width in (('r', 512), ('k', 512), ('v', 512), ('wl', 128), ('al', 128), ('gl', 128),
                        ('mq', 512), ('mk', 512), ('mv', 512), ('mo', 512), ('mif', 16),
                        ('nq', 512), ('nk', 512), ('nv', 512), ('gate', 3072)):
        off[name] = (o, o + width)
        o += width
    wb = w.astype(BF16)
    col = lambda n: wb[:, off[n][0]:off[n][1]]
    parts = [col(n) for n in ('r', 'k', 'v', 'mq', 'mk', 'mv', 'mo', 'nq', 'nk', 'nv', 'gate', 'wl', 'al', 'gl', 'mif')]
    parts.append(jnp.zeros((w.shape[0], LANE - 16), BF16))
    return jnp.concatenate(parts, axis=1)


def _block_avg(width, group):
    i = jnp.arange(width) // group
    return jnp.where(i[:, None] == i[None, :], 1.0 / group, 0.0).astype(BF16)


def _zero_pad_rows(w2):
    z = jnp.zeros_like(w2[0])
    return jnp.stack([jnp.concatenate([w2[0], z], axis=0), jnp.concatenate([z, w2[1]], axis=0)])


def kernel(x_prompt, x_sample, state_rwkv, state_mlstm_C, state_mlstm_n, state_mlstm_m, cache_nat_k, cache_nat_v,
           c, c_ctx, w_mod, b_mod, g_pre_mix, g_post_mix, g_pre_ffn, g_post_ffn, w_in,
           rw_w0, rw_w2, rw_a0, rw_a2, rw_g2, rw_k_k, rw_k_a, rw_r_k, rw_ln_w, rw_ln_b,
           ml_b_i, ml_b_f, ml_norm_w, nat_rpb, w_o_rwkv, w_o_mlstm, w_o_nat, w_out,
           ff_w_gate, ff_w_up, ff_w_down, moe_w_router, moe_b_router, moe_w_gate, moe_w_up, moe_w_down):
    cb, ct = x_prompt.shape[:2]
    lb, lt = x_sample.shape[:2]
    n_ctx, n_lat = cb * ct, lb * lt
    x = jnp.concatenate([x_prompt.reshape(n_ctx, D), x_sample.reshape(n_lat, D)], axis=0)
    cvec = jnp.zeros((8, D), F32).at[0].set(c_ctx).at[1:1 + lb].set(c)
    rope = _rope_tables(lt)
    avg64, avg128 = _block_avg(HW, R_HD), _block_avg(HW, M_HD)
    s_rw = ml_st = kv_c = None
    for l in range(DEPTH):
        mod = _mod_call(cvec, w_mod[l], b_mod[l])
        mod3 = jnp.pad(mod[:1 + lb].reshape(1 + lb, 6, D), ((0, 0), (0, 2), (0, 0)))
        z = _proj_call(x, mod3, g_pre_mix[l], _pad_in_proj(w_in[l]), n_ctx, lt)

        rp = dict(w0=rw_w0[l].reshape(2, 1, HW), w2=_zero_pad_rows(rw_w2[l]),
                  a0=rw_a0[l].reshape(2, 1, HW), a2=_zero_pad_rows(rw_a2[l]),
                  kk=rw_k_k[l].reshape(1, HW), ka=rw_k_a[l].reshape(1, HW), rk=rw_r_k[l].reshape(1, HW))
        yr, bv, s_rw = _rwkv_call(z, rp, None, None, row0=0, nb=cb, t=ct, tb=RW_BLOCK, layer=l,
                                  prev_state=None if s_rw is None else (s_rw,))
        yr, bv = _rwkv_call(z, rp, state_rwkv[:, l], (yr, bv), row0=n_ctx, nb=lb, t=lt, tb=RW_BLOCK)

        gt = z[:, ZB_MIF * LANE:ZB_MIF * LANE + 16].reshape(-1, CHUNK, 16).transpose(0, 2, 1)
        hm, *ml_st = _mlstm_call(z, gt, None, ml_b_i[l], ml_b_f[l], None, None,
                                 row0=0, nb=cb, t=ct, tb=ML_BLOCK, layer=l, prev_state=ml_st)
        init = (state_mlstm_C[:, l], state_mlstm_n[:, l][:, :, :, None, :], state_mlstm_m[:, l])
        hm, = _mlstm_call(z, gt, rope, ml_b_i[l], ml_b_f[l], init, (hm,),
                          row0=n_ctx, nb=lb, t=lt, tb=ML_BLOCK)

        yn, *kv_c = _ctx_attn_call(z, kv_c, nb=cb, t=ct, layer=l)
        yn = _natten_call(z, cache_nat_k[:, l], cache_nat_v[:, l], _natten_table(nat_rpb[l]), yn,
                          row0=n_ctx, nb=lb, t=lt)

        mw = dict(avg64=avg64, avg128=avg128, rg2=rw_g2[l].astype(BF16), lnw=rw_ln_w[l].reshape(1, HW),
                  lnb=rw_ln_b[l].reshape(1, HW), mlw=ml_norm_w[l].reshape(1, HW),
                  wor=w_o_rwkv[l].astype(BF16), wom=w_o_mlstm[l].astype(BF16), won=w_o_nat[l].astype(BF16),
                  wout=w_out[l].astype(BF16), gpost=g_post_mix[l].reshape(1, D), gpre=g_pre_ffn[l].reshape(1, D))
        j = l // 2
        if l % 2 == 0:
            router = None
        else:
            wr = jnp.pad(moe_w_router[j], ((0, 0), (0, LANE - N_EXPERTS)))
            br = jnp.pad(moe_b_router[j], (0, LANE - N_EXPERTS), constant_values=NEG).reshape(1, LANE)
            router = (wr, br)
        merged = _merge_call(x, mod3, yr, bv, hm, yn, z, mw, n_ctx, lt, router)
        gpost = g_post_ffn[l].reshape(1, D)
        if l % 2 == 0:
            x = _ffn_call(merged[1], merged[0], mod3, gpost, ff_w_gate[j].astype(BF16),
                          ff_w_up[j].astype(BF16), ff_w_down[j].astype(BF16), n_ctx, lt)
        else:
            plan = _moe_plan(merged[3], n_ctx + n_lat)
            xa, xb = _moe_gather_call(merged[1], plan), _moe_gather_call(merged[2], plan)
            ys = _moe_group_call(xa, xb, plan, moe_w_gate[j].astype(BF16), moe_w_up[j].astype(BF16),
                                 moe_w_down[j].astype(BF16))
            groups = [_moe_combine_call(ys, plan, merged[0], mod3, gpost, n_ctx, lt, r0, nr)
                      for r0, nr in ((0, n_ctx), (n_ctx, n_lat))]
            x = jnp.concatenate(groups, axis=0) if l + 1 < DEPTH else None
        if x is not None:
            groups = [x[:n_ctx], x[n_ctx:]]

    c_m, n_m, m_m = ml_st
    return (groups[0].reshape(cb, ct, D), groups[1].reshape(lb, lt, D),
            s_rw, c_m, n_m[:, :, :, :, 0, :], m_m[:, :, :, :, 0, 0], kv_c[0], kv_c[1])
```

```python
import functools
import math

import jax
import jax.numpy as jnp
from jax import lax
from jax.experimental import pallas as pl
from jax.experimental.pallas import tpu as pltpu

F32 = jnp.float32
BF16 = jnp.bfloat16

D = 1024
DEPTH = 2
GRID_W = 64
R_HEADS, R_HD = 8, 64
M_HEADS, M_HD = 4, 128
N_HEADS, N_HD = 8, 64
HW = 512
WIN_ROWS, WIN_COLS = 8, 16
N_EXPERTS = 8
RWKV_GN_EPS = 64e-5
NORM_EPS = 1e-6
ROPE_BASE = 10000.0
CHUNK = 64
NEG = -1e30
LANE = 128
VMEM_LIMIT = 56 * 1024 * 1024

ZB_R, ZB_K, ZB_V = 0, 4, 8
ZB_MQ, ZB_MK, ZB_MV, ZB_MO = 12, 16, 20, 24
ZB_NQ, ZB_NK, ZB_NV = 28, 32, 36
ZB_GATE = 40
ZB_WL, ZB_AL, ZB_GL, ZB_MIF = 64, 65, 66, 67
P_PAD = 68 * LANE

NN = (((1,), (0,)), ((), ()))
NT = (((1,), (1,)), ((), ()))
TN = (((0,), (0,)), ((), ()))


def _dot(a, b, dims=NN):
    return lax.dot_general(a.astype(BF16), b.astype(BF16), dims, preferred_element_type=F32)


def _split(x):
    hi = x.astype(BF16)
    return hi, (x - hi.astype(F32)).astype(BF16)


def _dot3(a, b, dims=NN):
    ah, al = _split(a)
    bh, bl = _split(b)
    d = lambda u, w: lax.dot_general(u, w, dims, preferred_element_type=F32)
    return d(ah, bh) + (d(ah, bl) + d(al, bh))


def _dot_lhs2(a, b_exact, dims=NN):
    ah, al = _split(a)
    d = lambda u: lax.dot_general(u, b_exact, dims, preferred_element_type=F32)
    return d(ah) + d(al)


def _dot_rhs2(a_exact, b, dims=NN):
    bh, bl = _split(b)
    d = lambda w: lax.dot_general(a_exact, w, dims, preferred_element_type=F32)
    return d(bh) + d(bl)


def _log_sigmoid(x):
    return jnp.minimum(x, 0.0) - jnp.log1p(jnp.exp(-jnp.abs(x)))


def _rms(x, g):
    return x * lax.rsqrt(jnp.mean(x * x, axis=-1, keepdims=True) + NORM_EPS) * g


def _params(sem):
    return pltpu.CompilerParams(dimension_semantics=sem, vmem_limit_bytes=VMEM_LIMIT)


def _alias_prev(in_specs, args, prev, first_out=0):
    aliases = {}
    for k, a in enumerate(prev or ()):
        aliases[len(args)] = first_out + k
        in_specs.append(pl.BlockSpec(memory_space=pl.ANY))
        args.append(a)
    return aliases


def _mod_body(c_ref, w_ref, b_ref, o_ref):
    c = c_ref[...]
    o_ref[...] = _dot(c * jax.nn.sigmoid(c), w_ref[...]) + b_ref[...]


def _mod_call(cvec, w_mod, b_mod):
    tn = 1536
    return pl.pallas_call(
        _mod_body,
        out_shape=jax.ShapeDtypeStruct((8, 6 * D), F32),
        grid=(6 * D // tn,),
        in_specs=[pl.BlockSpec((8, D), lambda j: (0, 0)),
                  pl.BlockSpec((D, tn), lambda j: (0, j)),
                  pl.BlockSpec((1, tn), lambda j: (0, j))],
        out_specs=pl.BlockSpec((8, tn), lambda j: (0, j)),
        compiler_params=_params(("arbitrary",)),
        name="adaln_mod",
    )(cvec, w_mod, b_mod.reshape(1, 6 * D))


def _mod_index(tile_rows, n_ctx, lat_t):
    def f(i):
        start = i * tile_rows
        return jnp.where(start < n_ctx, 0, 1 + (start - n_ctx) // lat_t)
    return f


def _proj_body(x_ref, mod_ref, g_ref, w_ref, z_ref, hin_s):
    @pl.when(pl.program_id(1) == 0)
    def _():
        m = mod_ref[0]
        hin_s[...] = (_rms(x_ref[...], g_ref[...]) * (1.0 + m[1:2]) + m[0:1]).astype(BF16)
    z_ref[...] = jnp.dot(hin_s[...], w_ref[...], preferred_element_type=F32)


def _proj_call(x, mod3, g, w_bf, n_ctx, lat_t, tm=1024, tn=2176):
    ntok = x.shape[0]
    midx = _mod_index(tm, n_ctx, lat_t)
    return pl.pallas_call(
        _proj_body,
        out_shape=jax.ShapeDtypeStruct((ntok, P_PAD), F32),
        grid=(ntok // tm, P_PAD // tn),
        in_specs=[pl.BlockSpec((tm, D), lambda i, j: (i, 0)),
                  pl.BlockSpec((1, 8, D), lambda i, j: (midx(i), 0, 0)),
                  pl.BlockSpec((1, D), lambda i, j: (0, 0)),
                  pl.BlockSpec((D, tn), lambda i, j: (0, j))],
        out_specs=pl.BlockSpec((tm, tn), lambda i, j: (i, j)),
        scratch_shapes=[pltpu.VMEM((tm, D), BF16)],
        compiler_params=_params(("parallel", "arbitrary")),
        name="in_proj",
    )(x, mod3, g.reshape(1, D), w_bf)


RW_TILE = 2 * CHUNK
RW_BLOCK = 1024
ML_BLOCK = 512


def _rwkv_body(*refs, tb_rows, nseq, has_init, has_out, n_alias):
    (r_ref, k_ref, v_ref, wl_ref, al_ref, w0_ref, w2_ref, a0_ref, a2_ref,
     kkw_ref, kaw_ref, rkw_ref, hsum_ref) = refs[:13]
    rest = list(refs[13:])
    s0_ref = rest.pop(0) if has_init else None
    del rest[:n_alias]
    y_ref, bv_ref = rest.pop(0), rest.pop(0)
    sout_ref = rest.pop(0) if has_out else None
    r2_s, m_s, g_s, st_s = rest
    C = CHUNK
    TL = RW_TILE
    nchunk = tb_rows // nseq // C
    d = pl.program_id(2)
    tb = pl.program_id(3)
    n_tb = pl.num_programs(3)

    ti = lax.broadcasted_iota(jnp.int32, (TL, TL), 0)
    si = lax.broadcasted_iota(jnp.int32, (TL, TL), 1)
    head_blk = (ti >> 6) == (si >> 6)

    @pl.when(tb == 0)
    def _():
        for q in range(nseq):
            if has_init:
                s0 = s0_ref[q, 0]
                two = jnp.concatenate([jnp.concatenate([s0[0], s0[0]], axis=1),
                                       jnp.concatenate([s0[1], s0[1]], axis=1)], axis=0)
                st_s[q] = jnp.where(head_blk, two, 0.0)
            else:
                st_s[q] = jnp.zeros((LANE, LANE), F32)

    hsum = hsum_ref[...]
    r = r_ref[...]
    kf = k_ref[...]
    v = v_ref[...]
    w_pre = w0_ref[0] + _dot(jnp.tanh(wl_ref[...]), w2_ref[0])
    lw = -math.exp(-0.5) * jax.nn.sigmoid(w_pre)
    a = jax.nn.sigmoid(a0_ref[0] + _dot(al_ref[...], a2_ref[0]))
    kd = kf * (1.0 + (a - 1.0) * kaw_ref[...])
    kkf = kf * kkw_ref[...]
    kn = kkf * lax.rsqrt(_dot_lhs2(kkf * kkf, hsum) + 1e-12)
    bd = kn * a
    bv_ref[0] = _dot_lhs2(r * kd * rkw_ref[...], hsum) * v

    dlt = (ti - si) * (1 - 2 * d)
    strict = head_blk & (dlt > 0)
    incl = head_blk & (dlt >= 0)
    cum_ones = jnp.concatenate([jnp.where(incl, 1.0, 0.0), jnp.where(head_blk, 1.0, 0.0)], axis=0).astype(BF16)
    is_diag = ti == si
    eye = jnp.where(is_diag, 1.0, 0.0)
    pair = (ti >> 1) == (si >> 1)
    sibling = [((ti >> lg) ^ (si >> lg)) == 1 for lg in range(1, 6)]
    head0 = si < R_HD

    n_tiles = tb_rows // TL
    tiles = []
    for tile in range(n_tiles):
        rows = slice(TL * tile, TL * (tile + 1))
        cs = _dot_rhs2(cum_ones, lw[rows])
        cum, tot = cs[:TL], cs[TL:]
        e_neg = jnp.exp(-cum)
        e_hat = jnp.exp(tot - cum)
        rt = r[rows] * jnp.exp(cum)
        at = -kn[rows] * jnp.exp(cum - lw[rows])
        tiles.append(dict(rows=rows, tot=tot, rt=rt, at=at, bh=bd[rows] * e_hat, kh=kd[rows] * e_hat,
                          rhs=jnp.concatenate([bd[rows] * e_neg, kd[rows] * e_neg], axis=0).astype(BF16),
                          at_bf=at.astype(BF16), v_bf=v[rows].astype(BF16)))
    chains = [(tl, h) for tl in tiles for h in range(2)]
    a_all = []
    for tl, h in chains:
        hm = head0 if h == 0 else ~head0
        lhs = jnp.concatenate([jnp.where(hm, tl['at'], 0.0), jnp.where(hm, tl['rt'], 0.0)], axis=0)
        a_all.append(_dot(lhs, tl['rhs'], NT))
    n_mat = [jnp.where(strict, a[:TL, :TL], 0.0) for a in a_all]
    a_ak = [jnp.where(strict, a[:TL, TL:], 0.0).astype(BF16) for a in a_all]
    a_rb = [jnp.where(incl, a[TL:, :TL], 0.0).astype(BF16) for a in a_all]
    a_rk = [jnp.where(incl, a[TL:, TL:], 0.0).astype(BF16) for a in a_all]
    t_inv = [eye + jnp.where(pair, n, 0.0) for n in n_mat]
    for sib in sibling:
        x_mat = [_dot(jnp.where(sib, n, 0.0), t) for n, t in zip(n_mat, t_inv)]
        t_inv = [t + _dot(t, x) for t, x in zip(t_inv, x_mat)]
    p_h = [_dot(t, tl['at_bf']) for t, (tl, _) in zip(t_inv, chains)]
    av = [_dot(a, tl['v_bf']) for a, (tl, _) in zip(a_ak, chains)]
    q_h = [_dot(t, x) for t, x in zip(t_inv, av)]
    r2_h = [_dot(a, p) for a, p in zip(a_rb, p_h)]
    y0_h = [_dot(a, q) + _dot(k_, tl['v_bf']) for a, q, k_, (tl, _) in zip(a_rb, q_h, a_rk, chains)]
    p_m, q_m = [], []
    for i, tl in enumerate(tiles):
        rows = tl['rows']
        pick = lambda u: jnp.where(head0, u[2 * i], u[2 * i + 1])
        p_m.append(pick(p_h))
        q_m.append(pick(q_h))
        r2_s[rows, :] = tl['rt'] + pick(r2_h)
        y_ref[0, rows, :] = pick(y0_h)
        for c in range(TL // C):
            cr = slice(C * c, C * (c + 1))
            decay = jnp.where(is_diag, jnp.exp(tl['tot'][C * c:C * c + 1]), 0.0)
            m_s[i * (TL // C) + c] = decay + jnp.where(head_blk, _dot(p_m[i][cr], tl['bh'][cr], TN), 0.0)
            g_s[i * (TL // C) + c] = jnp.where(
                head_blk, _dot(q_m[i][cr], tl['bh'][cr], TN) + _dot(v[rows][cr], tl['kh'][cr], TN), 0.0)

    def step(i, states):
        ci = jnp.where(d == 0, i, nchunk - 1 - i)
        new = []
        for q, s in enumerate(states):
            cq = q * nchunk + ci
            rows = pl.ds(pl.multiple_of(cq * C, C), C)
            y_ref[0, rows, :] = y_ref[0, rows, :] + _dot(r2_s[rows, :], s, NT)
            new.append(_dot3(s, m_s[cq]) + g_s[cq])
        return tuple(new)

    s_fin = lax.fori_loop(0, nchunk, step, tuple(st_s[q] for q in range(nseq)))
    for q in range(nseq):
        st_s[q] = s_fin[q]

    if has_out:
        @pl.when(tb == n_tb - 1)
        def _():
            for q in range(nseq):
                sout_ref[q, 0, 0, 0] = s_fin[q][:R_HD, :R_HD]
                sout_ref[q, 0, 0, 1] = s_fin[q][R_HD:, R_HD:]


def _rwkv_call(z, P, s0, prev, *, row0, nb, t, tb, layer=None, prev_state=None):
    nseq = max(tb // t, 1)
    n_tb = max(t // tb, 1)
    blk0 = row0 // tb
    has_init = s0 is not None

    def rowblk(b, d, i):
        return blk0 + b * n_tb + jnp.where(d == 0, i, n_tb - 1 - i)

    def zspec(cb):
        return pl.BlockSpec((tb, LANE), lambda b, hp, d, i: (rowblk(b, d, i), cb(hp)))

    vec = lambda: pl.BlockSpec((1, LANE), lambda b, hp, d, i: (0, hp))
    dvec = lambda: pl.BlockSpec((1, 1, LANE), lambda b, hp, d, i: (d, 0, hp))
    dmat = lambda: pl.BlockSpec((1, LANE, LANE), lambda b, hp, d, i: (d, 0, hp))
    st_spec = pl.BlockSpec((nseq, 1, 2, R_HD, R_HD), lambda b, hp, d, i: (b, d, hp, 0, 0))
    in_specs = [zspec(lambda hp: ZB_R + hp), zspec(lambda hp: ZB_K + hp), zspec(lambda hp: ZB_V + hp),
                zspec(lambda hp: ZB_WL), zspec(lambda hp: ZB_AL),
                dvec(), dmat(), dvec(), dmat(), vec(), vec(), vec(),
                pl.BlockSpec((LANE, LANE), lambda b, hp, d, i: (0, 0))]
    args = [z, z, z, z, z, P['w0'], P['w2'], P['a0'], P['a2'], P['kk'], P['ka'], P['rk'],
            _block_avg(LANE, R_HD) * R_HD]
    if has_init:
        in_specs.append(st_spec)
        args.append(s0)
    has_out = layer is not None
    aliases = _alias_prev(in_specs, args, prev)
    aliases.update(_alias_prev(in_specs, args, prev_state, first_out=2))
    yspec = pl.BlockSpec((1, tb, LANE), lambda b, hp, d, i: (d, rowblk(b, d, i), hp))
    out_shape = [jax.ShapeDtypeStruct((2, z.shape[0], HW), F32)] * 2
    out_specs = [yspec, yspec]
    if has_out:
        out_shape.append(jax.ShapeDtypeStruct((nb, DEPTH, 2, R_HEADS, R_HD, R_HD), F32))
        out_specs.append(pl.BlockSpec((nseq, 1, 1, 2, R_HD, R_HD), lambda b, hp, d, i: (b, layer, d, hp, 0, 0)))
    return pl.pallas_call(
        functools.partial(_rwkv_body, tb_rows=tb, nseq=nseq, has_init=has_init, has_out=has_out,
                          n_alias=len(aliases)),
        out_shape=out_shape,
        grid=(nb // nseq, R_HEADS // 2, 2, n_tb),
        in_specs=in_specs,
        out_specs=out_specs,
        input_output_aliases=aliases,
        scratch_shapes=[pltpu.VMEM((tb, LANE), F32), pltpu.VMEM((tb // CHUNK, LANE, LANE), F32),
                        pltpu.VMEM((tb // CHUNK, LANE, LANE), F32), pltpu.VMEM((nseq, LANE, LANE), F32)],
        compiler_params=_params(("parallel", "parallel", "arbitrary", "arbitrary")),
        name="rwkv7_chunked",
    )(*args)


def _mlstm_body(*refs, tb_rows, nseq, use_rope, has_init, has_out, n_alias):
    refs = list(refs)
    q_ref, k_ref, v_ref, g_ref, gt_ref = [refs.pop(0) for _ in range(5)]
    cos_ref, sin_ref, perm_ref = [refs.pop(0) for _ in range(3)] if use_rope else (None,) * 3
    bi_ref, bf_ref = refs.pop(0), refs.pop(0)
    c0_ref, n0_ref, m0_ref = [refs.pop(0) for _ in range(3)] if has_init else (None,) * 3
    del refs[:n_alias]
    h_ref = refs.pop(0)
    cout_ref, nout_ref, mout_ref = [refs.pop(0) for _ in range(3)] if has_out else (None,) * 3
    sc_s, kv_s, nk_s, cprev_s, nprev_s, c_s, n_s, m_s = refs
    L = CHUNK
    nck = tb_rows // L
    ncs = nck // nseq
    b = pl.program_id(0)
    h = pl.program_id(1)
    d = pl.program_id(2)
    tb = pl.program_id(3)
    n_tb = pl.num_programs(3)
    bi = bi_ref[d, h]
    bf = bf_ref[d, h]
    ji = d * (2 * M_HEADS) + h
    jf = ji + M_HEADS

    @pl.when(tb == 0)
    def _():
        for q in range(nseq):
            if has_init:
                c_s[q] = c0_ref[q, 0, 0]
                n_s[q] = n0_ref[q, 0, 0]
                m_s[q] = jnp.full((1, M_HD), m0_ref[b * nseq + q, d, h], F32)
            else:
                c_s[q] = jnp.zeros((M_HD, M_HD), F32)
                n_s[q] = jnp.zeros((1, M_HD), F32)
                m_s[q] = jnp.zeros((1, M_HD), F32)

    g = g_ref[...]
    lane = lax.broadcasted_iota(jnp.int32, g.shape, 1)
    icol_all = jnp.sum(jnp.where(lane == ji, g, 0.0), axis=1, keepdims=True) + bi
    fcol_all = _log_sigmoid(jnp.sum(jnp.where(lane == jf, g, 0.0), axis=1, keepdims=True) + bf)

    ti = lax.broadcasted_iota(jnp.int32, (L, L), 0)
    si = lax.broadcasted_iota(jnp.int32, (L, L), 1)
    sgn = 1 - 2 * d
    incl = (ti - si) * sgn >= 0
    incl_t = (si - ti) * sgn >= 0
    row128 = lambda x: jnp.broadcast_to(x, (1, M_HD))

    ch = []
    for c in range(nck):
        rows = slice(L * c, L * (c + 1))
        irow = gt_ref[c, pl.ds(ji, 1), :] + bi
        frow = _log_sigmoid(gt_ref[c, pl.ds(jf, 1), :] + bf)
        bsum_col = jnp.sum(jnp.where(incl, frow, 0.0), axis=1, keepdims=True)
        bsum_row = jnp.sum(jnp.where(incl_t, fcol_all[rows], 0.0), axis=0, keepdims=True)
        blast = jnp.sum(frow, axis=1, keepdims=True)
        wlog = blast - bsum_col + icol_all[rows]
        dmat = jnp.where(incl, bsum_col - bsum_row + irow, NEG)
        sc_s[c, 0:1, :] = row128(blast)
        sc_s[c, 1:2, :] = row128(jnp.max(wlog, axis=0, keepdims=True))
        ch.append(dict(rows=rows, bsum_col=bsum_col, blast=blast, wlog=wlog, dmat=dmat,
                       dmax=jnp.max(dmat, axis=1, keepdims=True)))

    def m_step(i, ms):
        ci = jnp.where(d == 0, i, ncs - 1 - i)
        new = []
        for q, m in enumerate(ms):
            cq = q * ncs + ci
            m_new = jnp.maximum(sc_s[cq, 0:1, :] + m, sc_s[cq, 1:2, :])
            sc_s[cq, 2:3, :] = m
            sc_s[cq, 3:4, :] = m_new
            new.append(m_new)
        return tuple(new)

    m_fin = lax.fori_loop(0, ncs, m_step, tuple(m_s[q] for q in range(nseq)))

    for c, cd in enumerate(ch):
        rows = cd['rows']
        cd['q'], cd['k'] = q_ref[rows, :], k_ref[rows, :]
        cd['v'] = v_ref[rows, :]
        cd['m_prev'] = sc_s[c, 2:3, 0:1]
        cd['m_new'] = sc_s[c, 3:4, 0:1]
    if use_rope:
        perm = perm_ref[...]
        rot = [(_dot_lhs2(cd['q'], perm), _dot_lhs2(cd['k'], perm)) for cd in ch]
        for cd, (rq, rk) in zip(ch, rot):
            cos, sin = cos_ref[cd['rows'], :], sin_ref[cd['rows'], :]
            cd['q'] = cd['q'] * cos + rq * sin
            cd['k'] = cd['k'] * cos + rk * sin
    for cd in ch:
        cd['k'] = cd['k'] * (M_HD ** -0.5)
    for cd in ch:
        cd['qk'] = _dot(cd['q'], cd['k'], NT)
    for c, cd in enumerate(ch):
        kw = cd['k'] * jnp.exp(cd['wlog'] - cd['m_new'])
        kv_s[c] = _dot3(kw, cd['v'], TN)
        nk_s[c] = jnp.sum(kw, axis=0, keepdims=True)
        sc_s[c, 4:5, :] = row128(jnp.exp(cd['blast'] + cd['m_prev'] - cd['m_new']))
    for cd in ch:
        inter = cd['bsum_col'] + cd['m_prev']
        cd['mt'] = jnp.maximum(inter, cd['dmax'])
        cd['iw'] = jnp.exp(inter - cd['mt'])
        sm = cd['qk'] * jnp.exp(cd['dmat'] - cd['mt'])
        cd['ssum'] = jnp.sum(sm, axis=1, keepdims=True)
        cd['sv'] = _dot(sm, cd['v'])

    def c_step(i, carry):
        ci = jnp.where(d == 0, i, ncs - 1 - i)
        new = []
        for q, (c_st, n_st) in enumerate(carry):
            cq = q * ncs + ci
            dec = sc_s[cq, 4:5, 0:1]
            cprev_s[cq] = c_st
            nprev_s[cq] = n_st
            new.append((dec * c_st + kv_s[cq], dec * n_st + nk_s[cq]))
        return tuple(new)

    st_fin = lax.fori_loop(0, ncs, c_step, tuple((c_s[q], n_s[q]) for q in range(nseq)))
    for q in range(nseq):
        c_s[q], n_s[q] = st_fin[q]
        m_s[q] = m_fin[q]

    qc_prev = [_dot(cd['q'], cprev_s[c]) for c, cd in enumerate(ch)]
    for c, cd in enumerate(ch):
        num = cd['iw'] * qc_prev[c] + cd['sv']
        den = cd['iw'] * jnp.sum(cd['q'] * nprev_s[c], axis=1, keepdims=True) + cd['ssum']
        h_ref[0, cd['rows'], :] = num / jnp.maximum(jnp.abs(den), jnp.exp(-cd['mt']))

    if has_out:
        @pl.when(tb == n_tb - 1)
        def _():
            for q in range(nseq):
                cout_ref[q, 0, 0, 0] = st_fin[q][0]
                nout_ref[q, 0, 0, 0] = st_fin[q][1]
                mout_ref[q, 0, 0, 0] = m_fin[q]


def _mlstm_call(z, gt, rope, b_i, b_f, init, prev, *, row0, nb, t, tb, layer=None, prev_state=None):
    nseq = max(tb // t, 1)
    n_tb = max(t // tb, 1)
    blk0 = row0 // tb
    cpb = tb // CHUNK
    use_rope = rope is not None
    has_init = init is not None

    def rowblk(b, d, i):
        return blk0 + b * n_tb + jnp.where(d == 0, i, n_tb - 1 - i)

    def zspec(cb):
        return pl.BlockSpec((tb, LANE), lambda b, h, d, i: (rowblk(b, d, i), cb(h)))

    smem = pl.BlockSpec(memory_space=pltpu.SMEM)
    in_specs = [zspec(lambda h: ZB_MQ + h), zspec(lambda h: ZB_MK + h), zspec(lambda h: ZB_MV + h),
                zspec(lambda h: ZB_MIF),
                pl.BlockSpec((cpb, 16, CHUNK), lambda b, h, d, i: (rowblk(b, d, i), 0, 0))]
    args = [z, z, z, z, gt]
    if use_rope:
        tspec = pl.BlockSpec((tb, LANE), lambda b, h, d, i: (jnp.where(d == 0, i, n_tb - 1 - i), 0))
        in_specs += [tspec, tspec, pl.BlockSpec((M_HD, M_HD), lambda b, h, d, i: (0, 0))]
        args += list(rope)
    in_specs += [smem, smem]
    args += [b_i, b_f]
    cspec = pl.BlockSpec((nseq, 1, 1, M_HD, M_HD), lambda b, h, d, i: (b, d, h, 0, 0))
    nspec = pl.BlockSpec((nseq, 1, 1, 1, M_HD), lambda b, h, d, i: (b, d, h, 0, 0))
    if has_init:
        in_specs += [cspec, nspec, smem]
        args += list(init)
    has_out = layer is not None
    aliases = _alias_prev(in_specs, args, prev)
    aliases.update(_alias_prev(in_specs, args, prev_state, first_out=1))
    out_shape = [jax.ShapeDtypeStruct((2, z.shape[0], HW), F32)]
    out_specs = [pl.BlockSpec((1, tb, LANE), lambda b, h, d, i: (d, rowblk(b, d, i), h))]
    if has_out:
        vec_shape = jax.ShapeDtypeStruct((nb, DEPTH, 2, M_HEADS, 1, M_HD), F32)
        out_shape += [jax.ShapeDtypeStruct((nb, DEPTH, 2, M_HEADS, M_HD, M_HD), F32), vec_shape, vec_shape]
        vec_spec = pl.BlockSpec((nseq, 1, 1, 1, 1, M_HD), lambda b, h, d, i: (b, layer, d, h, 0, 0))
        out_specs += [pl.BlockSpec((nseq, 1, 1, 1, M_HD, M_HD), lambda b, h, d, i: (b, layer, d, h, 0, 0)),
                      vec_spec, vec_spec]
    return pl.pallas_call(
        functools.partial(_mlstm_body, tb_rows=tb, nseq=nseq, use_rope=use_rope, has_init=has_init,
                          has_out=has_out, n_alias=len(aliases)),
        out_shape=out_shape,
        grid=(nb // nseq, M_HEADS, 2, n_tb),
        in_specs=in_specs,
        out_specs=out_specs,
        input_output_aliases=aliases,
        scratch_shapes=[pltpu.VMEM((cpb, 8, M_HD), F32),
                        pltpu.VMEM((cpb, M_HD, M_HD), F32), pltpu.VMEM((cpb, 1, M_HD), F32),
                        pltpu.VMEM((cpb, M_HD, M_HD), F32), pltpu.VMEM((cpb, 1, M_HD), F32),
                        pltpu.VMEM((nseq, M_HD, M_HD), F32), pltpu.VMEM((nseq, 1, M_HD), F32),
                        pltpu.VMEM((nseq, 1, M_HD), F32)],
        compiler_params=_params(("parallel", "parallel", "arbitrary", "arbitrary")),
        name="mlstm_chunkwise",
    )(*args)


def _ctx_attn_body(q_ref, k_ref, v_ref, *refs):
    y_ref, ko_ref, vo_ref = refs[-3:]
    q = q_ref[...] * (N_HD ** -0.5)
    k = k_ref[...]
    v = v_ref[...]
    heads = [slice(N_HD * h, N_HD * (h + 1)) for h in range(N_HEADS)]
    for h, sl in enumerate(heads):
        ko_ref[0, 0, h] = k[:, sl]
        vo_ref[0, 0, h] = v[:, sl]
    s = [_dot(q[:, sl], k[:, sl], NT) for sl in heads]
    p = [jnp.exp(s_ - jnp.max(s_, axis=-1, keepdims=True)) for s_ in s]
    o = [_dot(p_, v[:, sl]) for p_, sl in zip(p, heads)]
    for o_, p_, sl in zip(o, p, heads):
        y_ref[:, sl] = o_ / jnp.sum(p_, axis=-1, keepdims=True)


def _ctx_attn_call(z, prev_kv, *, nb, t, layer):
    zspec = lambda cb: pl.BlockSpec((t, HW), lambda b: (b, cb // (HW // LANE)))
    kv_spec = pl.BlockSpec((1, 1, N_HEADS, t, N_HD), lambda b: (b, layer, 0, 0, 0))
    kv_shape = jax.ShapeDtypeStruct((nb, DEPTH, N_HEADS, t, N_HD), F32)
    in_specs = [zspec(ZB_NQ), zspec(ZB_NK), zspec(ZB_NV)]
    args = [z, z, z]
    aliases = _alias_prev(in_specs, args, prev_kv, first_out=1)
    return pl.pallas_call(
        _ctx_attn_body,
        out_shape=[jax.ShapeDtypeStruct((z.shape[0], HW), F32), kv_shape, kv_shape],
        grid=(nb,),
        in_specs=in_specs,
        out_specs=[pl.BlockSpec((t, HW), lambda b: (b, 0)), kv_spec, kv_spec],
        input_output_aliases=aliases,
        compiler_params=_params(("parallel",)),
        name="context_attention",
    )(*args)


def _natten_body(q_ref, k_ref, v_ref, kc_ref, vc_ref, tab_ref, prev_ref, y_ref, *, rows_per_step, rows_n):
    del prev_ref
    rb = pl.program_id(2)
    n_loc = WIN_ROWS * GRID_W

    unroll = 4

    def rows(it, carry):
        work = []
        for u in range(unroll):
            i = it * unroll + u
            r = rb * rows_per_step + i
            rstart = jnp.clip(r - WIN_ROWS // 2, 0, rows_n - WIN_ROWS)
            qrows = pl.ds(pl.multiple_of(i * GRID_W, GRID_W), GRID_W)
            krows = pl.ds(pl.multiple_of(rstart * GRID_W, GRID_W), n_loc)
            work += [(qrows, krows, r - rstart, h, slice(N_HD * h, N_HD * (h + 1))) for h in range(2)]
        q = [q_ref[qr, sl] * (N_HD ** -0.5) for qr, _, _, _, sl in work]
        s_loc = [_dot(q_, k_ref[kr, sl], NT) + tab_ref[h, var] for q_, (_, kr, var, h, sl) in zip(q, work)]
        s_ctx = [_dot(q_, kc_ref[0, h], NT) for q_, (_, _, _, h, _) in zip(q, work)]
        m = [jnp.maximum(jnp.max(a, axis=-1, keepdims=True), jnp.max(c, axis=-1, keepdims=True))
             for a, c in zip(s_loc, s_ctx)]
        p_loc = [jnp.exp(a - m_) for a, m_ in zip(s_loc, m)]
        p_ctx = [jnp.exp(c - m_) for c, m_ in zip(s_ctx, m)]
        o = [_dot(a, v_ref[kr, sl]) + _dot(c, vc_ref[0, h])
             for a, c, (_, kr, _, h, sl) in zip(p_loc, p_ctx, work)]
        for o_, a, c, (qr, _, _, _, sl) in zip(o, p_loc, p_ctx, work):
            y_ref[qr, sl] = o_ / (jnp.sum(a, axis=-1, keepdims=True) + jnp.sum(c, axis=-1, keepdims=True))
        return carry

    lax.fori_loop(0, rows_per_step // unroll, rows, 0)


def _natten_call(z, k_ctx, v_ctx, tab, prev, *, row0, nb, t, rows_per_step=8):
    rows_n = t // GRID_W
    tq = rows_per_step * GRID_W
    qblk0 = row0 // tq
    sblk0 = row0 // t
    n_rb = rows_n // rows_per_step
    past = k_ctx.shape[2]
    kv_spec = lambda cb: pl.BlockSpec((t, LANE), lambda b, hp, rb: (sblk0 + b, cb + hp))
    cache_spec = pl.BlockSpec((1, 2, past, N_HD), lambda b, hp, rb: (b, hp, 0, 0))
    return pl.pallas_call(
        functools.partial(_natten_body, rows_per_step=rows_per_step, rows_n=rows_n),
        out_shape=jax.ShapeDtypeStruct((z.shape[0], HW), F32),
        grid=(nb, N_HEADS // 2, n_rb),
        in_specs=[pl.BlockSpec((tq, LANE), lambda b, hp, rb: (qblk0 + b * n_rb + rb, ZB_NQ + hp)),
                  kv_spec(ZB_NK), kv_spec(ZB_NV), cache_spec, cache_spec,
                  pl.BlockSpec((2, WIN_ROWS, GRID_W, WIN_ROWS * GRID_W), lambda b, hp, rb: (hp, 0, 0, 0)),
                  pl.BlockSpec(memory_space=pl.ANY)],
        out_specs=pl.BlockSpec((tq, LANE), lambda b, hp, rb: (qblk0 + b * n_rb + rb, hp)),
        input_output_aliases={6: 0},
        compiler_params=_params(("parallel", "parallel", "arbitrary")),
        name="neighbourhood_attention",
    )(z, z, z, k_ctx, v_ctx, tab, prev)


def _natten_table(rpb):
    cq = jnp.arange(GRID_W)[:, None]
    ck = jnp.arange(GRID_W)[None, :]
    dc = jnp.clip(ck - cq, -(WIN_COLS - 1), WIN_COLS - 1) + (WIN_COLS - 1)
    cstart = jnp.clip(cq - WIN_COLS // 2, 0, GRID_W - WIN_COLS)
    in_band = (ck >= cstart) & (ck < cstart + WIN_COLS)
    onehot = ((dc[None] == jnp.arange(2 * WIN_COLS - 1)[:, None, None]) & in_band[None]).astype(F32)
    cols = jnp.einsum('hrd,dqk->hrqk', rpb, onehot, precision=lax.Precision.HIGHEST)
    cols = cols + jnp.where(in_band, 0.0, NEG)
    tab = jnp.stack([cols[:, WIN_ROWS - 1 - var:2 * WIN_ROWS - 1 - var] for var in range(WIN_ROWS)], axis=1)
    return tab.transpose(0, 1, 3, 2, 4).reshape(rpb.shape[0], WIN_ROWS, GRID_W, WIN_ROWS * GRID_W)


def _group_norm(y, avg_bf, eps):
    mu = _dot_lhs2(y, avg_bf)
    yc = y - mu
    var = _dot_lhs2(yc * yc, avg_bf)
    return yc * lax.rsqrt(var + eps)


def _merge_body(*refs, with_router):
    refs = list(refs)
    (x_ref, mod_ref, yr_ref, bv_ref, hm_ref, yn_ref, gl_ref, mo_ref, g0_ref, g1_ref, g2_ref,
     avg64_ref, avg128_ref, rg2_ref, lnw_ref, lnb_ref, mlw_ref, wor_ref, wom_ref, won_ref, wout_ref,
     gpost_ref, gpre_ref) = [refs.pop(0) for _ in range(23)]
    wr_ref, br_ref = (refs.pop(0), refs.pop(0)) if with_router else (None, None)
    x1_ref, hin_ref = refs.pop(0), refs.pop(0)
    hin_hi_ref, sel_ref = (refs.pop(0), refs.pop(0)) if with_router else (None, None)
    m = mod_ref[0]
    yr = yr_ref[0] + yr_ref[1]
    yn_r = _group_norm(yr, avg64_ref[...], RWKV_GN_EPS) * lnw_ref[...] + lnb_ref[...]
    g = _dot(jax.nn.sigmoid(gl_ref[...]), rg2_ref[...])
    out_r = (yn_r + bv_ref[0] + bv_ref[1]) * g
    hn = _group_norm(hm_ref[0] + hm_ref[1], avg128_ref[...], NORM_EPS) * mlw_ref[...]
    out_m = jax.nn.sigmoid(mo_ref[...]) * hn
    merged = (jax.nn.sigmoid(g0_ref[...]) * _dot(out_r, wor_ref[...])
              + jax.nn.sigmoid(g1_ref[...]) * _dot(out_m, wom_ref[...])
              + jax.nn.sigmoid(g2_ref[...]) * _dot(yn_ref[...], won_ref[...]))
    o = _dot(merged, wout_ref[...])
    x1 = x_ref[...] + m[2:3] * _rms(o, gpost_ref[...])
    x1_ref[...] = x1
    hin = _rms(x1, gpre_ref[...]) * (1.0 + m[4:5]) + m[3:4]
    if with_router:
        sel_ref[...] = _top2_select(_dot3(hin, wr_ref[...]) + br_ref[...])
        hin_ref[...] = hin[:, :D // 2]
        hin_hi_ref[...] = hin[:, D // 2:]
    else:
        hin_ref[...] = hin.astype(BF16)


def _merge_call(x, mod3, yr, bv, hm, yn, z, W, n_ctx, lat_t, router, tm=256):
    ntok = x.shape[0]
    midx = _mod_index(tm, n_ctx, lat_t)
    with_router = router is not None
    row = lambda w: pl.BlockSpec((tm, w), lambda i: (i, 0))
    dirs = pl.BlockSpec((2, tm, HW), lambda i: (0, i, 0))
    zspec = lambda w, cb: pl.BlockSpec((tm, w), lambda i: (i, cb))
    full = lambda a: pl.BlockSpec(a.shape, lambda i: (0,) * a.ndim)
    consts = [W['avg64'], W['avg128'], W['rg2'], W['lnw'], W['lnb'], W['mlw'], W['wor'], W['wom'], W['won'],
              W['wout'], W['gpost'], W['gpre']]
    in_specs = [row(D), pl.BlockSpec((1, 8, D), lambda i: (midx(i), 0, 0)), dirs, dirs, dirs, row(HW),
                zspec(LANE, ZB_GL), zspec(HW, ZB_MO // 4),
                zspec(D, ZB_GATE // 8), zspec(D, ZB_GATE // 8 + 1), zspec(D, ZB_GATE // 8 + 2)]
    in_specs += [full(a) for a in consts]
    args = [x, mod3, yr, bv, hm, yn, z, z, z, z, z] + consts
    if with_router:
        half = jax.ShapeDtypeStruct((ntok, D // 2), F32)
        out_shape = [jax.ShapeDtypeStruct((ntok, D), F32), half, half]
        out_specs = [row(D), row(D // 2), row(D // 2)]
    else:
        out_shape = [jax.ShapeDtypeStruct((ntok, D), F32), jax.ShapeDtypeStruct((ntok, D), BF16)]
        out_specs = [row(D), row(D)]
    if with_router:
        in_specs += [full(router[0]), full(router[1])]
        args += list(router)
        out_shape.append(jax.ShapeDtypeStruct((ntok, LANE), F32))
        out_specs.append(row(LANE))
    return pl.pallas_call(
        functools.partial(_merge_body, with_router=with_router),
        out_shape=out_shape,
        grid=(ntok // tm,),
        in_specs=in_specs,
        out_specs=out_specs,
        compiler_params=_params(("parallel",)),
        name="branch_merge",
    )(*args)


def _top2_select(lg):
    lane = lax.broadcasted_iota(jnp.int32, lg.shape, 1)
    m1 = jnp.max(lg, axis=1, keepdims=True)
    i1 = jnp.min(jnp.where(lg == m1, lane, LANE), axis=1, keepdims=True)
    lg2 = jnp.where(lane == i1, -jnp.inf, lg)
    m2 = jnp.max(lg2, axis=1, keepdims=True)
    i2 = jnp.min(jnp.where(lg2 == m2, lane, LANE), axis=1, keepdims=True)
    e2 = jnp.exp(m2 - m1)
    den = 1.0 + e2
    return jnp.where(lane == 0, i1.astype(F32),
                     jnp.where(lane == 1, i2.astype(F32),
                               jnp.where(lane == 2, 1.0 / den, jnp.where(lane == 3, e2 / den, 0.0))))


def _ffn_body(h_ref, x1_ref, mod_ref, gpost_ref, wg_ref, wu_ref, wd_ref, o_ref, acc_s):
    j = pl.program_id(1)

    @pl.when(j == 0)
    def _():
        acc_s[...] = jnp.zeros_like(acc_s)

    hin = h_ref[...]
    hg = jnp.dot(hin, wg_ref[...], preferred_element_type=F32)
    hu = jnp.dot(hin, wu_ref[...], preferred_element_type=F32)
    acc_s[...] += _dot(hg * jax.nn.sigmoid(hg) * hu, wd_ref[...])

    @pl.when(j == pl.num_programs(1) - 1)
    def _():
        m = mod_ref[0]
        o_ref[...] = x1_ref[...] + m[5:6] * _rms(acc_s[...], gpost_ref[...])


def _ffn_call(hin, x1, mod3, gpost, wg, wu, wd, n_ctx, lat_t, tm=512, tf=2816):
    ntok = hin.shape[0]
    dff = wg.shape[1]
    midx = _mod_index(tm, n_ctx, lat_t)
    return pl.pallas_call(
        _ffn_body,
        out_shape=jax.ShapeDtypeStruct((ntok, D), F32),
        grid=(ntok // tm, dff // tf),
        in_specs=[pl.BlockSpec((tm, D), lambda i, j: (i, 0)),
                  pl.BlockSpec((tm, D), lambda i, j: (i, 0)),
                  pl.BlockSpec((1, 8, D), lambda i, j: (midx(i), 0, 0)),
                  pl.BlockSpec((1, D), lambda i, j: (0, 0)),
                  pl.BlockSpec((D, tf), lambda i, j: (0, j)),
                  pl.BlockSpec((D, tf), lambda i, j: (0, j)),
                  pl.BlockSpec((tf, D), lambda i, j: (j, 0))],
        out_specs=pl.BlockSpec((tm, D), lambda i, j: (i, 0)),
        scratch_shapes=[pltpu.VMEM((tm, D), F32)],
        compiler_params=_params(("parallel", "arbitrary")),
        name="dense_swiglu",
    )(hin, x1, mod3, gpost, wg, wu, wd)


MOE_TILE = 512
MOE_TF = 1792
CMB_TILE = 128
SLAB_ALIGN = 8
SLAB = CMB_TILE + SLAB_ALIGN


def _moe_plan(sel, ntok):
    i32 = jnp.int32
    e1, e2 = sel[:, 0].astype(i32), sel[:, 1].astype(i32)
    ar = jnp.arange(ntok, dtype=i32)
    ex = jnp.arange(N_EXPERTS, dtype=i32)
    oh = ((e1[:, None] == ex) | (e2[:, None] == ex)).astype(i32)
    csum = jnp.cumsum(oh, axis=0)
    total = csum[-1]
    gsz = (total + MOE_TILE - 1) // MOE_TILE * MOE_TILE
    gend = jnp.cumsum(gsz)
    gstart = gend - gsz
    dstart = jnp.cumsum(total) - total
    key, wts = lax.sort((jnp.concatenate([e1 * ntok + ar, e2 * ntok + ar]),
                         jnp.concatenate([sel[:, 2], sel[:, 3]])), num_keys=1)
    tok_d = key % ntok
    n_pad = 2 * ntok + (N_EXPERTS + 1) * MOE_TILE
    p = jnp.arange(n_pad, dtype=i32)
    ge = jnp.minimum(jnp.sum((p[:, None] >= gend[None, :]).astype(i32), axis=1), N_EXPERTS - 1)
    pick = lambda tab: jnp.sum(jnp.where(ge[:, None] == ex, tab[None, :], 0), axis=1)
    q = p - pick(gstart)
    live = (q < pick(total)) & (p < gend[-1])
    src = jnp.clip(pick(dstart) + q, 0, 2 * ntok - 1)
    tok_p = jnp.where(live, tok_d[src], 0)
    w_p = jnp.where(live, wts[src], 0.0)
    tstart = jnp.arange(n_pad // MOE_TILE, dtype=i32) * MOE_TILE
    texp = jnp.minimum(jnp.sum((tstart[:, None] >= gend[None, :]).astype(i32), axis=1), N_EXPERTS - 1)
    tval = (tstart < gend[-1]).astype(i32)
    r0 = (csum - oh)[::CMB_TILE]
    r1 = jnp.concatenate([r0[1:], total[None]], axis=0)
    alo = (gstart[None, :] + r0).reshape(-1)
    ahi = (gstart[None, :] + r1).reshape(-1)
    return dict(tok=tok_p, w=w_p.reshape(n_pad, 1), texp=texp, tval=tval,
                a8=alo // SLAB_ALIGN, alo=alo, ahi=ahi, n_pad=n_pad)


def _moe_gather_body(tok_ref, src_ref, o_ref):
    base = pl.program_id(0) * MOE_TILE

    group = 16

    def gather(g, carry):
        r0 = pl.multiple_of(g * group, group)
        rows = [src_ref[pl.ds(tok_ref[base + r0 + u], 1), :] for u in range(group)]
        o_ref[pl.ds(r0, group), :] = jnp.concatenate(rows, axis=0).astype(BF16)
        return carry

    lax.fori_loop(0, MOE_TILE // group, gather, 0)


def _moe_gather_call(src, plan):
    n_pad = plan['n_pad']
    width = src.shape[1]
    grid_spec = pltpu.PrefetchScalarGridSpec(
        num_scalar_prefetch=1,
        grid=(n_pad // MOE_TILE,),
        in_specs=[pl.BlockSpec(memory_space=pltpu.VMEM)],
        out_specs=pl.BlockSpec((MOE_TILE, width), lambda i, tk: (i, 0)))
    return pl.pallas_call(
        _moe_gather_body,
        out_shape=jax.ShapeDtypeStruct((n_pad, width), BF16),
        grid_spec=grid_spec,
        compiler_params=_params(("arbitrary",)),
        name="moe_row_gather",
    )(plan['tok'], src)


def _moe_group_body(texp_ref, tval_ref, xa_ref, xb_ref, ws_ref, wg_ref, wu_ref, wd_ref, ys_ref, acc_s):
    del texp_ref
    i = pl.program_id(0)
    j = pl.program_id(1)
    half = D // 2

    @pl.when(j == 0)
    def _():
        acc_s[...] = jnp.zeros_like(acc_s)

    @pl.when(tval_ref[i] > 0)
    def _():
        xa, xb = xa_ref[...], xb_ref[...]
        mm = lambda u, ref, rows: jnp.dot(u, ref[0, rows, :], preferred_element_type=F32)
        hg = mm(xa, wg_ref, slice(0, half)) + mm(xb, wg_ref, slice(half, D))
        hu = mm(xa, wu_ref, slice(0, half)) + mm(xb, wu_ref, slice(half, D))
        acc_s[...] += _dot(hg * jax.nn.sigmoid(hg) * hu, wd_ref[0])

    @pl.when(j == pl.num_programs(1) - 1)
    def _():
        ys_ref[...] = acc_s[...] * ws_ref[...]


def _moe_group_call(xa, xb, plan, wg, wu, wd):
    n_pad = plan['n_pad']
    dff = wg.shape[2]
    xspec = pl.BlockSpec((MOE_TILE, D // 2), lambda i, j, te, tv: (i, 0))
    grid_spec = pltpu.PrefetchScalarGridSpec(
        num_scalar_prefetch=2,
        grid=(n_pad // MOE_TILE, dff // MOE_TF),
        in_specs=[xspec, xspec,
                  pl.BlockSpec((MOE_TILE, 1), lambda i, j, te, tv: (i, 0)),
                  pl.BlockSpec((1, D, MOE_TF), lambda i, j, te, tv: (te[i], 0, j)),
                  pl.BlockSpec((1, D, MOE_TF), lambda i, j, te, tv: (te[i], 0, j)),
                  pl.BlockSpec((1, MOE_TF, D), lambda i, j, te, tv: (te[i], j, 0))],
        out_specs=pl.BlockSpec((MOE_TILE, D), lambda i, j, te, tv: (i, 0)),
        scratch_shapes=[pltpu.VMEM((MOE_TILE, D), F32)])
    return pl.pallas_call(
        _moe_group_body,
        out_shape=jax.ShapeDtypeStruct((n_pad, D), F32),
        grid_spec=grid_spec,
        compiler_params=_params(("parallel", "arbitrary")),
        name="moe_expert_swiglu",
    )(plan['texp'], plan['tval'], xa, xb, plan['w'], wg, wu, wd)


def _moe_combine_body(a8_ref, alo_ref, ahi_ref, *refs, tile0):
    ys_refs, tk_refs = refs[:N_EXPERTS], refs[N_EXPERTS:2 * N_EXPERTS]
    x1_ref, mod_ref, gpost_ref, o_ref = refs[2 * N_EXPERTS:]
    t = pl.program_id(0) + tile0
    tloc = t * CMB_TILE + lax.broadcasted_iota(jnp.int32, (SLAB, CMB_TILE), 1)
    srow = lax.broadcasted_iota(jnp.int32, (SLAB, 1), 0)
    f = jnp.zeros((CMB_TILE, D), F32)
    for e in range(N_EXPERTS):
        k = t * N_EXPERTS + e
        row = a8_ref[k] * SLAB_ALIGN + srow
        tok = jnp.where((row >= alo_ref[k]) & (row < ahi_ref[k]), tk_refs[e][...], -1)
        onehot = jnp.where(tok == tloc, 1.0, 0.0).astype(BF16)
        f = f + _dot_rhs2(onehot, ys_refs[e][...], TN)
    m = mod_ref[0]
    o_ref[...] = x1_ref[...] + m[5:6] * _rms(f, gpost_ref[...])


def _moe_combine_call(ys, plan, x1, mod3, gpost, n_ctx, lat_t, row0, nrows):
    midx = _mod_index(CMB_TILE, n_ctx, lat_t)
    tile0 = row0 // CMB_TILE

    def slab(w, e):
        return pl.BlockSpec((pl.Element(SLAB), pl.Element(w)),
                            lambda t, a8, lo, hi: (a8[(t + tile0) * N_EXPERTS + e] * SLAB_ALIGN, 0))

    experts = range(N_EXPERTS)
    grid_spec = pltpu.PrefetchScalarGridSpec(
        num_scalar_prefetch=3,
        grid=(nrows // CMB_TILE,),
        in_specs=[slab(D, e) for e in experts] + [slab(1, e) for e in experts] + [
            pl.BlockSpec((CMB_TILE, D), lambda t, a8, lo, hi: (t + tile0, 0)),
            pl.BlockSpec((1, 8, D), lambda t, a8, lo, hi: (midx(t + tile0), 0, 0)),
            pl.BlockSpec((1, D), lambda t, a8, lo, hi: (0, 0))],
        out_specs=pl.BlockSpec((CMB_TILE, D), lambda t, a8, lo, hi: (t, 0)))
    tok2d = plan['tok'].reshape(-1, 1)
    return pl.pallas_call(
        functools.partial(_moe_combine_body, tile0=tile0),
        out_shape=jax.ShapeDtypeStruct((nrows, D), F32),
        grid_spec=grid_spec,
        compiler_params=_params(("parallel",)),
        name="moe_combine",
    )(plan['a8'], plan['alo'], plan['ahi'], *([ys] * N_EXPERTS), *([tok2d] * N_EXPERTS), x1, mod3, gpost)


def _rope_tables(t):
    i = jnp.arange(M_HD)
    half, pair, f = i // 64, (i % 64) // 32, i % 32
    tt = jnp.arange(t)
    pos = jnp.stack([tt // GRID_W, tt % GRID_W], axis=-1).astype(F32)
    inv = ROPE_BASE ** (-f.astype(F32) / 32)
    ang = pos[:, half] * inv[None, :]
    src = jnp.where(pair == 0, i + 32, i - 32)
    perm = jnp.where(i[:, None] == src[None, :], jnp.where(pair == 0, -1.0, 1.0)[None, :], 0.0)
    return jnp.cos(ang), jnp.sin(ang), perm.astype(BF16)


def _pad_in_proj(w):
    off = {}
    o = 0
    for name, width in (('r', 512), ('k', 512), ('v', 512), ('wl', 128), ('al', 128), ('gl', 128),
                        ('mq', 512), ('mk', 512), ('mv', 512), ('mo', 512), ('mif', 16),
                        ('nq', 512), ('nk', 512), ('nv', 512), ('gate', 3072)):
        off[name] = (o, o + width)
        o += width
    wb = w.astype(BF16)
    col = lambda n: wb[:, off[n][0]:off[n][1]]
    parts = [col(n) for n in ('r', 'k', 'v', 'mq', 'mk', 'mv', 'mo', 'nq', 'nk', 'nv', 'gate', 'wl', 'al', 'gl', 'mif')]
    parts.append(jnp.zeros((w.shape[0], LANE - 16), BF16))
    return jnp.concatenate(parts, axis=1)


def _block_avg(width, group):
    i = jnp.arange(width) // group
    return jnp.where(i[:, None] == i[None, :], 1.0 / group, 0.0).astype(BF16)


def _zero_pad_rows(w2):
    z = jnp.zeros_like(w2[0])
    return jnp.stack([jnp.concatenate([w2[0], z], axis=0), jnp.concatenate([z, w2[1]], axis=0)])


def kernel(x_prompt, x_sample, state_rwkv, state_mlstm_C, state_mlstm_n, state_mlstm_m, cache_nat_k, cache_nat_v,
           c, c_ctx, w_mod, b_mod, g_pre_mix, g_post_mix, g_pre_ffn, g_post_ffn, w_in,
           rw_w0, rw_w2, rw_a0, rw_a2, rw_g2, rw_k_k, rw_k_a, rw_r_k, rw_ln_w, rw_ln_b,
           ml_b_i, ml_b_f, ml_norm_w, nat_rpb, w_o_rwkv, w_o_mlstm, w_o_nat, w_out,
           ff_w_gate, ff_w_up, ff_w_down, moe_w_router, moe_b_router, moe_w_gate, moe_w_up, moe_w_down):
    cb, ct = x_prompt.shape[:2]
    lb, lt = x_sample.shape[:2]
    n_ctx, n_lat = cb * ct, lb * lt
    x = jnp.concatenate([x_prompt.reshape(n_ctx, D), x_sample.reshape(n_lat, D)], axis=0)
    cvec = jnp.zeros((8, D), F32).at[0].set(c_ctx).at[1:1 + lb].set(c)
    rope = _rope_tables(lt)
    avg64, avg128 = _block_avg(HW, R_HD), _block_avg(HW, M_HD)
    s_rw = ml_st = kv_c = None
    for l in range(DEPTH):
        mod = _mod_call(cvec, w_mod[l], b_mod[l])
        mod3 = jnp.pad(mod[:1 + lb].reshape(1 + lb, 6, D), ((0, 0), (0, 2), (0, 0)))
        z = _proj_call(x, mod3, g_pre_mix[l], _pad_in_proj(w_in[l]), n_ctx, lt)

        rp = dict(w0=rw_w0[l].reshape(2, 1, HW), w2=_zero_pad_rows(rw_w2[l]),
                  a0=rw_a0[l].reshape(2, 1, HW), a2=_zero_pad_rows(rw_a2[l]),
                  kk=rw_k_k[l].reshape(1, HW), ka=rw_k_a[l].reshape(1, HW), rk=rw_r_k[l].reshape(1, HW))
        yr, bv, s_rw = _rwkv_call(z, rp, None, None, row0=0, nb=cb, t=ct, tb=RW_BLOCK, layer=l,
                                  prev_state=None if s_rw is None else (s_rw,))
        yr, bv = _rwkv_call(z, rp, state_rwkv[:, l], (yr, bv), row0=n_ctx, nb=lb, t=lt, tb=RW_BLOCK)

        gt = z[:, ZB_MIF * LANE:ZB_MIF * LANE + 16].reshape(-1, CHUNK, 16).transpose(0, 2, 1)
        hm, *ml_st = _mlstm_call(z, gt, None, ml_b_i[l], ml_b_f[l], None, None,
                                 row0=0, nb=cb, t=ct, tb=ML_BLOCK, layer=l, prev_state=ml_st)
        init = (state_mlstm_C[:, l], state_mlstm_n[:, l][:, :, :, None, :], state_mlstm_m[:, l])
        hm, = _mlstm_call(z, gt, rope, ml_b_i[l], ml_b_f[l], init, (hm,),
                          row0=n_ctx, nb=lb, t=lt, tb=ML_BLOCK)

        yn, *kv_c = _ctx_attn_call(z, kv_c, nb=cb, t=ct, layer=l)
        yn = _natten_call(z, cache_nat_k[:, l], cache_nat_v[:, l], _natten_table(nat_rpb[l]), yn,
                          row0=n_ctx, nb=lb, t=lt)

        mw = dict(avg64=avg64, avg128=avg128, rg2=rw_g2[l].astype(BF16), lnw=rw_ln_w[l].reshape(1, HW),
                  lnb=rw_ln_b[l].reshape(1, HW), mlw=ml_norm_w[l].reshape(1, HW),
                  wor=w_o_rwkv[l].astype(BF16), wom=w_o_mlstm[l].astype(BF16), won=w_o_nat[l].astype(BF16),
                  wout=w_out[l].astype(BF16), gpost=g_post_mix[l].reshape(1, D), gpre=g_pre_ffn[l].reshape(1, D))
        j = l // 2
        if l % 2 == 0:
            router = None
        else:
            wr = jnp.pad(moe_w_router[j], ((0, 0), (0, LANE - N_EXPERTS)))
            br = jnp.pad(moe_b_router[j], (0, LANE - N_EXPERTS), constant_values=NEG).reshape(1, LANE)
            router = (wr, br)
        merged = _merge_call(x, mod3, yr, bv, hm, yn, z, mw, n_ctx, lt, router)
        gpost = g_post_ffn[l].reshape(1, D)
        if l % 2 == 0:
            x = _ffn_call(merged[1], merged[0], mod3, gpost, ff_w_gate[j].astype(BF16),
                          ff_w_up[j].astype(BF16), ff_w_down[j].astype(BF16), n_ctx, lt)
        else:
            plan = _moe_plan(merged[3], n_ctx + n_lat)
            xa, xb = _moe_gather_call(merged[1], plan), _moe_gather_call(merged[2], plan)
            ys = _moe_group_call(xa, xb, plan, moe_w_gate[j].astype(BF16), moe_w_up[j].astype(BF16),
                                 moe_w_down[j].astype(BF16))
            groups = [_moe_combine_call(ys, plan, merged[0], mod3, gpost, n_ctx, lt, r0, nr)
                      for r0, nr in ((0, n_ctx), (n_ctx, n_lat))]
            x = jnp.concatenate(groups, axis=0) if l + 1 < DEPTH else None
        if x is not None:
            groups = [x[:n_ctx], x[n_ctx:]]

    c_m, n_m, m_m = ml_st
    return (groups[0].reshape(cb, ct, D), groups[1].reshape(lb, lt, D),
            s_rw, c_m, n_m[:, :, :, :, 0, :], m_m[:, :, :, :, 0, 0], kv_c[0], kv_c[1])
```
